```python
import math
import jax, jax.numpy as jnp
from jax import lax
import numpy as np

D_MODEL = 1024
BATCH = 8
SEQ = 4096
DEPTH = 4

GRID_W = 64
CTX_LEN = 256

N_MIXERS = 3
N_ATTN_LAYERS = (DEPTH + 2) // 3
N_CONV_LAYERS = (DEPTH + 1) // 3
N_HYENA_LAYERS = DEPTH // 3

HEAD_DIM = 128
N_HEADS = D_MODEL // HEAD_DIM
N_KV_HEADS = N_HEADS // 4
GQA_GROUP = N_HEADS // N_KV_HEADS
Q_BLOCK = 128
ROPE_THETA = 10000.0
ROPE_PAIRS_PER_AXIS = HEAD_DIM // 4

CONV_WIDTH = 31

HYENA_ORDER = 2
SHORT_CONV_WIDTH = 3
FILTER_BANDS = 8
FILTER_EMB = 1 + 2 * FILTER_BANDS
FILTER_HIDDEN = 64
DECAY_TARGET = 1e-2
FAST_DECAY_PCT = 0.3
SLOW_DECAY_PCT = 1.5

N_GROUPS = 4
EXPERTS_PER_GROUP = 8
N_EXPERTS = N_GROUPS * EXPERTS_PER_GROUP
TOP_K = 2
D_EXPERT = D_MODEL // 4

EPS = 1e-6

kernel_name = "hybrid_attn_conformer_hyena_hmoe_prefix_dit"


def rmsnorm(x, g):
    xf = x.astype(jnp.float32)
    y = xf * lax.rsqrt(jnp.mean(xf * xf, axis=-1, keepdims=True) + EPS)
    return (y * g.astype(jnp.float32)).astype(x.dtype)


def layernorm(x, g, b):
    xf = x.astype(jnp.float32)
    mu = jnp.mean(xf, axis=-1, keepdims=True)
    xc = xf - mu
    y = xc * lax.rsqrt(jnp.mean(xc * xc, axis=-1, keepdims=True) + EPS)
    return (y * g.astype(jnp.float32) + b.astype(jnp.float32)).astype(x.dtype)


def modulated_norm(x, g, shift, scale):
    return rmsnorm(x, g) * (1.0 + scale) + shift


def depthwise_conv(x, w, b):
    k = w.shape[0]
    y = lax.conv_general_dilated(
        x, w[:, None, :], window_strides=(1,), padding=[((k - 1) // 2, k // 2)],
        dimension_numbers=("NWC", "WIO", "NWC"), feature_group_count=x.shape[-1])
    return y + b


def axial_angles(rows):
    row = jnp.repeat(jnp.arange(rows), GRID_W).astype(jnp.float32)
    col = jnp.tile(jnp.arange(GRID_W), rows).astype(jnp.float32)
    inv_freq = ROPE_THETA ** (-jnp.arange(ROPE_PAIRS_PER_AXIS, dtype=jnp.float32) / ROPE_PAIRS_PER_AXIS)
    return row[:, None] * inv_freq, col[:, None] * inv_freq


def rope_rotate(x, ang):
    cos = jnp.cos(ang)[:, None, :].astype(x.dtype)
    sin = jnp.sin(ang)[:, None, :].astype(x.dtype)
    x1, x2 = jnp.split(x, 2, axis=-1)
    return jnp.concatenate([x1 * cos - x2 * sin, x1 * sin + x2 * cos], axis=-1)


def axial_rope(x, ang_row, ang_col):
    xr, xc = jnp.split(x, 2, axis=-1)
    return jnp.concatenate([rope_rotate(xr, ang_row), rope_rotate(xc, ang_col)], axis=-1)


def attend(q, k, v):
    s = jnp.einsum("bkgqd,bksd->bkgqs", q, k, preferred_element_type=jnp.float32) * (HEAD_DIM ** -0.5)
    p = jax.nn.softmax(s, axis=-1).astype(v.dtype)
    return jnp.einsum("bkgqs,bksd->bkgqd", p, v)


def attention_mixer(h_ctx, h_lat, ang_row, ang_col, w_q, w_kv, q_gain, k_gain, w_o, with_ctx):
    B, S, D = h_lat.shape

    def q_proj(h):
        q = rmsnorm((h @ w_q).reshape(B, h.shape[1], N_HEADS, HEAD_DIM), q_gain)
        return q

    def kv_proj(h):
        kv = (h @ w_kv).reshape(B, h.shape[1], 2, N_KV_HEADS, HEAD_DIM)
        return rmsnorm(kv[:, :, 0], k_gain), kv[:, :, 1]

    def q_heads(q):
        return q.reshape(B, q.shape[1], N_KV_HEADS, GQA_GROUP, HEAD_DIM).transpose(0, 2, 3, 1, 4)

    def kv_heads(t):
        return t.transpose(0, 2, 1, 3)

    k_c, v_c = kv_proj(h_ctx)
    k_l, v_l = kv_proj(h_lat)
    k_l = axial_rope(k_l, ang_row, ang_col)
    q_l = axial_rope(q_proj(h_lat), ang_row, ang_col)

    k_c, v_c = kv_heads(k_c), kv_heads(v_c)
    k_all = jnp.concatenate([k_c, kv_heads(k_l)], axis=2)
    v_all = jnp.concatenate([v_c, kv_heads(v_l)], axis=2)

    n_blocks = S // Q_BLOCK
    qb = q_heads(q_l).reshape(B, N_KV_HEADS, GQA_GROUP, n_blocks, Q_BLOCK, HEAD_DIM)
    qb = qb.transpose(3, 0, 1, 2, 4, 5)
    ob = lax.map(lambda qblk: attend(qblk, k_all, v_all), qb)
    o_l = ob.transpose(1, 0, 4, 2, 3, 5).reshape(B, S, D) @ w_o

    o_c = None
    if with_ctx:
        oc = attend(q_heads(q_proj(h_ctx)), k_c, v_c)
        o_c = oc.transpose(0, 3, 1, 2, 4).reshape(B, h_ctx.shape[1], D) @ w_o
    return o_c, o_l


def conformer_conv_mixer(h_ctx, h_lat, w_pw1, b_pw1, w_dw, b_dw, ln_g, ln_b, w_pw2, b_pw2, with_ctx):
    def f(h):
        a, g = jnp.split(h @ w_pw1 + b_pw1, 2, axis=-1)
        u = a * jax.nn.sigmoid(g)
        u = depthwise_conv(u, w_dw, b_dw)
        u = jax.nn.silu(layernorm(u, ln_g, ln_b))
        return u @ w_pw2 + b_pw2
    return (f(h_ctx) if with_ctx else None), f(h_lat)


def hyena_filter_spectrum(L, f_w1, f_b1, f_freq1, f_w2, f_b2, f_freq2, f_w3):
    f32 = jnp.float32
    d = f_w3.shape[-1] // (2 * HYENA_ORDER)
    pos = jnp.arange(L, dtype=f32)[:, None]
    t = pos / (L - 1)
    w = 2.0 * math.pi * pos / L
    bands = jnp.linspace(1e-4, FILTER_BANDS - 1, FILTER_BANDS, dtype=f32)
    feats = jnp.concatenate([t, jnp.cos(bands * w), -jnp.sin(bands * w)], axis=-1)
    z = jnp.sin(f_freq1.astype(f32) * (feats @ f_w1.astype(f32) + f_b1.astype(f32)))
    z = jnp.sin(f_freq2.astype(f32) * (z @ f_w2.astype(f32) + f_b2.astype(f32)))
    h = (z @ f_w3.astype(f32)).reshape(L, 2, HYENA_ORDER, d)
    deltas = jnp.linspace(math.log(DECAY_TARGET) / SLOW_DECAY_PCT,
                          math.log(DECAY_TARGET) / FAST_DECAY_PCT, d, dtype=f32)
    decay = jnp.exp(-t * jnp.abs(deltas))
    h = h * decay[:, None, None, :]
    two_sided = jnp.concatenate(
        [h[:, 0], jnp.zeros((1, HYENA_ORDER, d), f32), h[:0:-1, 1]], axis=0)
    two_sided = two_sided / jnp.sum(jnp.abs(two_sided), axis=0, keepdims=True)
    return jnp.fft.rfft(two_sided, axis=0)


def long_conv(z, h_spec, bias):
    L = z.shape[1]
    zf32 = z.astype(jnp.float32)
    zf = jnp.fft.rfft(zf32, n=2 * L, axis=1)
    y = jnp.fft.irfft(zf * h_spec[None], n=2 * L, axis=1)[:, :L]
    return (y + zf32 * bias.astype(jnp.float32)).astype(z.dtype)


def hyena_mixer(h_ctx, h_lat, w_in, b_in, w_short, b_short, f_w1, f_b1, f_freq1, f_w2, f_b2,
                f_freq2, f_w3, long_bias, w_out, b_out, with_ctx):
    def f(h):
        L = h.shape[1]
        u = depthwise_conv(h @ w_in + b_in, w_short, b_short)
        v, x1, x2 = jnp.split(u, 3, axis=-1)
        spec = hyena_filter_spectrum(L, f_w1, f_b1, f_freq1, f_w2, f_b2, f_freq2, f_w3)
        z = v
        for n, gate in enumerate((x1, x2)):
            z = gate * long_conv(z, spec[:, n], long_bias[n])
        return z @ w_out + b_out
    return (f(h_ctx) if with_ctx else None), f(h_lat)


def hierarchical_moe(x, w_group, b_group, w_router, b_router, w_gate, w_up, w_down):
    f32 = jnp.float32
    xf = x.astype(f32)
    g_prob = jax.nn.softmax(xf @ w_group.astype(f32) + b_group.astype(f32), axis=-1)
    g_p, g_idx = lax.top_k(g_prob, 1)
    e_logits = (xf @ w_router.astype(f32) + b_router.astype(f32)).reshape(-1, N_GROUPS, EXPERTS_PER_GROUP)
    e_logits = jnp.einsum("ng,nge->ne", jax.nn.one_hot(g_idx[:, 0], N_GROUPS, dtype=f32), e_logits)
    e_p, e_idx = lax.top_k(jax.nn.softmax(e_logits, axis=-1), TOP_K)
    w_tok = g_p * (e_p / jnp.sum(e_p, axis=-1, keepdims=True))
    flat_idx = g_idx * EXPERTS_PER_GROUP + e_idx
    combine = jnp.einsum("nk,nke->ne", w_tok, jax.nn.one_hot(flat_idx, N_EXPERTS, dtype=f32))
    combine = combine.reshape(-1, N_GROUPS, EXPERTS_PER_GROUP).astype(x.dtype)
    out = jnp.zeros_like(x)
    for g in range(N_GROUPS):
        a = jnp.einsum("nd,edf->nef", x, w_gate[g])
        b = jnp.einsum("nd,edf->nef", x, w_up[g])
        hid = jax.nn.silu(a) * b * combine[:, g, :, None]
        out = out + jnp.einsum("nef,efd->nd", hid, w_down[g])
    return out


def setup_inputs(seed: int = 0) -> dict:
    key = jax.random.key(seed)
    ks = iter(jax.random.split(key, 64))
    D = D_MODEL

    def nrm(shape, scale):
        return scale * jax.random.normal(next(ks), shape, jnp.float32)

    def gain(shape):
        return 1.0 + nrm(shape, 0.05)

    nA, nC, nH = N_ATTN_LAYERS, N_CONV_LAYERS, N_HYENA_LAYERS
    return {
        "x": nrm((BATCH, SEQ, D), 1.0),
        "c": nrm((BATCH, D), 1.0),
        "ctx": nrm((BATCH, CTX_LEN, D), 1.0),
        "c_ctx": nrm((D,), 1.0),
        "w_mod": nrm((DEPTH, D, 6 * D), 0.3 * D ** -0.5),
        "b_mod": nrm((DEPTH, 6 * D), 0.01),
        "norm_mix_g": gain((DEPTH, D)),
        "norm_ffn_g": gain((DEPTH, D)),
        "attn_w_q": nrm((nA, D, N_HEADS * HEAD_DIM), D ** -0.5),
        "attn_w_kv": nrm((nA, D, 2 * N_KV_HEADS * HEAD_DIM), D ** -0.5),
        "attn_q_gain": gain((nA, HEAD_DIM)),
        "attn_k_gain": gain((nA, HEAD_DIM)),
        "attn_w_o": nrm((nA, N_HEADS * HEAD_DIM, D), D ** -0.5),
        "conv_w_pw1": nrm((nC, D, 2 * D), D ** -0.5),
        "conv_b_pw1": nrm((nC, 2 * D), 0.01),
        "conv_w_dw": nrm((nC, CONV_WIDTH, D), CONV_WIDTH ** -0.5),
        "conv_b_dw": nrm((nC, D), 0.01),
        "conv_ln_g": gain((nC, D)),
        "conv_ln_b": nrm((nC, D), 0.01),
        "conv_w_pw2": nrm((nC, D, D), D ** -0.5),
        "conv_b_pw2": nrm((nC, D), 0.01),
        "hy_w_in": nrm((nH, D, 3 * D), D ** -0.5),
        "hy_b_in": nrm((nH, 3 * D), 0.01),
        "hy_w_short": nrm((nH, SHORT_CONV_WIDTH, 3 * D), SHORT_CONV_WIDTH ** -0.5),
        "hy_b_short": nrm((nH, 3 * D), 0.01),
        "hy_f_w1": nrm((nH, FILTER_EMB, FILTER_HIDDEN), FILTER_EMB ** -0.5),
        "hy_f_b1": nrm((nH, FILTER_HIDDEN), 0.01),
        "hy_f_freq1": gain((nH, FILTER_HIDDEN)),
        "hy_f_w2": nrm((nH, FILTER_HIDDEN, FILTER_HIDDEN), FILTER_HIDDEN ** -0.5),
        "hy_f_b2": nrm((nH, FILTER_HIDDEN), 0.01),
        "hy_f_freq2": gain((nH, FILTER_HIDDEN)),
        "hy_f_w3": nrm((nH, FILTER_HIDDEN, 2 * HYENA_ORDER * D), FILTER_HIDDEN ** -0.5),
        "hy_long_bias": nrm((nH, HYENA_ORDER, D), 1.0),
        "hy_w_out": nrm((nH, D, D), D ** -0.5),
        "hy_b_out": nrm((nH, D), 0.01),
        "moe_w_group": nrm((DEPTH, D, N_GROUPS), D ** -0.5),
        "moe_b_group": nrm((DEPTH, N_GROUPS), 0.01),
        "moe_w_router": nrm((DEPTH, D, N_EXPERTS), D ** -0.5),
        "moe_b_router": nrm((DEPTH, N_EXPERTS), 0.01),
        "moe_w_gate": nrm((DEPTH, N_GROUPS, EXPERTS_PER_GROUP, D, D_EXPERT), D ** -0.5),
        "moe_w_up": nrm((DEPTH, N_GROUPS, EXPERTS_PER_GROUP, D, D_EXPERT), D ** -0.5),
        "moe_w_down": nrm((DEPTH, N_GROUPS, EXPERTS_PER_GROUP, D_EXPERT, D), D_EXPERT ** -0.5),
        "final_norm_g": gain((D,)),
    }


def reference(x, c, ctx, c_ctx, w_mod, b_mod, norm_mix_g, norm_ffn_g,
              attn_w_q, attn_w_kv, attn_q_gain, attn_k_gain, attn_w_o,
              conv_w_pw1, conv_b_pw1, conv_w_dw, conv_b_dw, conv_ln_g, conv_ln_b, conv_w_pw2, conv_b_pw2,
              hy_w_in, hy_b_in, hy_w_short, hy_b_short, hy_f_w1, hy_f_b1, hy_f_freq1, hy_f_w2, hy_f_b2,
              hy_f_freq2, hy_f_w3, hy_long_bias, hy_w_out, hy_b_out,
              moe_w_group, moe_b_group, moe_w_router, moe_b_router, moe_w_gate, moe_w_up, moe_w_down,
              final_norm_g):
    B, S, D = x.shape
    n_ctx = ctx.shape[1]
    rows = S // GRID_W
    ang_row, ang_col = axial_angles(rows)
    silu_c = jax.nn.silu(c)
    silu_cc = jax.nn.silu(c_ctx)
    x_lat, x_ctx = x, ctx

    for i in range(DEPTH):
        kind, slot = i % N_MIXERS, i // N_MIXERS
        with_ctx = i < DEPTH - 1
        needs_ctx_input = with_ctx or kind == 0

        sh1, sc1, g1, sh2, sc2, g2 = jnp.split((silu_c @ w_mod[i] + b_mod[i])[:, None, :], 6, axis=-1)
        csh1, csc1, cg1, csh2, csc2, cg2 = jnp.split(silu_cc @ w_mod[i] + b_mod[i], 6, axis=-1)

        h_l = modulated_norm(x_lat, norm_mix_g[i], sh1, sc1)
        h_c = modulated_norm(x_ctx, norm_mix_g[i], csh1, csc1) if needs_ctx_input else None
        if kind == 0:
            y_c, y_l = attention_mixer(h_c, h_l, ang_row, ang_col, attn_w_q[slot], attn_w_kv[slot],
                                       attn_q_gain[slot], attn_k_gain[slot], attn_w_o[slot], with_ctx)
        elif kind == 1:
            y_c, y_l = conformer_conv_mixer(h_c, h_l, conv_w_pw1[slot], conv_b_pw1[slot], conv_w_dw[slot],
                                            conv_b_dw[slot], conv_ln_g[slot], conv_ln_b[slot],
                                            conv_w_pw2[slot], conv_b_pw2[slot], with_ctx)
        else:
            y_c, y_l = hyena_mixer(h_c, h_l, hy_w_in[slot], hy_b_in[slot], hy_w_short[slot], hy_b_short[slot],
                                   hy_f_w1[slot], hy_f_b1[slot], hy_f_freq1[slot], hy_f_w2[slot],
                                   hy_f_b2[slot], hy_f_freq2[slot], hy_f_w3[slot], hy_long_bias[slot],
                                   hy_w_out[slot], hy_b_out[slot], with_ctx)
        x_lat = x_lat + g1 * y_l
        if with_ctx:
            x_ctx = x_ctx + cg1 * y_c

        h_l = modulated_norm(x_lat, norm_ffn_g[i], sh2, sc2)
        moe_args = (moe_w_group[i], moe_b_group[i], moe_w_router[i], moe_b_router[i],
                    moe_w_gate[i], moe_w_up[i], moe_w_down[i])
        if with_ctx:
            h_c = modulated_norm(x_ctx, norm_ffn_g[i], csh2, csc2)
            tokens = jnp.concatenate([h_c, h_l], axis=1)
            y = hierarchical_moe(tokens.reshape(-1, D), *moe_args).reshape(B, n_ctx + S, D)
            x_ctx = x_ctx + cg2 * y[:, :n_ctx]
            x_lat = x_lat + g2 * y[:, n_ctx:]
        else:
            y_l = hierarchical_moe(h_l.reshape(-1, D), *moe_args).reshape(B, S, D)
            x_lat = x_lat + g2 * y_l

    return rmsnorm(x_lat, final_norm_g)
```

```python
import functools
import math
from typing import NamedTuple

import jax
import jax.numpy as jnp
from jax import lax
from jax.experimental import pallas as pl
from jax.experimental.pallas import tpu as pltpu

F32 = jnp.float32
BF16 = jnp.bfloat16
EPS = 1e-6
V7X_VMEM_LIMIT_BYTES = 56 * 1024 * 1024


class Cfg(NamedTuple):
    batch: int
    seq: int
    ctx: int
    d: int
    grid_w: int
    n_heads: int
    n_kv: int
    head_dim: int
    conv_w: int
    n_groups: int
    epg: int
    d_exp: int
    tb: int
    fft_n1: int
    fft_n2: int
    kv_chunk: int

    @property
    def nlat(self):
        return self.batch * self.seq

    @property
    def nt(self):
        return self.batch * (self.seq + self.ctx)

    @property
    def nbl(self):
        return self.nlat // self.tb

    @property
    def nbc(self):
        return self.batch * self.ctx // self.tb

    @property
    def nb(self):
        return self.nbl + self.nbc

    @property
    def spb(self):
        return self.seq // self.tb

    @property
    def cpb(self):
        return self.ctx // self.tb


CFG = Cfg(batch=8, seq=4096, ctx=256, d=1024, grid_w=64, n_heads=8, n_kv=2, head_dim=128, conv_w=31,
          n_groups=4, epg=8, d_exp=256, tb=256, fft_n1=256, fft_n2=32, kv_chunk=512)

MOD_ROWS = 16


def _cparams(sem):
    return pltpu.CompilerParams(dimension_semantics=sem, vmem_limit_bytes=V7X_VMEM_LIMIT_BYTES)


def _split_bf16(a):
    hi = a.astype(BF16)
    lo = (a - hi.astype(F32)).astype(BF16)
    return hi, lo


def _dot(a, b):
    return jnp.dot(a, b, preferred_element_type=F32)


def _dot3(a, b):
    ah, al = _split_bf16(a)
    bh, bl = _split_bf16(b)
    return _dot(ah, bh) + (_dot(ah, bl) + _dot(al, bh))


def _modnorm(x, g, sh, sc):
    ms = jnp.mean(x * x, axis=-1, keepdims=True)
    y = x * lax.rsqrt(ms + EPS) * g
    return y * (1.0 + sc) + sh


def _silu(x):
    return x * jax.nn.sigmoid(x)


def _mod_row(cfg, j):
    return jnp.where(j < cfg.nbl, j // cfg.spb, cfg.batch)


def _mod_spec(cfg, chunk):
    return pl.BlockSpec((1, 1, cfg.d), lambda j: (_mod_row(cfg, j) * 6 + chunk, 0, 0))


def _row_spec(cfg, width):
    return pl.BlockSpec((cfg.tb, width), lambda j: (j, 0))


def _full_spec(shape):
    n = len(shape)
    return pl.BlockSpec(shape, lambda *_: (0,) * n)


def _mod_kernel(c_ref, w_ref, b_ref, o_ref):
    o_ref[0] = _dot3(_silu(c_ref[...]), w_ref[0]) + b_ref[0]


def modulation(c_all, w_mod, b_mod):
    depth, d, n6 = w_mod.shape
    bn = n6 // 4
    return pl.pallas_call(
        _mod_kernel,
        grid=(depth, n6 // bn),
        in_specs=[_full_spec((MOD_ROWS, d)),
                  pl.BlockSpec((1, d, bn), lambda i, n: (i, 0, n)),
                  pl.BlockSpec((1, 1, bn), lambda i, n: (i, 0, n))],
        out_specs=pl.BlockSpec((1, MOD_ROWS, bn), lambda i, n: (i, 0, n)),
        out_shape=jax.ShapeDtypeStruct((depth, MOD_ROWS, n6), F32),
        compiler_params=_cparams(("arbitrary", "arbitrary")),
        name="modulation",
    )(c_all, w_mod, b_mod.reshape(depth, 1, n6))


def _qkv_kernel(x_ref, g_ref, sh_ref, sc_ref, w_ref, cs_ref, sn_ref, qg_ref, kg_ref, o_ref, *, cfg):
    hd = cfg.head_dim
    h = _modnorm(x_ref[...], g_ref[...], sh_ref[0], sc_ref[0]).astype(BF16)
    y = _dot(h, w_ref[...])
    cs = cs_ref[...]
    sn = sn_ref[...]
    lane = lax.broadcasted_iota(jnp.int32, cs.shape, 1)
    first_half = (lane % (hd // 2)) < (hd // 4)

    def norm_rope(v, gain, scale):
        ms = jnp.mean(v * v, axis=-1, keepdims=True)
        v = v * lax.rsqrt(ms + EPS) * gain
        partner = jnp.where(first_half, pltpu.roll(v, hd - hd // 4, 1), pltpu.roll(v, hd // 4, 1))
        return (v * cs + partner * sn) * scale

    nq = cfg.n_heads
    for hh in range(nq):
        sl = slice(hh * hd, (hh + 1) * hd)
        o_ref[:, sl] = norm_rope(y[:, sl], qg_ref[...], hd ** -0.5).astype(o_ref.dtype)
    for hh in range(cfg.n_kv):
        sl = slice((nq + hh) * hd, (nq + hh + 1) * hd)
        o_ref[:, sl] = norm_rope(y[:, sl], kg_ref[...], 1.0).astype(o_ref.dtype)
    v0 = (nq + cfg.n_kv) * hd
    o_ref[:, v0:] = y[:, v0:].astype(o_ref.dtype)


def qkv_project(cfg, x, norm_g, mod, w_qkv, rope_cos, rope_sin, q_gain, k_gain):
    nqkv = w_qkv.shape[1]
    hd = cfg.head_dim

    def rope_idx(j):
        return (jnp.where(j < cfg.nbl, j % cfg.spb, cfg.spb), 0)

    return pl.pallas_call(
        functools.partial(_qkv_kernel, cfg=cfg),
        grid=(cfg.nb,),
        in_specs=[_row_spec(cfg, cfg.d), _full_spec((1, cfg.d)), _mod_spec(cfg, 0), _mod_spec(cfg, 1),
                  _full_spec((cfg.d, nqkv)),
                  pl.BlockSpec((cfg.tb, hd), rope_idx), pl.BlockSpec((cfg.tb, hd), rope_idx),
                  _full_spec((1, hd)), _full_spec((1, hd))],
        out_specs=_row_spec(cfg, nqkv),
        out_shape=jax.ShapeDtypeStruct((cfg.nt, nqkv), BF16),
        compiler_params=_cparams(("arbitrary",)),
        name="qkv_project",
    )(x, norm_g, mod, mod, w_qkv, rope_cos, rope_sin, q_gain, k_gain)


def _attn_kernel(q_ref, kc_ref, vc_ref, kl_ref, vl_ref, o_ref, *, cfg, n_lat_chunks):
    hd = cfg.head_dim
    group = cfg.n_heads // cfg.n_kv
    tq = cfg.tb
    ch = cfg.kv_chunk
    q = jnp.concatenate([q_ref[:, h * hd:(h + 1) * hd] for h in range(group)], axis=0)

    def scores(k):
        return lax.dot_general(q, k, (((1,), (1,)), ((), ())), preferred_element_type=F32)

    s = scores(kc_ref[...])
    m = jnp.max(s, axis=-1, keepdims=True)
    p = jnp.exp(s - m)
    l = jnp.sum(p, axis=-1, keepdims=True)
    acc = _dot(p.astype(BF16), vc_ref[...])

    def body(c, carry):
        m, l, acc = carry
        start = pl.multiple_of(c * ch, ch)
        s = scores(kl_ref[pl.ds(start, ch), :])
        m_new = jnp.maximum(m, jnp.max(s, axis=-1, keepdims=True))
        alpha = jnp.exp(m - m_new)
        p = jnp.exp(s - m_new)
        l = alpha * l + jnp.sum(p, axis=-1, keepdims=True)
        acc = alpha * acc + _dot(p.astype(BF16), vl_ref[pl.ds(start, ch), :])
        return m_new, l, acc

    n = jnp.where(pl.program_id(2) < cfg.spb, n_lat_chunks, 0)
    m, l, acc = lax.fori_loop(0, n, body, (m, l, acc))
    o = acc / l
    for h in range(group):
        o_ref[:, h * hd:(h + 1) * hd] = o[h * tq:(h + 1) * tq].astype(o_ref.dtype)


def attention(cfg, qkv):
    hd = cfg.head_dim
    group = cfg.n_heads // cfg.n_kv
    gw = group * hd
    kcol = cfg.n_heads
    vcol = cfg.n_heads + cfg.n_kv
    assert cfg.cpb == 1

    def qrow(b, qb):
        return jnp.where(qb < cfg.spb, b * cfg.spb + qb, cfg.nbl + b)

    return pl.pallas_call(
        functools.partial(_attn_kernel, cfg=cfg, n_lat_chunks=cfg.seq // cfg.kv_chunk),
        grid=(cfg.batch, cfg.n_kv, cfg.spb + 1),
        in_specs=[pl.BlockSpec((cfg.tb, gw), lambda b, k, qb: (qrow(b, qb), k)),
                  pl.BlockSpec((cfg.ctx, hd), lambda b, k, qb: (cfg.nlat // cfg.ctx + b, kcol + k)),
                  pl.BlockSpec((cfg.ctx, hd), lambda b, k, qb: (cfg.nlat // cfg.ctx + b, vcol + k)),
                  pl.BlockSpec((cfg.seq, hd), lambda b, k, qb: (b, kcol + k)),
                  pl.BlockSpec((cfg.seq, hd), lambda b, k, qb: (b, vcol + k))],
        out_specs=pl.BlockSpec((cfg.tb, gw), lambda b, k, qb: (qrow(b, qb), k)),
        out_shape=jax.ShapeDtypeStruct((cfg.nt, cfg.n_heads * hd), BF16),
        compiler_params=_cparams(("arbitrary", "arbitrary", "arbitrary")),
        name="attention",
    )(qkv, qkv, qkv, qkv, qkv)


def _proj_res_kernel(a_ref, w_ref, b_ref, x_ref, gate_ref, o_ref):
    y = _dot(a_ref[...].astype(BF16), w_ref[...]) + b_ref[...]
    o_ref[...] = x_ref[...] + gate_ref[0] * y


def project_residual(cfg, a, w, bias, x, mod, gate_chunk, n_blocks):
    return pl.pallas_call(
        _proj_res_kernel,
        grid=(n_blocks,),
        in_specs=[_row_spec(cfg, a.shape[1]), _full_spec(w.shape), _full_spec((1, cfg.d)),
                  _row_spec(cfg, cfg.d), _mod_spec(cfg, gate_chunk)],
        out_specs=_row_spec(cfg, cfg.d),
        out_shape=jax.ShapeDtypeStruct((n_blocks * cfg.tb, cfg.d), F32),
        compiler_params=_cparams(("arbitrary",)),
        name="project_residual",
    )(a, w, bias, x, mod)


def _glu_kernel(x_ref, g_ref, sh_ref, sc_ref, w_ref, b_ref, o_ref, *, d):
    h = _modnorm(x_ref[...], g_ref[...], sh_ref[0], sc_ref[0]).astype(BF16)
    y = _dot(h, w_ref[...]) + b_ref[...]
    o_ref[...] = y[:, :d] * jax.nn.sigmoid(y[:, d:])


def glu_project(cfg, x, norm_g, mod, w, bias):
    return pl.pallas_call(
        functools.partial(_glu_kernel, d=cfg.d),
        grid=(cfg.nb,),
        in_specs=[_row_spec(cfg, cfg.d), _full_spec((1, cfg.d)), _mod_spec(cfg, 0), _mod_spec(cfg, 1),
                  _full_spec(w.shape), _full_spec((1, w.shape[1]))],
        out_specs=_row_spec(cfg, cfg.d),
        out_shape=jax.ShapeDtypeStruct((cfg.nt, cfg.d), F32),
        compiler_params=_cparams(("arbitrary",)),
        name="glu_project",
    )(x, norm_g, mod, mod, w, bias)


def _seq_edges(cfg, j):
    lat = j < cfg.nbl
    first = jnp.where(lat, j % cfg.spb == 0, (j - cfg.nbl) % cfg.cpb == 0)
    last = jnp.where(lat, j % cfg.spb == cfg.spb - 1, (j - cfg.nbl) % cfg.cpb == cfg.cpb - 1)
    return first, last


def _halo_specs(cfg, width, halo):
    per = cfg.tb // halo
    last_blk = cfg.nt // halo - 1
    prev = pl.BlockSpec((halo, width), lambda j: (jnp.maximum(j * per - 1, 0), 0))
    nxt = pl.BlockSpec((halo, width), lambda j: (jnp.minimum((j + 1) * per, last_blk), 0))
    return prev, nxt


def _fill_ext(cfg, ext_ref, prev_ref, cur_ref, next_ref, halo):
    first, last = _seq_edges(cfg, pl.program_id(0))
    tb = cfg.tb
    ext_ref[0:halo, :] = jnp.where(first, 0.0, prev_ref[...])
    ext_ref[halo:halo + tb, :] = cur_ref[...]
    ext_ref[halo + tb:2 * halo + tb, :] = jnp.where(last, 0.0, next_ref[...])


def _conv_kernel(up_ref, uc_ref, un_ref, wdw_ref, bdw_ref, lng_ref, lnb_ref, w2_ref, b2_ref, x_ref, gate_ref,
                 o_ref, ext_ref, *, cfg, halo):
    tb = cfg.tb
    _fill_ext(cfg, ext_ref, up_ref, uc_ref, un_ref, halo)
    half = (cfg.conv_w - 1) // 2
    acc = jnp.zeros((tb, cfg.d), F32)
    for k in range(cfg.conv_w):
        off = halo - half + k
        acc = acc + wdw_ref[k:k + 1, :] * ext_ref[off:off + tb, :]
    u = acc + bdw_ref[...]
    mu = jnp.mean(u, axis=-1, keepdims=True)
    uc = u - mu
    u = uc * lax.rsqrt(jnp.mean(uc * uc, axis=-1, keepdims=True) + EPS) * lng_ref[...] + lnb_ref[...]
    u = _silu(u).astype(BF16)
    y = _dot(u, w2_ref[...]) + b2_ref[...]
    o_ref[...] = x_ref[...] + gate_ref[0] * y


def conv_module(cfg, u, w_dw, b_dw, ln_g, ln_b, w2, b2, x, mod):
    halo = 16
    prev, nxt = _halo_specs(cfg, cfg.d, halo)
    return pl.pallas_call(
        functools.partial(_conv_kernel, cfg=cfg, halo=halo),
        grid=(cfg.nb,),
        in_specs=[prev, _row_spec(cfg, cfg.d), nxt, _full_spec(w_dw.shape), _full_spec((1, cfg.d)),
                  _full_spec((1, cfg.d)), _full_spec((1, cfg.d)), _full_spec(w2.shape), _full_spec((1, cfg.d)),
                  _row_spec(cfg, cfg.d), _mod_spec(cfg, 2)],
        out_specs=_row_spec(cfg, cfg.d),
        out_shape=jax.ShapeDtypeStruct((cfg.nt, cfg.d), F32),
        scratch_shapes=[pltpu.VMEM((cfg.tb + 2 * halo, cfg.d), F32)],
        compiler_params=_cparams(("arbitrary",)),
        name="conv_module",
    )(u, u, u, w_dw, b_dw, ln_g, ln_b, w2, b2, x, mod)


def _inproj_kernel(x_ref, g_ref, sh_ref, sc_ref, w_ref, b_ref, o_ref):
    h = _modnorm(x_ref[...], g_ref[...], sh_ref[0], sc_ref[0]).astype(BF16)
    o_ref[...] = _dot(h, w_ref[...]) + b_ref[...]


def in_project(cfg, x, norm_g, mod, w, bias):
    n = w.shape[1]
    return pl.pallas_call(
        _inproj_kernel,
        grid=(cfg.nb,),
        in_specs=[_row_spec(cfg, cfg.d), _full_spec((1, cfg.d)), _mod_spec(cfg, 0), _mod_spec(cfg, 1),
                  _full_spec(w.shape), _full_spec((1, n))],
        out_specs=_row_spec(cfg, n),
        out_shape=jax.ShapeDtypeStruct((cfg.nt, n), F32),
        compiler_params=_cparams(("arbitrary",)),
        name="in_project",
    )(x, norm_g, mod, mod, w, bias)


def _short_conv_kernel(pp_ref, pc_ref, pn_ref, w_ref, b_ref, o_ref, ext_ref, *, cfg, halo, width):
    tb = cfg.tb
    _fill_ext(cfg, ext_ref, pp_ref, pc_ref, pn_ref, halo)
    half = (width - 1) // 2
    acc = b_ref[...]
    for k in range(width):
        off = halo - half + k
        acc = acc + w_ref[k:k + 1, :] * ext_ref[off:off + tb, :]
    o_ref[...] = acc


def short_conv(cfg, p, w, bias):
    halo = 8
    n = p.shape[1]
    bn = cfg.d
    per = cfg.tb // halo
    last_blk = cfg.nt // halo - 1
    return pl.pallas_call(
        functools.partial(_short_conv_kernel, cfg=cfg, halo=halo, width=w.shape[0]),
        grid=(cfg.nb, n // bn),
        in_specs=[pl.BlockSpec((halo, bn), lambda j, c: (jnp.maximum(j * per - 1, 0), c)),
                  pl.BlockSpec((cfg.tb, bn), lambda j, c: (j, c)),
                  pl.BlockSpec((halo, bn), lambda j, c: (jnp.minimum((j + 1) * per, last_blk), c)),
                  pl.BlockSpec((w.shape[0], bn), lambda j, c: (0, c)),
                  pl.BlockSpec((1, bn), lambda j, c: (0, c))],
        out_specs=pl.BlockSpec((cfg.tb, bn), lambda j, c: (j, c)),
        out_shape=jax.ShapeDtypeStruct((cfg.nt, n), F32),
        scratch_shapes=[pltpu.VMEM((cfg.tb + 2 * halo, bn), F32)],
        compiler_params=_cparams(("arbitrary", "arbitrary")),
        name="short_conv",
    )(p, p, p, w, bias)


def _filter_mlp_kernel(feat_ref, w1_ref, b1_ref, f1_ref, w2_ref, b2_ref, f2_ref, w3_ref, delta_ref,
                       h_ref, s_ref, *, length, rows):
    i = pl.program_id(0)
    z = jnp.sin(f1_ref[...] * (_dot3(feat_ref[...], w1_ref[...]) + b1_ref[...]))
    z = jnp.sin(f2_ref[...] * (_dot3(z, w2_ref[...]) + b2_ref[...]))
    h = _dot3(z, w3_ref[...])
    pos = (i * rows + lax.broadcasted_iota(jnp.int32, h.shape, 0)).astype(F32)
    t = pos / (length - 1)
    h = h * jnp.exp(-t * jnp.abs(delta_ref[...]))
    col = lax.broadcasted_iota(jnp.int32, h.shape, 1)
    h = jnp.where((pos == 0.0) & (col >= h.shape[1] // 2), 0.0, h)
    h_ref[...] = h
    part = jnp.sum(jnp.abs(h), axis=0, keepdims=True)

    @pl.when(i == 0)
    def _():
        s_ref[...] = part

    @pl.when(i > 0)
    def _():
        s_ref[...] = s_ref[...] + part


def filter_taps(length, feats, w1, b1, f1, w2, b2, f2, w3, deltas4):
    rows = min(length, 256)
    nh = w1.shape[1]
    n = w3.shape[1]
    return pl.pallas_call(
        functools.partial(_filter_mlp_kernel, length=length, rows=rows),
        grid=(length // rows,),
        in_specs=[pl.BlockSpec((rows, feats.shape[1]), lambda i: (i, 0)), _full_spec(w1.shape),
                  _full_spec((1, nh)), _full_spec((1, nh)), _full_spec(w2.shape), _full_spec((1, nh)),
                  _full_spec((1, nh)), _full_spec(w3.shape), _full_spec((1, n))],
        out_specs=[pl.BlockSpec((rows, n), lambda i: (i, 0)), _full_spec((1, n))],
        out_shape=[jax.ShapeDtypeStruct((length, n), F32), jax.ShapeDtypeStruct((1, n), F32)],
        compiler_params=_cparams(("arbitrary",)),
        name="filter_taps",
    )(feats, w1, b1, f1, w2, b2, f2, w3, deltas4)


def _fft_stage1_kernel(z_ref, m_ref, a_ref):
    a_ref[...] = _dot(m_ref[0], z_ref[...].astype(BF16)).astype(a_ref.dtype)


def fft_stage1(z_view, mats, n_pairs, n2, lane_stride, lane_off, width, out_dtype=BF16):
    rows_out, rows_in = mats.shape[1:]
    return pl.pallas_call(
        _fft_stage1_kernel,
        grid=(n_pairs, n2),
        in_specs=[pl.BlockSpec((rows_in, width), lambda p, s: (p, s * lane_stride + lane_off)),
                  pl.BlockSpec((1, rows_out, rows_in), lambda p, s: (s, 0, 0))],
        out_specs=pl.BlockSpec((rows_out, width), lambda p, s: (p, s)),
        out_shape=jax.ShapeDtypeStruct((n_pairs * rows_out, n2 * width), out_dtype),
        compiler_params=_cparams(("arbitrary", "arbitrary")),
        name="fft_stage1",
    )(z_view, mats)


def _fft_mid_kernel(a_ref, h_ref, bf_ref, bi_ref, o_ref):
    half = a_ref.shape[0] // 2
    x = _dot(bf_ref[...], a_ref[...])
    xr, xi = x[:half], x[half:]
    hr, hi = h_ref[:half, :], h_ref[half:, :]
    y = jnp.concatenate([xr * hr - xi * hi, xr * hi + xi * hr], axis=0).astype(BF16)
    o_ref[...] = _dot(bi_ref[...], y).astype(o_ref.dtype)


def fft_mid(a, h, h_col, bd_fwd, bd_inv, n_pairs, n_groups, width):
    rows = bd_fwd.shape[0]
    a2 = a.reshape(n_pairs * n_groups * rows, width)
    out = pl.pallas_call(
        _fft_mid_kernel,
        grid=(n_groups, n_pairs),
        in_specs=[pl.BlockSpec((rows, width), lambda q, p: (p * n_groups + q, 0)),
                  pl.BlockSpec((rows, width), lambda q, p: (q, h_col)),
                  _full_spec(bd_fwd.shape), _full_spec(bd_inv.shape)],
        out_specs=pl.BlockSpec((rows, width), lambda q, p: (p * n_groups + q, 0)),
        out_shape=jax.ShapeDtypeStruct(a2.shape, BF16),
        compiler_params=_cparams(("arbitrary", "arbitrary")),
        name="fft_mid",
    )(a2, h, bd_fwd, bd_inv)
    return out.reshape(a.shape)


def _fft_spec_kernel(af_ref, ab_ref, bf_ref, nrm_ref, h_ref):
    half = af_ref.shape[0] // 2
    xf = _dot(bf_ref[...], af_ref[...])
    xb = _dot(bf_ref[...], ab_ref[...])
    inv = 1.0 / nrm_ref[...]
    h_ref[:half, :] = (xf[:half] + xb[:half]) * inv
    h_ref[half:, :] = (xf[half:] - xb[half:]) * inv


def fft_filter_spectrum(a, bd_fwd, norm, n_groups, width, n_orders):
    rows = bd_fwd.shape[0]
    a2 = a.reshape(n_groups * rows, 2 * n_orders * width)
    return pl.pallas_call(
        _fft_spec_kernel,
        grid=(n_groups, n_orders),
        in_specs=[pl.BlockSpec((rows, width), lambda q, o: (q, o)),
                  pl.BlockSpec((rows, width), lambda q, o: (q, n_orders + o)),
                  _full_spec(bd_fwd.shape),
                  pl.BlockSpec((1, width), lambda q, o: (0, o))],
        out_specs=pl.BlockSpec((rows, width), lambda q, o: (q, o)),
        out_shape=jax.ShapeDtypeStruct((n_groups * rows, n_orders * width), F32),
        compiler_params=_cparams(("arbitrary", "arbitrary")),
        name="fft_filter_spectrum",
    )(a2, a2, bd_fwd, norm)


def _fft_inv_kernel(b_ref, m_ref, z_ref, gate_ref, bias_ref, o_ref):
    y = _dot(m_ref[0], b_ref[...])
    o_ref[...] = gate_ref[...] * (y + z_ref[...] * bias_ref[...])


def fft_inverse_gate(b, mats, z_view, z_stride, z_off, gate_view, g_stride, g_off, bias, n_pairs, n2, width,
                     out_rows):
    rows_out, rows_in = mats.shape[1:]
    return pl.pallas_call(
        _fft_inv_kernel,
        grid=(n_pairs, n2),
        in_specs=[pl.BlockSpec((rows_in, width), lambda p, s: (p, s)),
                  pl.BlockSpec((1, rows_out, rows_in), lambda p, s: (s, 0, 0)),
                  pl.BlockSpec((rows_out, width), lambda p, s: (p, s * z_stride + z_off)),
                  pl.BlockSpec((rows_out, width), lambda p, s: (p, s * g_stride + g_off)),
                  _full_spec((1, width))],
        out_specs=pl.BlockSpec((rows_out, width), lambda p, s: (p, s)),
        out_shape=jax.ShapeDtypeStruct((out_rows, n2 * width), F32),
        compiler_params=_cparams(("arbitrary", "arbitrary")),
        name="fft_inverse_gate",
    )(b, mats, z_view, gate_view, bias)


def _ctx_spec_kernel(hf_ref, hb_ref, f_ref, nrm_ref, h_ref):
    half = f_ref.shape[0] // 2
    xf = _dot3(f_ref[...], hf_ref[...])
    xb = _dot3(f_ref[...], hb_ref[...])
    inv = 1.0 / nrm_ref[...]
    h_ref[:half, :] = (xf[:half] + xb[:half]) * inv
    h_ref[half:, :] = (xf[half:] - xb[half:]) * inv


def ctx_filter_spectrum(taps, f_real, norm, width, n_orders):
    length = taps.shape[0]
    rows = f_real.shape[0]
    return pl.pallas_call(
        _ctx_spec_kernel,
        grid=(n_orders,),
        in_specs=[pl.BlockSpec((length, width), lambda o: (0, o)),
                  pl.BlockSpec((length, width), lambda o: (0, n_orders + o)),
                  _full_spec(f_real.shape),
                  pl.BlockSpec((1, width), lambda o: (0, o))],
        out_specs=pl.BlockSpec((rows, width), lambda o: (0, o)),
        out_shape=jax.ShapeDtypeStruct((rows, n_orders * width), F32),
        compiler_params=_cparams(("arbitrary",)),
        name="ctx_filter_spectrum",
    )(taps, taps, f_real, norm)


def _ctx_conv_kernel(v_ref, g1_ref, g2_ref, h_ref, ff_ref, fi_ref, bias_ref, zin_ref, o_ref, *, width):
    del zin_ref
    half = ff_ref.shape[0] // 2
    z = v_ref[...]
    for n, gate_ref in enumerate((g1_ref, g2_ref)):
        x = _dot(ff_ref[...], z.astype(BF16))
        xr, xi = x[:half], x[half:]
        hr = h_ref[:half, n * width:(n + 1) * width]
        hi = h_ref[half:, n * width:(n + 1) * width]
        y = jnp.concatenate([xr * hr - xi * hi, xr * hi + xi * hr], axis=0).astype(BF16)
        y = _dot(fi_ref[...], y)
        z = gate_ref[...] * (y + z * bias_ref[n:n + 1, :])
    o_ref[...] = z


def ctx_long_conv(cfg, u, h_ctx, f_fwd, f_inv, bias, z_out):
    d = cfg.d
    rows = 2 * cfg.ctx
    base = cfg.nlat // rows
    return pl.pallas_call(
        functools.partial(_ctx_conv_kernel, width=d),
        grid=(cfg.batch // 2,),
        in_specs=[pl.BlockSpec((rows, d), lambda p: (base + p, 0)),
                  pl.BlockSpec((rows, d), lambda p: (base + p, 1)),
                  pl.BlockSpec((rows, d), lambda p: (base + p, 2)),
                  _full_spec(h_ctx.shape), _full_spec(f_fwd.shape), _full_spec(f_inv.shape),
                  _full_spec(bias.shape), pl.BlockSpec(memory_space=pl.ANY)],
        out_specs=pl.BlockSpec((rows, d), lambda p: (base + p, 0)),
        out_shape=jax.ShapeDtypeStruct(z_out.shape, F32),
        input_output_aliases={7: 0},
        compiler_params=_cparams(("arbitrary",)),
        name="ctx_long_conv",
    )(u, u, u, h_ctx, f_fwd, f_inv, bias, z_out)


def _dft_tables(n1, n2, group):
    n = n1 * n2
    h1 = n1 // 2
    f1 = jnp.arange(n1, dtype=jnp.int32)[None, :, None]
    s1 = jnp.arange(h1, dtype=jnp.int32)[None, None, :]
    s2 = jnp.arange(n2, dtype=jnp.int32)[:, None, None]
    k = (f1 * (n2 * s1 + s2)) % n
    ang = (2.0 * math.pi / n) * k.astype(F32)
    er, ei = jnp.cos(ang), -jnp.sin(ang)
    fwd = jnp.concatenate([jnp.concatenate([er, -ei], axis=2),
                           jnp.concatenate([ei, er], axis=2)], axis=1)
    cr, ci = jnp.swapaxes(er, 1, 2) / n, -jnp.swapaxes(ei, 1, 2) / n
    inv = jnp.concatenate([jnp.concatenate([cr, -ci], axis=2),
                           jnp.concatenate([ci, cr], axis=2)], axis=1)
    q = jnp.arange(n1 // group)[:, None, None]
    ri = jnp.arange(2)[None, :, None]
    j = jnp.arange(group)[None, None, :]
    perm = (ri * n1 + q * group + j).reshape(-1)
    fwd = fwd[:, perm, :]
    inv = inv[:, :, perm]
    real_only = fwd[:, :, :h1]
    a = jnp.arange(n2, dtype=jnp.int32)
    ang2 = (2.0 * math.pi / n2) * ((a[:, None] * a[None, :]) % n2).astype(F32)
    f2r, f2i = jnp.cos(ang2), -jnp.sin(ang2)
    eye = jnp.eye(group, dtype=F32)
    kr, ki = jnp.kron(eye, f2r), jnp.kron(eye, f2i)
    bd_fwd = jnp.concatenate([jnp.concatenate([kr, -ki], axis=1), jnp.concatenate([ki, kr], axis=1)], axis=0)
    bd_inv = jnp.concatenate([jnp.concatenate([kr, ki], axis=1), jnp.concatenate([-ki, kr], axis=1)], axis=0)
    return (fwd.astype(BF16), inv.astype(BF16), real_only.astype(BF16), bd_fwd.astype(BF16),
            bd_inv.astype(BF16))


def _ctx_dft_tables(length):
    n = 2 * length
    f = jnp.arange(n, dtype=jnp.int32)[:, None]
    s = jnp.arange(length, dtype=jnp.int32)[None, :]
    ang = (2.0 * math.pi / n) * ((f * s) % n).astype(F32)
    fr, fi = jnp.cos(ang), -jnp.sin(ang)
    fwd = jnp.concatenate([jnp.concatenate([fr, -fi], axis=1), jnp.concatenate([fi, fr], axis=1)], axis=0)
    frt, fit = fr.T / n, fi.T / n
    inv = jnp.concatenate([jnp.concatenate([frt, fit], axis=1), jnp.concatenate([-fit, frt], axis=1)], axis=0)
    real_only = jnp.concatenate([fr, fi], axis=0)
    return fwd.astype(BF16), inv.astype(BF16), real_only


def _filter_features(length, bands):
    pos = jnp.arange(length, dtype=F32)[:, None]
    t = pos / (length - 1)
    w = 2.0 * math.pi * pos / length
    bnd = jnp.linspace(1e-4, bands - 1, bands, dtype=F32)
    feats = jnp.concatenate([t, jnp.cos(bnd * w), -jnp.sin(bnd * w)], axis=-1)
    return jnp.pad(feats, ((0, 0), (0, 128 - feats.shape[1])))


def hyena_long_convs(cfg, u, fp, long_bias):
    d = cfg.d
    n1, n2 = cfg.fft_n1, cfg.fft_n2
    group = 4
    n_groups = n1 // group
    n_orders = 2
    n_pairs = cfg.batch // 2
    nrow = cfg.nt // n2
    (w1, b1, fq1, w2, b2, fq2, w3) = fp
    nh = w1.shape[1]
    w1p = jnp.pad(w1, ((0, 128 - w1.shape[0]), (0, 0)))
    bands = (w1.shape[0] - 1) // 2
    deltas = jnp.linspace(math.log(1e-2) / 1.5, math.log(1e-2) / 0.3, d, dtype=F32)
    deltas4 = jnp.tile(deltas, 2 * n_orders)[None, :]
    mlp = (w1p, b1.reshape(1, nh), fq1.reshape(1, nh), w2, b2.reshape(1, nh), fq2.reshape(1, nh), w3, deltas4)

    fwd, inv, real_only, bd_fwd, bd_inv = _dft_tables(n1, n2, group)

    taps, sums = filter_taps(cfg.seq, _filter_features(cfg.seq, bands), *mlp)
    norm = sums[:, :n_orders * d] + sums[:, n_orders * d:]
    a_f = fft_stage1(taps.reshape(n1 // 2, n2 * 2 * n_orders * d), real_only, 1, n2, 1, 0, 2 * n_orders * d)
    h_lat = fft_filter_spectrum(a_f, bd_fwd, norm, n_groups, d, n_orders)

    cf_fwd, cf_inv, cf_real = _ctx_dft_tables(cfg.ctx)
    taps_c, sums_c = filter_taps(cfg.ctx, _filter_features(cfg.ctx, bands), *mlp)
    norm_c = sums_c[:, :n_orders * d] + sums_c[:, n_orders * d:]
    h_ctx = ctx_filter_spectrum(taps_c, cf_real, norm_c, d, n_orders)

    u_view = u.reshape(nrow, n2 * 3 * d)
    z_view, z_stride, z_off = u_view, 3, 0
    z = None
    for order in range(n_orders):
        a = fft_stage1(z_view, fwd, n_pairs, n2, z_stride, z_off, d)
        bmid = fft_mid(a, h_lat, order, bd_fwd, bd_inv, n_pairs, n_groups, d)
        z = fft_inverse_gate(bmid, inv, z_view, z_stride, z_off, u_view, 3, 1 + order,
                             long_bias[order:order + 1], n_pairs, n2, d, nrow)
        z_view, z_stride, z_off = z, 1, 0
    z2 = z.reshape(cfg.nt, d)
    return ctx_long_conv(cfg, u, h_ctx, cf_fwd, cf_inv, long_bias, z2)


def _route_kernel(x_ref, g_ref, sh_ref, sc_ref, wr_ref, br_ref, h_ref, comb_ref, *, cfg):
    ne = cfg.n_groups * cfg.epg
    h = _modnorm(x_ref[...], g_ref[...], sh_ref[0], sc_ref[0])
    h_ref[...] = h.astype(h_ref.dtype)
    logits = _dot3(h, wr_ref[...]) + br_ref[...]
    lane = lax.broadcasted_iota(jnp.int32, logits.shape, 1).astype(F32)
    neg = -jnp.inf
    big = 1e9

    def first_argmax(mask):
        v = jnp.where(mask, logits, neg)
        mx = jnp.max(v, axis=-1, keepdims=True)
        idx = jnp.min(jnp.where(mask & (logits == mx), lane, big), axis=-1, keepdims=True)
        return mx, idx

    gmask = (lane >= ne) & (lane < ne + cfg.n_groups)
    gmax, gidx = first_argmax(gmask)
    g_p = 1.0 / jnp.sum(jnp.where(gmask, jnp.exp(logits - gmax), 0.0), axis=-1, keepdims=True)
    e0 = (gidx - ne) * cfg.epg
    emask = (lane >= e0) & (lane < e0 + cfg.epg)
    m1, i1 = first_argmax(emask)
    m2, i2 = first_argmax(emask & (lane != i1))
    r = jnp.exp(m2 - m1)
    w1 = g_p / (1.0 + r)
    w2 = g_p * r / (1.0 + r)
    comb_ref[...] = jnp.where(lane == i1, w1, 0.0) + jnp.where(lane == i2, w2, 0.0)


def route(cfg, x, norm_g, mod, w_route, b_route, n_blocks):
    return pl.pallas_call(
        functools.partial(_route_kernel, cfg=cfg),
        grid=(n_blocks,),
        in_specs=[_row_spec(cfg, cfg.d), _full_spec((1, cfg.d)), _mod_spec(cfg, 3), _mod_spec(cfg, 4),
                  _full_spec(w_route.shape), _full_spec((1, 128))],
        out_specs=[_row_spec(cfg, cfg.d), _row_spec(cfg, 128)],
        out_shape=[jax.ShapeDtypeStruct((n_blocks * cfg.tb, cfg.d), BF16),
                   jax.ShapeDtypeStruct((n_blocks * cfg.tb, 128), F32)],
        compiler_params=_cparams(("arbitrary",)),
        name="route",
    )(x, norm_g, mod, mod, w_route, b_route)


def _moe_dense_kernel(h_ref, comb_ref, wg_ref, wu_ref, wd_ref, yin_ref, y_ref, *, cfg):
    g = pl.program_id(0)
    h = h_ref[...]
    a = _dot(h, wg_ref[0])
    b = _dot(h, wu_ref[0])
    comb = comb_ref[...]
    lane = lax.broadcasted_iota(jnp.int32, comb.shape, 1)
    de = cfg.d_exp
    parts = []
    for e in range(cfg.epg):
        c = jnp.sum(jnp.where(lane == g * cfg.epg + e, comb, 0.0), axis=-1, keepdims=True)
        sl = slice(e * de, (e + 1) * de)
        parts.append((_silu(a[:, sl]) * b[:, sl] * c).astype(BF16))
    y = _dot(jnp.concatenate(parts, axis=-1), wd_ref[0])

    @pl.when(g == 0)
    def _():
        y_ref[...] = y

    @pl.when(g > 0)
    def _():
        y_ref[...] = yin_ref[...] + y


def moe_dense(cfg, h, comb, wg, wu, wd, n_blocks):
    ge = cfg.epg * cfg.d_exp
    n = n_blocks * cfg.tb
    y0 = jnp.zeros((n, cfg.d), F32)
    return pl.pallas_call(
        functools.partial(_moe_dense_kernel, cfg=cfg),
        grid=(cfg.n_groups, n_blocks),
        in_specs=[pl.BlockSpec((cfg.tb, cfg.d), lambda g, j: (j, 0)),
                  pl.BlockSpec((cfg.tb, 128), lambda g, j: (j, 0)),
                  pl.BlockSpec((1, cfg.d, ge), lambda g, j: (g, 0, 0)),
                  pl.BlockSpec((1, cfg.d, ge), lambda g, j: (g, 0, 0)),
                  pl.BlockSpec((1, ge, cfg.d), lambda g, j: (g, 0, 0)),
                  pl.BlockSpec((cfg.tb, cfg.d), lambda g, j: (j, 0))],
        out_specs=pl.BlockSpec((cfg.tb, cfg.d), lambda g, j: (j, 0)),
        out_shape=jax.ShapeDtypeStruct((n, cfg.d), F32),
        input_output_aliases={5: 0},
        compiler_params=_cparams(("arbitrary", "arbitrary")),
        name="moe_dense",
    )(h, comb, wg, wu, wd, y0)


def _residual_kernel(x_ref, y_ref, gate_ref, o_ref):
    o_ref[...] = x_ref[...] + gate_ref[0] * y_ref[...]


def gated_residual(cfg, x, y, mod, gate_chunk, n_blocks):
    return pl.pallas_call(
        _residual_kernel,
        grid=(n_blocks,),
        in_specs=[_row_spec(cfg, cfg.d), _row_spec(cfg, cfg.d), _mod_spec(cfg, gate_chunk)],
        out_specs=_row_spec(cfg, cfg.d),
        out_shape=jax.ShapeDtypeStruct((n_blocks * cfg.tb, cfg.d), F32),
        compiler_params=_cparams(("arbitrary",)),
        name="gated_residual",
    )(x, y, mod)


def _final_norm_kernel(x_ref, g_ref, o_ref):
    x = x_ref[...]
    o_ref[...] = x * lax.rsqrt(jnp.mean(x * x, axis=-1, keepdims=True) + EPS) * g_ref[...]


def final_norm(cfg, x, g):
    return pl.pallas_call(
        _final_norm_kernel,
        grid=(cfg.nbl,),
        in_specs=[_row_spec(cfg, cfg.d), _full_spec((1, cfg.d))],
        out_specs=_row_spec(cfg, cfg.d),
        out_shape=jax.ShapeDtypeStruct((cfg.nlat, cfg.d), F32),
        compiler_params=_cparams(("arbitrary",)),
        name="final_norm",
    )(x, g)


def _rope_tables(cfg):
    hd = cfg.head_dim
    pairs = hd // 4
    rows = cfg.seq // cfg.grid_w
    row = jnp.repeat(jnp.arange(rows), cfg.grid_w).astype(F32)
    col = jnp.tile(jnp.arange(cfg.grid_w), rows).astype(F32)
    inv_freq = 10000.0 ** (-jnp.arange(pairs, dtype=F32) / pairs)
    ar, ac = row[:, None] * inv_freq, col[:, None] * inv_freq
    cos = jnp.concatenate([jnp.cos(ar), jnp.cos(ar), jnp.cos(ac), jnp.cos(ac)], axis=-1)
    sin = jnp.concatenate([-jnp.sin(ar), jnp.sin(ar), -jnp.sin(ac), jnp.sin(ac)], axis=-1)
    cos = jnp.concatenate([cos, jnp.ones((cfg.tb, hd), F32)], axis=0)
    sin = jnp.concatenate([sin, jnp.zeros((cfg.tb, hd), F32)], axis=0)
    return cos, sin


def _moe_layer(cfg, x, norm_g, mod, w_group, b_group, w_router, b_router, w_gate, w_up, w_down, n_blocks):
    ne = cfg.n_groups * cfg.epg
    w_route = jnp.pad(jnp.concatenate([w_router, w_group], axis=1), ((0, 0), (0, 128 - ne - cfg.n_groups)))
    b_route = jnp.pad(jnp.concatenate([b_router, b_group]), (0, 128 - ne - cfg.n_groups)).reshape(1, 128)
    h, comb = route(cfg, x, norm_g, mod, w_route, b_route, n_blocks)
    ge = cfg.epg * cfg.d_exp
    wg = jnp.swapaxes(w_gate, 1, 2).reshape(cfg.n_groups, cfg.d, ge).astype(BF16)
    wu = jnp.swapaxes(w_up, 1, 2).reshape(cfg.n_groups, cfg.d, ge).astype(BF16)
    wd = w_down.reshape(cfg.n_groups, ge, cfg.d).astype(BF16)
    y = moe_dense(cfg, h, comb, wg, wu, wd, n_blocks)
    return gated_residual(cfg, x, y, mod, 5, n_blocks)


def _forward(cfg, x, c, ctx, c_ctx, w_mod, b_mod, norm_mix_g, norm_ffn_g,
             attn_w_q, attn_w_kv, attn_q_gain, attn_k_gain, attn_w_o,
             conv_w_pw1, conv_b_pw1, conv_w_dw, conv_b_dw, conv_ln_g, conv_ln_b, conv_w_pw2, conv_b_pw2,
             hy_w_in, hy_b_in, hy_w_short, hy_b_short, hy_f_w1, hy_f_b1, hy_f_freq1, hy_f_w2, hy_f_b2,
             hy_f_freq2, hy_f_w3, hy_long_bias, hy_w_out, hy_b_out,
             moe_w_group, moe_b_group, moe_w_router, moe_b_router, moe_w_gate, moe_w_up, moe_w_down,
             final_norm_g):
    d = cfg.d
    depth = w_mod.shape[0]
    xs = jnp.concatenate([x.reshape(-1, d), ctx.reshape(-1, d)], axis=0)
    c_all = jnp.concatenate([c, c_ctx[None, :], jnp.zeros((MOD_ROWS - cfg.batch - 1, d), F32)], axis=0)
    mods = modulation(c_all, w_mod, b_mod)
    rope_cos, rope_sin = _rope_tables(cfg)
    zero_bias = jnp.zeros((1, d), F32)

    for i in range(depth):
        kind, slot = i % 3, i // 3
        last = i == depth - 1
        n_blocks = cfg.nbl if last else cfg.nb
        mod = mods[i].reshape(MOD_ROWS * 6, 1, d)
        g_mix = norm_mix_g[i].reshape(1, d)
        if kind == 0:
            w_qkv = jnp.concatenate([attn_w_q[slot], attn_w_kv[slot]], axis=1).astype(BF16)
            qkv = qkv_project(cfg, xs, g_mix, mod, w_qkv, rope_cos, rope_sin,
                              attn_q_gain[slot].reshape(1, -1), attn_k_gain[slot].reshape(1, -1))
            o = attention(cfg, qkv)
            xs = project_residual(cfg, o, attn_w_o[slot].astype(BF16), zero_bias, xs, mod, 2, n_blocks)
        elif kind == 1:
            u = glu_project(cfg, xs, g_mix, mod, conv_w_pw1[slot].astype(BF16), conv_b_pw1[slot].reshape(1, -1))
            xs = conv_module(cfg, u, conv_w_dw[slot], conv_b_dw[slot].reshape(1, d), conv_ln_g[slot].reshape(1, d),
                             conv_ln_b[slot].reshape(1, d), conv_w_pw2[slot].astype(BF16),
                             conv_b_pw2[slot].reshape(1, d), xs, mod)
        else:
            p = in_project(cfg, xs, g_mix, mod, hy_w_in[slot].astype(BF16), hy_b_in[slot].reshape(1, -1))
            u = short_conv(cfg, p, hy_w_short[slot], hy_b_short[slot].reshape(1, -1))
            fp = (hy_f_w1[slot], hy_f_b1[slot], hy_f_freq1[slot], hy_f_w2[slot], hy_f_b2[slot],
                  hy_f_freq2[slot], hy_f_w3[slot])
            z2 = hyena_long_convs(cfg, u, fp, hy_long_bias[slot])
            xs = project_residual(cfg, z2, hy_w_out[slot].astype(BF16), hy_b_out[slot].reshape(1, d), xs, mod, 2,
                                  n_blocks)
        xs = _moe_layer(cfg, xs, norm_ffn_g[i].reshape(1, d), mod, moe_w_group[i], moe_b_group[i],
                        moe_w_router[i], moe_b_router[i], moe_w_gate[i], moe_w_up[i], moe_w_down[i], n_blocks)

    out = final_norm(cfg, xs, final_norm_g.reshape(1, d))
    return out.reshape(cfg.batch, cfg.seq, d)


def kernel(x, c, ctx, c_ctx, w_mod, b_mod, norm_mix_g, norm_ffn_g, attn_w_q, attn_w_kv, attn_q_gain, attn_k_gain, attn_w_o, conv_w_pw1, conv_b_pw1, conv_w_dw, conv_b_dw, conv_ln_g, conv_ln_b, conv_w_pw2, conv_b_pw2, hy_w_in, hy_b_in, hy_w_short, hy_b_short, hy_f_w1, hy_f_b1, hy_f_freq1, hy_f_w2, hy_f_b2, hy_f_freq2, hy_f_w3, hy_long_bias, hy_w_out, hy_b_out, moe_w_group, moe_b_group, moe_w_router, moe_b_router, moe_w_gate, moe_w_up, moe_w_down, final_norm_g):
    return _forward(CFG, x, c, ctx, c_ctx, w_mod, b_mod, norm_mix_g, norm_ffn_g, attn_w_q, attn_w_kv, attn_q_gain, attn_k_gain, attn_w_o, conv_w_pw1, conv_b_pw1, conv_w_dw, conv_b_dw, conv_ln_g, conv_ln_b, conv_w_pw2, conv_b_pw2, hy_w_in, hy_b_in, hy_w_short, hy_b_short, hy_f_w1, hy_f_b1, hy_f_freq1, hy_f_w2, hy_f_b2, hy_f_freq2, hy_f_w3, hy_long_bias, hy_w_out, hy_b_out, moe_w_group, moe_b_group, moe_w_router, moe_b_router, moe_w_gate, moe_w_up, moe_w_down, final_norm_g)
```

```python
import functools
import math
from typing import NamedTuple

import jax
import jax.numpy as jnp
from jax import lax
from jax.experimental import pallas as pl
from jax.experimental.pallas import tpu as pltpu

F32 = jnp.float32
BF16 = jnp.bfloat16
EPS = 1e-6
V7X_VMEM_LIMIT_BYTES = 56 * 1024 * 1024


class Cfg(NamedTuple):
    batch: int
    seq: int
    ctx: int
    d: int
    grid_w: int
    n_heads: int
    n_kv: int
    head_dim: int
    conv_w: int
    n_groups: int
    epg: int
    d_exp: int
    tb: int
    fft_n1: int
    fft_n2: int
    kv_chunk: int

    @property
    def nlat(self):
        return self.batch * self.seq

    @property
    def nt(self):
        return self.batch * (self.seq + self.ctx)

    @property
    def nbl(self):
        return self.nlat // self.tb

    @property
    def nbc(self):
        return self.batch * self.ctx // self.tb

    @property
    def nb(self):
        return self.nbl + self.nbc

    @property
    def spb(self):
        return self.seq // self.tb

    @property
    def cpb(self):
        return self.ctx // self.tb


CFG = Cfg(batch=8, seq=4096, ctx=256, d=1024, grid_w=64, n_heads=8, n_kv=2, head_dim=128, conv_w=31,
          n_groups=4, epg=8, d_exp=256, tb=256, fft_n1=256, fft_n2=32, kv_chunk=512)

MOD_ROWS = 16


def _cparams(sem):
    return pltpu.CompilerParams(dimension_semantics=sem, vmem_limit_bytes=V7X_VMEM_LIMIT_BYTES)


def _split_bf16(a):
    hi = a.astype(BF16)
    lo = (a - hi.astype(F32)).astype(BF16)
    return hi, lo


def _dot(a, b):
    return jnp.dot(a, b, preferred_element_type=F32)


def _dot3(a, b):
    ah, al = _split_bf16(a)
    bh, bl = _split_bf16(b)
    return _dot(ah, bh) + (_dot(ah, bl) + _dot(al, bh))


def _modnorm(x, g, sh, sc):
    ms = jnp.mean(x * x, axis=-1, keepdims=True)
    y = x * lax.rsqrt(ms + EPS) * g
    return y * (1.0 + sc) + sh


def _silu(x):
    return x * jax.nn.sigmoid(x)


def _mod_row(cfg, j):
    return jnp.where(j < cfg.nbl, j // cfg.spb, cfg.batch)


def _mod_spec(cfg, chunk):
    return pl.BlockSpec((1, 1, cfg.d), lambda j: (_mod_row(cfg, j) * 6 + chunk, 0, 0))


def _row_spec(cfg, width):
    return pl.BlockSpec((cfg.tb, width), lambda j: (j, 0))


def _full_spec(shape):
    n = len(shape)
    return pl.BlockSpec(shape, lambda *_: (0,) * n)


def _mod_kernel(c_ref, w_ref, b_ref, o_ref):
    o_ref[0] = _dot3(_silu(c_ref[...]), w_ref[0]) + b_ref[0]


def modulation(c_all, w_mod, b_mod):
    depth, d, n6 = w_mod.shape
    bn = n6 // 4
    return pl.pallas_call(
        _mod_kernel,
        grid=(depth, n6 // bn),
        in_specs=[_full_spec((MOD_ROWS, d)),
                  pl.BlockSpec((1, d, bn), lambda i, n: (i, 0, n)),
                  pl.BlockSpec((1, 1, bn), lambda i, n: (i, 0, n))],
        out_specs=pl.BlockSpec((1, MOD_ROWS, bn), lambda i, n: (i, 0, n)),
        out_shape=jax.ShapeDtypeStruct((depth, MOD_ROWS, n6), F32),
        compiler_params=_cparams(("arbitrary", "arbitrary")),
        name="modulation",
    )(c_all, w_mod, b_mod.reshape(depth, 1, n6))


def _qkv_kernel(x_ref, g_ref, sh_ref, sc_ref, w_ref, cs_ref, sn_ref, qg_ref, kg_ref, o_ref, *, cfg):
    hd = cfg.head_dim
    h = _modnorm(x_ref[...], g_ref[...], sh_ref[0], sc_ref[0]).astype(BF16)
    y = _dot(h, w_ref[...])
    cs = cs_ref[...]
    sn = sn_ref[...]
    lane = lax.broadcasted_iota(jnp.int32, cs.shape, 1)
    first_half = (lane % (hd // 2)) < (hd // 4)

    def norm_rope(v, gain, scale):
        ms = jnp.mean(v * v, axis=-1, keepdims=True)
        v = v * lax.rsqrt(ms + EPS) * gain
        partner = jnp.where(first_half, pltpu.roll(v, hd - hd // 4, 1), pltpu.roll(v, hd // 4, 1))
        return (v * cs + partner * sn) * scale

    nq = cfg.n_heads
    for hh in range(nq):
        sl = slice(hh * hd, (hh + 1) * hd)
        o_ref[:, sl] = norm_rope(y[:, sl], qg_ref[...], hd ** -0.5).astype(o_ref.dtype)
    for hh in range(cfg.n_kv):
        sl = slice((nq + hh) * hd, (nq + hh + 1) * hd)
        o_ref[:, sl] = norm_rope(y[:, sl], kg_ref[...], 1.0).astype(o_ref.dtype)
    v0 = (nq + cfg.n_kv) * hd
    o_ref[:, v0:] = y[:, v0:].astype(o_ref.dtype)


def qkv_project(cfg, x, norm_g, mod, w_qkv, rope_cos, rope_sin, q_gain, k_gain):
    nqkv = w_qkv.shape[1]
    hd = cfg.head_dim

    def rope_idx(j):
        return (jnp.where(j < cfg.nbl, j % cfg.spb, cfg.spb), 0)

    return pl.pallas_call(
        functools.partial(_qkv_kernel, cfg=cfg),
        grid=(cfg.nb,),
        in_specs=[_row_spec(cfg, cfg.d), _full_spec((1, cfg.d)), _mod_spec(cfg, 0), _mod_spec(cfg, 1),
                  _full_spec((cfg.d, nqkv)),
                  pl.BlockSpec((cfg.tb, hd), rope_idx), pl.BlockSpec((cfg.tb, hd), rope_idx),
                  _full_spec((1, hd)), _full_spec((1, hd))],
        out_specs=_row_spec(cfg, nqkv),
        out_shape=jax.ShapeDtypeStruct((cfg.nt, nqkv), BF16),
        compiler_params=_cparams(("arbitrary",)),
        name="qkv_project",
    )(x, norm_g, mod, mod, w_qkv, rope_cos, rope_sin, q_gain, k_gain)


def _attn_kernel(q_ref, kc_ref, vc_ref, kl_ref, vl_ref, o_ref, *, cfg, n_lat_chunks):
    hd = cfg.head_dim
    group = cfg.n_heads // cfg.n_kv
    tq = cfg.tb
    ch = cfg.kv_chunk
    q = jnp.concatenate([q_ref[:, h * hd:(h + 1) * hd] for h in range(group)], axis=0)

    def scores(k):
        return lax.dot_general(q, k, (((1,), (1,)), ((), ())), preferred_element_type=F32)

    s = scores(kc_ref[...])
    m = jnp.max(s, axis=-1, keepdims=True)
    p = jnp.exp(s - m)
    l = jnp.sum(p, axis=-1, keepdims=True)
    acc = _dot(p.astype(BF16), vc_ref[...])

    def body(c, carry):
        m, l, acc = carry
        start = pl.multiple_of(c * ch, ch)
        s = scores(kl_ref[pl.ds(start, ch), :])
        m_new = jnp.maximum(m, jnp.max(s, axis=-1, keepdims=True))
        alpha = jnp.exp(m - m_new)
        p = jnp.exp(s - m_new)
        l = alpha * l + jnp.sum(p, axis=-1, keepdims=True)
        acc = alpha * acc + _dot(p.astype(BF16), vl_ref[pl.ds(start, ch), :])
        return m_new, l, acc

    n = jnp.where(pl.program_id(2) < cfg.spb, n_lat_chunks, 0)
    m, l, acc = lax.fori_loop(0, n, body, (m, l, acc))
    o = acc / l
    for h in range(group):
        o_ref[:, h * hd:(h + 1) * hd] = o[h * tq:(h + 1) * tq].astype(o_ref.dtype)


def attention(cfg, qkv):
    hd = cfg.head_dim
    group = cfg.n_heads // cfg.n_kv
    gw = group * hd
    kcol = cfg.n_heads
    vcol = cfg.n_heads + cfg.n_kv
    assert cfg.cpb == 1

    def qrow(b, qb):
        return jnp.where(qb < cfg.spb, b * cfg.spb + qb, cfg.nbl + b)

    return pl.pallas_call(
        functools.partial(_attn_kernel, cfg=cfg, n_lat_chunks=cfg.seq // cfg.kv_chunk),
        grid=(cfg.batch, cfg.n_kv, cfg.spb + 1),
        in_specs=[pl.BlockSpec((cfg.tb, gw), lambda b, k, qb: (qrow(b, qb), k)),
                  pl.BlockSpec((cfg.ctx, hd), lambda b, k, qb: (cfg.nlat // cfg.ctx + b, kcol + k)),
                  pl.BlockSpec((cfg.ctx, hd), lambda b, k, qb: (cfg.nlat // cfg.ctx + b, vcol + k)),
                  pl.BlockSpec((cfg.seq, hd), lambda b, k, qb: (b, kcol + k)),
                  pl.BlockSpec((cfg.seq, hd), lambda b, k, qb: (b, vcol + k))],
        out_specs=pl.BlockSpec((cfg.tb, gw), lambda b, k, qb: (qrow(b, qb), k)),
        out_shape=jax.ShapeDtypeStruct((cfg.nt, cfg.n_heads * hd), BF16),
        compiler_params=_cparams(("arbitrary", "arbitrary", "arbitrary")),
        name="attention",
    )(qkv, qkv, qkv, qkv, qkv)


def _proj_res_kernel(a_ref, w_ref, b_ref, x_ref, gate_ref, o_ref):
    y = _dot(a_ref[...].astype(BF16), w_ref[...]) + b_ref[...]
    o_ref[...] = x_ref[...] + gate_ref[0] * y


def project_residual(cfg, a, w, bias, x, mod, gate_chunk, n_blocks):
    return pl.pallas_call(
        _proj_res_kernel,
        grid=(n_blocks,),
        in_specs=[_row_spec(cfg, a.shape[1]), _full_spec(w.shape), _full_spec((1, cfg.d)),
                  _row_spec(cfg, cfg.d), _mod_spec(cfg, gate_chunk)],
        out_specs=_row_spec(cfg, cfg.d),
        out_shape=jax.ShapeDtypeStruct((n_blocks * cfg.tb, cfg.d), F32),
        compiler_params=_cparams(("arbitrary",)),
        name="project_residual",
    )(a, w, bias, x, mod)


def _glu_kernel(x_ref, g_ref, sh_ref, sc_ref, w_ref, b_ref, o_ref, *, d):
    h = _modnorm(x_ref[...], g_ref[...], sh_ref[0], sc_ref[0]).astype(BF16)
    y = _dot(h, w_ref[...]) + b_ref[...]
    o_ref[...] = y[:, :d] * jax.nn.sigmoid(y[:, d:])


def glu_project(cfg, x, norm_g, mod, w, bias):
    return pl.pallas_call(
        functools.partial(_glu_kernel, d=cfg.d),
        grid=(cfg.nb,),
        in_specs=[_row_spec(cfg, cfg.d), _full_spec((1, cfg.d)), _mod_spec(cfg, 0), _mod_spec(cfg, 1),
                  _full_spec(w.shape), _full_spec((1, w.shape[1]))],
        out_specs=_row_spec(cfg, cfg.d),
        out_shape=jax.ShapeDtypeStruct((cfg.nt, cfg.d), F32),
        compiler_params=_cparams(("arbitrary",)),
        name="glu_project",
    )(x, norm_g, mod, mod, w, bias)


def _seq_edges(cfg, j):
    lat = j < cfg.nbl
    first = jnp.where(lat, j % cfg.spb == 0, (j - cfg.nbl) % cfg.cpb == 0)
    last = jnp.where(lat, j % cfg.spb == cfg.spb - 1, (j - cfg.nbl) % cfg.cpb == cfg.cpb - 1)
    return first, last


def _halo_specs(cfg, width, halo):
    per = cfg.tb // halo
    last_blk = cfg.nt // halo - 1
    prev = pl.BlockSpec((halo, width), lambda j: (jnp.maximum(j * per - 1, 0), 0))
    nxt = pl.BlockSpec((halo, width), lambda j: (jnp.minimum((j + 1) * per, last_blk), 0))
    return prev, nxt


def _fill_ext(cfg, ext_ref, prev_ref, cur_ref, next_ref, halo):
    first, last = _seq_edges(cfg, pl.program_id(0))
    tb = cfg.tb
    ext_ref[0:halo, :] = jnp.where(first, 0.0, prev_ref[...])
    ext_ref[halo:halo + tb, :] = cur_ref[...]
    ext_ref[halo + tb:2 * halo + tb, :] = jnp.where(last, 0.0, next_ref[...])


def _conv_kernel(up_ref, uc_ref, un_ref, wdw_ref, bdw_ref, lng_ref, lnb_ref, w2_ref, b2_ref, x_ref, gate_ref,
                 o_ref, ext_ref, *, cfg, halo):
    tb = cfg.tb
    _fill_ext(cfg, ext_ref, up_ref, uc_ref, un_ref, halo)
    half = (cfg.conv_w - 1) // 2
    acc = jnp.zeros((tb, cfg.d), F32)
    for k in range(cfg.conv_w):
        off = halo - half + k
        acc = acc + wdw_ref[k:k + 1, :] * ext_ref[off:off + tb, :]
    u = acc + bdw_ref[...]
    mu = jnp.mean(u, axis=-1, keepdims=True)
    uc = u - mu
    u = uc * lax.rsqrt(jnp.mean(uc * uc, axis=-1, keepdims=True) + EPS) * lng_ref[...] + lnb_ref[...]
    u = _silu(u).astype(BF16)
    y = _dot(u, w2_ref[...]) + b2_ref[...]
    o_ref[...] = x_ref[...] + gate_ref[0] * y


def conv_module(cfg, u, w_dw, b_dw, ln_g, ln_b, w2, b2, x, mod):
    halo = 16
    prev, nxt = _halo_specs(cfg, cfg.d, halo)
    return pl.pallas_call(
        functools.partial(_conv_kernel, cfg=cfg, halo=halo),
        grid=(cfg.nb,),
        in_specs=[prev, _row_spec(cfg, cfg.d), nxt, _full_spec(w_dw.shape), _full_spec((1, cfg.d)),
                  _full_spec((1, cfg.d)), _full_spec((1, cfg.d)), _full_spec(w2.shape), _full_spec((1, cfg.d)),
                  _row_spec(cfg, cfg.d), _mod_spec(cfg, 2)],
        out_specs=_row_spec(cfg, cfg.d),
        out_shape=jax.ShapeDtypeStruct((cfg.nt, cfg.d), F32),
        scratch_shapes=[pltpu.VMEM((cfg.tb + 2 * halo, cfg.d), F32)],
        compiler_params=_cparams(("arbitrary",)),
        name="conv_module",
    )(u, u, u, w_dw, b_dw, ln_g, ln_b, w2, b2, x, mod)


def _inproj_kernel(x_ref, g_ref, sh_ref, sc_ref, w_ref, b_ref, o_ref):
    h = _modnorm(x_ref[...], g_ref[...], sh_ref[0], sc_ref[0]).astype(BF16)
    o_ref[...] = _dot(h, w_ref[...]) + b_ref[...]


def in_project(cfg, x, norm_g, mod, w, bias):
    n = w.shape[1]
    return pl.pallas_call(
        _inproj_kernel,
        grid=(cfg.nb,),
        in_specs=[_row_spec(cfg, cfg.d), _full_spec((1, cfg.d)), _mod_spec(cfg, 0), _mod_spec(cfg, 1),
                  _full_spec(w.shape), _full_spec((1, n))],
        out_specs=_row_spec(cfg, n),
        out_shape=jax.ShapeDtypeStruct((cfg.nt, n), F32),
        compiler_params=_cparams(("arbitrary",)),
        name="in_project",
    )(x, norm_g, mod, mod, w, bias)


def _short_conv_kernel(pp_ref, pc_ref, pn_ref, w_ref, b_ref, o_ref, ext_ref, *, cfg, halo, width):
    tb = cfg.tb
    _fill_ext(cfg, ext_ref, pp_ref, pc_ref, pn_ref, halo)
    half = (width - 1) // 2
    acc = b_ref[...]
    for k in range(width):
        off = halo - half + k
        acc = acc + w_ref[k:k + 1, :] * ext_ref[off:off + tb, :]
    o_ref[...] = acc


def short_conv(cfg, p, w, bias):
    halo = 8
    n = p.shape[1]
    bn = cfg.d
    per = cfg.tb // halo
    last_blk = cfg.nt // halo - 1
    return pl.pallas_call(
        functools.partial(_short_conv_kernel, cfg=cfg, halo=halo, width=w.shape[0]),
        grid=(cfg.nb, n // bn),
        in_specs=[pl.BlockSpec((halo, bn), lambda j, c: (jnp.maximum(j * per - 1, 0), c)),
                  pl.BlockSpec((cfg.tb, bn), lambda j, c: (j, c)),
                  pl.BlockSpec((halo, bn), lambda j, c: (jnp.minimum((j + 1) * per, last_blk), c)),
                  pl.BlockSpec((w.shape[0], bn), lambda j, c: (0, c)),
                  pl.BlockSpec((1, bn), lambda j, c: (0, c))],
        out_specs=pl.BlockSpec((cfg.tb, bn), lambda j, c: (j, c)),
        out_shape=jax.ShapeDtypeStruct((cfg.nt, n), F32),
        scratch_shapes=[pltpu.VMEM((cfg.tb + 2 * halo, bn), F32)],
        compiler_params=_cparams(("arbitrary", "arbitrary")),
        name="short_conv",
    )(p, p, p, w, bias)


def _filter_mlp_kernel(feat_ref, w1_ref, b1_ref, f1_ref, w2_ref, b2_ref, f2_ref, w3_ref, delta_ref,
                       h_ref, s_ref, *, length, rows):
    i = pl.program_id(0)
    z = jnp.sin(f1_ref[...] * (_dot3(feat_ref[...], w1_ref[...]) + b1_ref[...]))
    z = jnp.sin(f2_ref[...] * (_dot3(z, w2_ref[...]) + b2_ref[...]))
    h = _dot3(z, w3_ref[...])
    pos = (i * rows + lax.broadcasted_iota(jnp.int32, h.shape, 0)).astype(F32)
    t = pos / (length - 1)
    h = h * jnp.exp(-t * jnp.abs(delta_ref[...]))
    col = lax.broadcasted_iota(jnp.int32, h.shape, 1)
    h = jnp.where((pos == 0.0) & (col >= h.shape[1] // 2), 0.0, h)
    h_ref[...] = h
    part = jnp.sum(jnp.abs(h), axis=0, keepdims=True)

    @pl.when(i == 0)
    def _():
        s_ref[...] = part

    @pl.when(i > 0)
    def _():
        s_ref[...] = s_ref[...] + part


def filter_taps(length, feats, w1, b1, f1, w2, b2, f2, w3, deltas4):
    rows = min(length, 256)
    nh = w1.shape[1]
    n = w3.shape[1]
    return pl.pallas_call(
        functools.partial(_filter_mlp_kernel, length=length, rows=rows),
        grid=(length // rows,),
        in_specs=[pl.BlockSpec((rows, feats.shape[1]), lambda i: (i, 0)), _full_spec(w1.shape),
                  _full_spec((1, nh)), _full_spec((1, nh)), _full_spec(w2.shape), _full_spec((1, nh)),
                  _full_spec((1, nh)), _full_spec(w3.shape), _full_spec((1, n))],
        out_specs=[pl.BlockSpec((rows, n), lambda i: (i, 0)), _full_spec((1, n))],
        out_shape=[jax.ShapeDtypeStruct((length, n), F32), jax.ShapeDtypeStruct((1, n), F32)],
        compiler_params=_cparams(("arbitrary",)),
        name="filter_taps",
    )(feats, w1, b1, f1, w2, b2, f2, w3, deltas4)


def _fft_stage1_kernel(z_ref, m_ref, a_ref):
    a_ref[...] = _dot(m_ref[0], z_ref[...].astype(BF16)).astype(a_ref.dtype)


def fft_stage1(z_view, mats, n_pairs, n2, lane_stride, lane_off, width, out_dtype=BF16):
    rows_out, rows_in = mats.shape[1:]
    return pl.pallas_call(
        _fft_stage1_kernel,
        grid=(n_pairs, n2),
        in_specs=[pl.BlockSpec((rows_in, width), lambda p, s: (p, s * lane_stride + lane_off)),
                  pl.BlockSpec((1, rows_out, rows_in), lambda p, s: (s, 0, 0))],
        out_specs=pl.BlockSpec((rows_out, width), lambda p, s: (p, s)),
        out_shape=jax.ShapeDtypeStruct((n_pairs * rows_out, n2 * width), out_dtype),
        compiler_params=_cparams(("arbitrary", "arbitrary")),
        name="fft_stage1",
    )(z_view, mats)


def _fft_mid_kernel(a_ref, h_ref, bf_ref, bi_ref, o_ref):
    half = a_ref.shape[0] // 2
    x = _dot(bf_ref[...], a_ref[...])
    xr, xi = x[:half], x[half:]
    hr, hi = h_ref[:half, :], h_ref[half:, :]
    y = jnp.concatenate([xr * hr - xi * hi, xr * hi + xi * hr], axis=0).astype(BF16)
    o_ref[...] = _dot(bi_ref[...], y).astype(o_ref.dtype)


def fft_mid(a, h, h_col, bd_fwd, bd_inv, n_pairs, n_groups, width):
    rows = bd_fwd.shape[0]
    a2 = a.reshape(n_pairs * n_groups * rows, width)
    out = pl.pallas_call(
        _fft_mid_kernel,
        grid=(n_groups, n_pairs),
        in_specs=[pl.BlockSpec((rows, width), lambda q, p: (p * n_groups + q, 0)),
                  pl.BlockSpec((rows, width), lambda q, p: (q, h_col)),
                  _full_spec(bd_fwd.shape), _full_spec(bd_inv.shape)],
        out_specs=pl.BlockSpec((rows, width), lambda q, p: (p * n_groups + q, 0)),
        out_shape=jax.ShapeDtypeStruct(a2.shape, BF16),
        compiler_params=_cparams(("arbitrary", "arbitrary")),
        name="fft_mid",
    )(a2, h, bd_fwd, bd_inv)
    return out.reshape(a.shape)


def _fft_spec_kernel(af_ref, ab_ref, bf_ref, nrm_ref, h_ref):
    half = af_ref.shape[0] // 2
    xf = _dot(bf_ref[...], af_ref[...])
    xb = _dot(bf_ref[...], ab_ref[...])
    inv = 1.0 / nrm_ref[...]
    h_ref[:half, :] = (xf[:half] + xb[:half]) * inv
    h_ref[half:, :] = (xf[half:] - xb[half:]) * inv


def fft_filter_spectrum(a, bd_fwd, norm, n_groups, width, n_orders):
    rows = bd_fwd.shape[0]
    a2 = a.reshape(n_groups * rows, 2 * n_orders * width)
    return pl.pallas_call(
        _fft_spec_kernel,
        grid=(n_groups, n_orders),
        in_specs=[pl.BlockSpec((rows, width), lambda q, o: (q, o)),
                  pl.BlockSpec((rows, width), lambda q, o: (q, n_orders + o)),
                  _full_spec(bd_fwd.shape),
                  pl.BlockSpec((1, width), lambda q, o: (0, o))],
        out_specs=pl.BlockSpec((rows, width), lambda q, o: (q, o)),
        out_shape=jax.ShapeDtypeStruct((n_groups * rows, n_orders * width), F32),
        compiler_params=_cparams(("arbitrary", "arbitrary")),
        name="fft_filter_spectrum",
    )(a2, a2, bd_fwd, norm)


def _fft_inv_kernel(b_ref, m_ref, z_ref, gate_ref, bias_ref, o_ref):
    y = _dot(m_ref[0], b_ref[...])
    o_ref[...] = gate_ref[...] * (y + z_ref[...] * bias_ref[...])


def fft_inverse_gate(b, mats, z_view, z_stride, z_off, gate_view, g_stride, g_off, bias, n_pairs, n2, width,
                     out_rows):
    rows_out, rows_in = mats.shape[1:]
    return pl.pallas_call(
        _fft_inv_kernel,
        grid=(n_pairs, n2),
        in_specs=[pl.BlockSpec((rows_in, width), lambda p, s: (p, s)),
                  pl.BlockSpec((1, rows_out, rows_in), lambda p, s: (s, 0, 0)),
                  pl.BlockSpec((rows_out, width), lambda p, s: (p, s * z_stride + z_off)),
                  pl.BlockSpec((rows_out, width), lambda p, s: (p, s * g_stride + g_off)),
                  _full_spec((1, width))],
        out_specs=pl.BlockSpec((rows_out, width), lambda p, s: (p, s)),
        out_shape=jax.ShapeDtypeStruct((out_rows, n2 * width), F32),
        compiler_params=_cparams(("arbitrary", "arbitrary")),
        name="fft_inverse_gate",
    )(b, mats, z_view, gate_view, bias)


def _ctx_spec_kernel(hf_ref, hb_ref, f_ref, nrm_ref, h_ref):
    half = f_ref.shape[0] // 2
    xf = _dot3(f_ref[...], hf_ref[...])
    xb = _dot3(f_ref[...], hb_ref[...])
    inv = 1.0 / nrm_ref[...]
    h_ref[:half, :] = (xf[:half] + xb[:half]) * inv
    h_ref[half:, :] = (xf[half:] - xb[half:]) * inv


def ctx_filter_spectrum(taps, f_real, norm, width, n_orders):
    length = taps.shape[0]
    rows = f_real.shape[0]
    return pl.pallas_call(
        _ctx_spec_kernel,
        grid=(n_orders,),
        in_specs=[pl.BlockSpec((length, width), lambda o: (0, o)),
                  pl.BlockSpec((length, width), lambda o: (0, n_orders + o)),
                  _full_spec(f_real.shape),
                  pl.BlockSpec((1, width), lambda o: (0, o))],
        out_specs=pl.BlockSpec((rows, width), lambda o: (0, o)),
        out_shape=jax.ShapeDtypeStruct((rows, n_orders * width), F32),
        compiler_params=_cparams(("arbitrary",)),
        name="ctx_filter_spectrum",
    )(taps, taps, f_real, norm)


def _ctx_conv_kernel(v_ref, g1_ref, g2_ref, h_ref, ff_ref, fi_ref, bias_ref, zin_ref, o_ref, *, width):
    del zin_ref
    half = ff_ref.shape[0] // 2
    z = v_ref[...]
    for n, gate_ref in enumerate((g1_ref, g2_ref)):
        x = _dot(ff_ref[...], z.astype(BF16))
        xr, xi = x[:half], x[half:]
        hr = h_ref[:half, n * width:(n + 1) * width]
        hi = h_ref[half:, n * width:(n + 1) * width]
        y = jnp.concatenate([xr * hr - xi * hi, xr * hi + xi * hr], axis=0).astype(BF16)
        y = _dot(fi_ref[...], y)
        z = gate_ref[...] * (y + z * bias_ref[n:n + 1, :])
    o_ref[...] = z


def ctx_long_conv(cfg, u, h_ctx, f_fwd, f_inv, bias, z_out):
    d = cfg.d
    rows = 2 * cfg.ctx
    base = cfg.nlat // rows
    return pl.pallas_call(
        functools.partial(_ctx_conv_kernel, width=d),
        grid=(cfg.batch // 2,),
        in_specs=[pl.BlockSpec((rows, d), lambda p: (base + p, 0)),
                  pl.BlockSpec((rows, d), lambda p: (base + p, 1)),
                  pl.BlockSpec((rows, d), lambda p: (base + p, 2)),
                  _full_spec(h_ctx.shape), _full_spec(f_fwd.shape), _full_spec(f_inv.shape),
                  _full_spec(bias.shape), pl.BlockSpec(memory_space=pl.ANY)],
        out_specs=pl.BlockSpec((rows, d), lambda p: (base + p, 0)),
        out_shape=jax.ShapeDtypeStruct(z_out.shape, F32),
        input_output_aliases={7: 0},
        compiler_params=_cparams(("arbitrary",)),
        name="ctx_long_conv",
    )(u, u, u, h_ctx, f_fwd, f_inv, bias, z_out)


def _dft_tables(n1, n2, group):
    n = n1 * n2
    h1 = n1 // 2
    f1 = jnp.arange(n1, dtype=jnp.int32)[None, :, None]
    s1 = jnp.arange(h1, dtype=jnp.int32)[None, None, :]
    s2 = jnp.arange(n2, dtype=jnp.int32)[:, None, None]
    k = (f1 * (n2 * s1 + s2)) % n
    ang = (2.0 * math.pi / n) * k.astype(F32)
    er, ei = jnp.cos(ang), -jnp.sin(ang)
    fwd = jnp.concatenate([jnp.concatenate([er, -ei], axis=2),
                           jnp.concatenate([ei, er], axis=2)], axis=1)
    cr, ci = jnp.swapaxes(er, 1, 2) / n, -jnp.swapaxes(ei, 1, 2) / n
    inv = jnp.concatenate([jnp.concatenate([cr, -ci], axis=2),
                           jnp.concatenate([ci, cr], axis=2)], axis=1)
    q = jnp.arange(n1 // group)[:, None, None]
    ri = jnp.arange(2)[None, :, None]
    j = jnp.arange(group)[None, None, :]
    perm = (ri * n1 + q * group + j).reshape(-1)
    fwd = fwd[:, perm, :]
    inv = inv[:, :, perm]
    real_only = fwd[:, :, :h1]
    a = jnp.arange(n2, dtype=jnp.int32)
    ang2 = (2.0 * math.pi / n2) * ((a[:, None] * a[None, :]) % n2).astype(F32)
    f2r, f2i = jnp.cos(ang2), -jnp.sin(ang2)
    eye = jnp.eye(group, dtype=F32)
    kr, ki = jnp.kron(eye, f2r), jnp.kron(eye, f2i)
    bd_fwd = jnp.concatenate([jnp.concatenate([kr, -ki], axis=1), jnp.concatenate([ki, kr], axis=1)], axis=0)
    bd_inv = jnp.concatenate([jnp.concatenate([kr, ki], axis=1), jnp.concatenate([-ki, kr], axis=1)], axis=0)
    return (fwd.astype(BF16), inv.astype(BF16), real_only.astype(BF16), bd_fwd.astype(BF16),
            bd_inv.astype(BF16))


def _ctx_dft_tables(length):
    n = 2 * length
    f = jnp.arange(n, dtype=jnp.int32)[:, None]
    s = jnp.arange(length, dtype=jnp.int32)[None, :]
    ang = (2.0 * math.pi / n) * ((f * s) % n).astype(F32)
    fr, fi = jnp.cos(ang), -jnp.sin(ang)
    fwd = jnp.concatenate([jnp.concatenate([fr, -fi], axis=1), jnp.concatenate([fi, fr], axis=1)], axis=0)
    frt, fit = fr.T / n, fi.T / n
    inv = jnp.concatenate([jnp.concatenate([frt, fit], axis=1), jnp.concatenate([-fit, frt], axis=1)], axis=0)
    real_only = jnp.concatenate([fr, fi], axis=0)
    return fwd.astype(BF16), inv.astype(BF16), real_only


def _filter_features(length, bands):
    pos = jnp.arange(length, dtype=F32)[:, None]
    t = pos / (length - 1)
    w = 2.0 * math.pi * pos / length
    bnd = jnp.linspace(1e-4, bands - 1, bands, dtype=F32)
    feats = jnp.concatenate([t, jnp.cos(bnd * w), -jnp.sin(bnd * w)], axis=-1)
    return jnp.pad(feats, ((0, 0), (0, 128 - feats.shape[1])))


def hyena_long_convs(cfg, u, fp, long_bias):
    d = cfg.d
    n1, n2 = cfg.fft_n1, cfg.fft_n2
    group = 4
    n_groups = n1 // group
    n_orders = 2
    n_pairs = cfg.batch // 2
    nrow = cfg.nt // n2
    (w1, b1, fq1, w2, b2, fq2, w3) = fp
    nh = w1.shape[1]
    w1p = jnp.pad(w1, ((0, 128 - w1.shape[0]), (0, 0)))
    bands = (w1.shape[0] - 1) // 2
    deltas = jnp.linspace(math.log(1e-2) / 1.5, math.log(1e-2) / 0.3, d, dtype=F32)
    deltas4 = jnp.tile(deltas, 2 * n_orders)[None, :]
    mlp = (w1p, b1.reshape(1, nh), fq1.reshape(1, nh), w2, b2.reshape(1, nh), fq2.reshape(1, nh), w3, deltas4)

    fwd, inv, real_only, bd_fwd, bd_inv = _dft_tables(n1, n2, group)

    taps, sums = filter_taps(cfg.seq, _filter_features(cfg.seq, bands), *mlp)
    norm = sums[:, :n_orders * d] + sums[:, n_orders * d:]
    a_f = fft_stage1(taps.reshape(n1 // 2, n2 * 2 * n_orders * d), real_only, 1, n2, 1, 0, 2 * n_orders * d)
    h_lat = fft_filter_spectrum(a_f, bd_fwd, norm, n_groups, d, n_orders)

    cf_fwd, cf_inv, cf_real = _ctx_dft_tables(cfg.ctx)
    taps_c, sums_c = filter_taps(cfg.ctx, _filter_features(cfg.ctx, bands), *mlp)
    norm_c = sums_c[:, :n_orders * d] + sums_c[:, n_orders * d:]
    h_ctx = ctx_filter_spectrum(taps_c, cf_real, norm_c, d, n_orders)

    u_view = u.reshape(nrow, n2 * 3 * d)
    z_view, z_stride, z_off = u_view, 3, 0
    z = None
    for order in range(n_orders):
        a = fft_stage1(z_view, fwd, n_pairs, n2, z_stride, z_off, d)
        bmid = fft_mid(a, h_lat, order, bd_fwd, bd_inv, n_pairs, n_groups, d)
        z = fft_inverse_gate(bmid, inv, z_view, z_stride, z_off, u_view, 3, 1 + order,
                             long_bias[order:order + 1], n_pairs, n2, d, nrow)
        z_view, z_stride, z_off = z, 1, 0
    z2 = z.reshape(cfg.nt, d)
    return ctx_long_conv(cfg, u, h_ctx, cf_fwd, cf_inv, long_bias, z2)


META_E, META_W, META_RANK = 0, 2, 4


def _route_kernel(x_ref, g_ref, sh_ref, sc_ref, wr_ref, br_ref, h_ref, meta_ref, cnt_ref, run_ref, *, cfg):
    ne = cfg.n_groups * cfg.epg
    step = pl.program_id(0)
    h = _modnorm(x_ref[...], g_ref[...], sh_ref[0], sc_ref[0])
    h_ref[...] = h.astype(h_ref.dtype)
    logits = _dot3(h, wr_ref[...]) + br_ref[...]
    lane = lax.broadcasted_iota(jnp.int32, logits.shape, 1).astype(F32)
    neg = -jnp.inf
    big = 1e9

    def first_argmax(mask):
        v = jnp.where(mask, logits, neg)
        mx = jnp.max(v, axis=-1, keepdims=True)
        idx = jnp.min(jnp.where(mask & (logits == mx), lane, big), axis=-1, keepdims=True)
        return mx, idx

    gmask = (lane >= ne) & (lane < ne + cfg.n_groups)
    gmax, gidx = first_argmax(gmask)
    g_p = 1.0 / jnp.sum(jnp.where(gmask, jnp.exp(logits - gmax), 0.0), axis=-1, keepdims=True)
    e0 = (gidx - ne) * cfg.epg
    emask = (lane >= e0) & (lane < e0 + cfg.epg)
    m1, i1 = first_argmax(emask)
    m2, i2 = first_argmax(emask & (lane != i1))
    r = jnp.exp(m2 - m1)
    w1 = g_p / (1.0 + r)
    w2 = g_p * r / (1.0 + r)

    @pl.when(step == 0)
    def _():
        run_ref[...] = jnp.zeros_like(run_ref)

    hit1 = lane == i1
    hit2 = lane == i2
    onehot = jnp.where(hit1 | hit2, 1.0, 0.0)
    tb = onehot.shape[0]
    row = lax.broadcasted_iota(jnp.int32, (tb, tb), 0)
    col = lax.broadcasted_iota(jnp.int32, (tb, tb), 1)
    earlier = jnp.where(col < row, 1.0, 0.0).astype(BF16)
    before = _dot(earlier, onehot.astype(BF16)) + run_ref[...]
    rank1 = jnp.sum(jnp.where(hit1, before, 0.0), axis=-1, keepdims=True)
    rank2 = jnp.sum(jnp.where(hit2, before, 0.0), axis=-1, keepdims=True)
    run_ref[...] = run_ref[...] + jnp.sum(onehot, axis=0, keepdims=True)
    cnt_ref[...] = run_ref[...]

    meta = jnp.zeros_like(logits)
    for k, v in enumerate((i1, i2, w1, w2, rank1, rank2)):
        meta = jnp.where(lane == float(k), v, meta)
    meta_ref[...] = meta


def route(cfg, x, norm_g, mod, w_route, b_route, n_blocks):
    return pl.pallas_call(
        functools.partial(_route_kernel, cfg=cfg),
        grid=(n_blocks,),
        in_specs=[_row_spec(cfg, cfg.d), _full_spec((1, cfg.d)), _mod_spec(cfg, 3), _mod_spec(cfg, 4),
                  _full_spec(w_route.shape), _full_spec((1, 128))],
        out_specs=[_row_spec(cfg, cfg.d), _row_spec(cfg, 128), _full_spec((1, 128))],
        out_shape=[jax.ShapeDtypeStruct((n_blocks * cfg.tb, cfg.d), F32),
                   jax.ShapeDtypeStruct((n_blocks * cfg.tb, 128), F32),
                   jax.ShapeDtypeStruct((1, 128), F32)],
        scratch_shapes=[pltpu.VMEM((1, 128), F32)],
        compiler_params=_cparams(("arbitrary",)),
        name="route",
    )(x, norm_g, mod, mod, w_route, b_route)


def _row_gather(src_hbm, idx_ref, dst, sem, n_rows):
    def copy(r):
        return pltpu.make_async_copy(src_hbm.at[pl.ds(idx_ref[0, 0, r], 1), :], dst.at[pl.ds(r, 1), :], sem)

    def start():
        def body(r, carry):
            copy(r).start()
            return carry
        lax.fori_loop(0, n_rows, body, 0, unroll=8)

    def wait():
        def body(r, carry):
            copy(r).wait()
            return carry
        lax.fori_loop(0, n_rows, body, 0, unroll=8)

    return start, wait


def _expert_kernel(be_ref, idx_ref, idx_next_ref, h_hbm, wg_ref, wu_ref, wd_ref, y_ref, hbuf, sem, *, tb, nblk):
    del be_ref
    i = pl.program_id(0)
    slot = i % 2
    start_cur, wait_cur = _row_gather(h_hbm, idx_ref, hbuf.at[slot], sem.at[slot], tb)
    start_next, _ = _row_gather(h_hbm, idx_next_ref, hbuf.at[1 - slot], sem.at[1 - slot], tb)

    @pl.when(i == 0)
    def _():
        start_cur()

    @pl.when(i + 1 < nblk)
    def _():
        start_next()

    wait_cur()
    hb = hbuf[slot].astype(BF16)
    a = _dot(hb, wg_ref[0].astype(BF16))
    b = _dot(hb, wu_ref[0].astype(BF16))
    hid = (_silu(a) * b).astype(BF16)
    y_ref[...] = _dot(hid, wd_ref[0].astype(BF16))


def expert_mlp(cfg, h, src, block_expert, w_gate, w_up, w_down, nblk):
    tb = cfg.tb
    de = cfg.d_exp
    grid_spec = pltpu.PrefetchScalarGridSpec(
        num_scalar_prefetch=1,
        grid=(nblk,),
        in_specs=[pl.BlockSpec((1, 1, tb), lambda i, be: (i, 0, 0), memory_space=pltpu.SMEM),
                  pl.BlockSpec((1, 1, tb), lambda i, be: (jnp.minimum(i + 1, nblk - 1), 0, 0),
                               memory_space=pltpu.SMEM),
                  pl.BlockSpec(memory_space=pl.ANY),
                  pl.BlockSpec((1, cfg.d, de), lambda i, be: (be[i], 0, 0)),
                  pl.BlockSpec((1, cfg.d, de), lambda i, be: (be[i], 0, 0)),
                  pl.BlockSpec((1, de, cfg.d), lambda i, be: (be[i], 0, 0))],
        out_specs=pl.BlockSpec((tb, cfg.d), lambda i, be: (i, 0)),
        scratch_shapes=[pltpu.VMEM((2, tb, cfg.d), F32), pltpu.SemaphoreType.DMA((2,))],
    )
    return pl.pallas_call(
        functools.partial(_expert_kernel, tb=tb, nblk=nblk),
        grid_spec=grid_spec,
        out_shape=jax.ShapeDtypeStruct((nblk * tb, cfg.d), F32),
        compiler_params=_cparams(("arbitrary",)),
        name="expert_mlp",
    )(block_expert, src, src, h, w_gate, w_up, w_down)


def _combine_kernel(pos_ref, pos_next_ref, x_ref, meta_ref, gate_ref, y_hbm, o_ref, ybuf, sem, *, tb, nblk):
    i = pl.program_id(0)
    slot = i % 2
    start_cur, wait_cur = _row_gather(y_hbm, pos_ref, ybuf.at[slot], sem.at[slot], 2 * tb)
    start_next, _ = _row_gather(y_hbm, pos_next_ref, ybuf.at[1 - slot], sem.at[1 - slot], 2 * tb)

    @pl.when(i == 0)
    def _():
        start_cur()

    @pl.when(i + 1 < nblk)
    def _():
        start_next()

    wait_cur()
    meta = meta_ref[...]
    w1 = meta[:, META_W:META_W + 1]
    w2 = meta[:, META_W + 1:META_W + 2]
    y = w1 * ybuf[slot, 0:tb, :] + w2 * ybuf[slot, tb:2 * tb, :]
    o_ref[...] = x_ref[...] + gate_ref[0] * y


def combine_residual(cfg, x, y_sorted, pos, meta, mod, gate_chunk, n_blocks):
    tb = cfg.tb
    return pl.pallas_call(
        functools.partial(_combine_kernel, tb=tb, nblk=n_blocks),
        grid=(n_blocks,),
        in_specs=[pl.BlockSpec((1, 1, 2 * tb), lambda j: (j, 0, 0), memory_space=pltpu.SMEM),
                  pl.BlockSpec((1, 1, 2 * tb), lambda j: (jnp.minimum(j + 1, n_blocks - 1), 0, 0),
                               memory_space=pltpu.SMEM),
                  _row_spec(cfg, cfg.d), _row_spec(cfg, 128), _mod_spec(cfg, gate_chunk),
                  pl.BlockSpec(memory_space=pl.ANY)],
        out_specs=_row_spec(cfg, cfg.d),
        out_shape=jax.ShapeDtypeStruct((n_blocks * tb, cfg.d), F32),
        scratch_shapes=[pltpu.VMEM((2, 2 * tb, cfg.d), F32), pltpu.SemaphoreType.DMA((2,))],
        compiler_params=_cparams(("arbitrary",)),
        name="combine_residual",
    )(pos, pos, x, meta, mod, y_sorted)


def _final_norm_kernel(x_ref, g_ref, o_ref):
    x = x_ref[...]
    o_ref[...] = x * lax.rsqrt(jnp.mean(x * x, axis=-1, keepdims=True) + EPS) * g_ref[...]


def final_norm(cfg, x, g):
    return pl.pallas_call(
        _final_norm_kernel,
        grid=(cfg.nbl,),
        in_specs=[_row_spec(cfg, cfg.d), _full_spec((1, cfg.d))],
        out_specs=_row_spec(cfg, cfg.d),
        out_shape=jax.ShapeDtypeStruct((cfg.nlat, cfg.d), F32),
        compiler_params=_cparams(("arbitrary",)),
        name="final_norm",
    )(x, g)


def _rope_tables(cfg):
    hd = cfg.head_dim
    pairs = hd // 4
    rows = cfg.seq // cfg.grid_w
    row = jnp.repeat(jnp.arange(rows), cfg.grid_w).astype(F32)
    col = jnp.tile(jnp.arange(cfg.grid_w), rows).astype(F32)
    inv_freq = 10000.0 ** (-jnp.arange(pairs, dtype=F32) / pairs)
    ar, ac = row[:, None] * inv_freq, col[:, None] * inv_freq
    cos = jnp.concatenate([jnp.cos(ar), jnp.cos(ar), jnp.cos(ac), jnp.cos(ac)], axis=-1)
    sin = jnp.concatenate([-jnp.sin(ar), jnp.sin(ar), -jnp.sin(ac), jnp.sin(ac)], axis=-1)
    cos = jnp.concatenate([cos, jnp.ones((cfg.tb, hd), F32)], axis=0)
    sin = jnp.concatenate([sin, jnp.zeros((cfg.tb, hd), F32)], axis=0)
    return cos, sin


def _moe_layer(cfg, x, norm_g, mod, w_group, b_group, w_router, b_router, w_gate, w_up, w_down, n_blocks):
    ne = cfg.n_groups * cfg.epg
    w_route = jnp.pad(jnp.concatenate([w_router, w_group], axis=1), ((0, 0), (0, 128 - ne - cfg.n_groups)))
    b_route = jnp.pad(jnp.concatenate([b_router, b_group]), (0, 128 - ne - cfg.n_groups)).reshape(1, 128)
    h, meta, counts = route(cfg, x, norm_g, mod, w_route, b_route, n_blocks)

    tb = cfg.tb
    n = n_blocks * tb
    nblk = 2 * n_blocks + ne
    e_ids = meta[:, META_E:META_E + 2].astype(jnp.int32)
    ranks = meta[:, META_RANK:META_RANK + 2].astype(jnp.int32)
    padded = ((counts[0, :ne].astype(jnp.int32) + tb - 1) // tb) * tb
    seg_end = jnp.cumsum(padded)
    pos = (seg_end - padded)[e_ids] + ranks
    src = jnp.zeros((nblk * tb,), jnp.int32).at[pos.reshape(-1)].set(jnp.repeat(jnp.arange(n, dtype=jnp.int32), 2))
    block_expert = jnp.minimum(jnp.searchsorted(seg_end, jnp.arange(nblk, dtype=jnp.int32) * tb, side="right"),
                               ne - 1).astype(jnp.int32)
    pos_blocks = pos.reshape(n_blocks, tb, 2).transpose(0, 2, 1).reshape(n_blocks, 1, 2 * tb)

    y = expert_mlp(cfg, h, src.reshape(nblk, 1, tb), block_expert, w_gate.reshape(ne, cfg.d, cfg.d_exp),
                   w_up.reshape(ne, cfg.d, cfg.d_exp), w_down.reshape(ne, cfg.d_exp, cfg.d), nblk)
    return combine_residual(cfg, x, y, pos_blocks, meta, mod, 5, n_blocks)


def _forward(cfg, x, c, ctx, c_ctx, w_mod, b_mod, norm_mix_g, norm_ffn_g,
             attn_w_q, attn_w_kv, attn_q_gain, attn_k_gain, attn_w_o,
             conv_w_pw1, conv_b_pw1, conv_w_dw, conv_b_dw, conv_ln_g, conv_ln_b, conv_w_pw2, conv_b_pw2,
             hy_w_in, hy_b_in, hy_w_short, hy_b_short, hy_f_w1, hy_f_b1, hy_f_freq1, hy_f_w2, hy_f_b2,
             hy_f_freq2, hy_f_w3, hy_long_bias, hy_w_out, hy_b_out,
             moe_w_group, moe_b_group, moe_w_router, moe_b_router, moe_w_gate, moe_w_up, moe_w_down,
             final_norm_g):
    d = cfg.d
    depth = w_mod.shape[0]
    xs = jnp.concatenate([x.reshape(-1, d), ctx.reshape(-1, d)], axis=0)
    c_all = jnp.concatenate([c, c_ctx[None, :], jnp.zeros((MOD_ROWS - cfg.batch - 1, d), F32)], axis=0)
    mods = modulation(c_all, w_mod, b_mod)
    rope_cos, rope_sin = _rope_tables(cfg)
    zero_bias = jnp.zeros((1, d), F32)

    for i in range(depth):
        kind, slot = i % 3, i // 3
        last = i == depth - 1
        n_blocks = cfg.nbl if last else cfg.nb
        mod = mods[i].reshape(MOD_ROWS * 6, 1, d)
        g_mix = norm_mix_g[i].reshape(1, d)
        if kind == 0:
            w_qkv = jnp.concatenate([attn_w_q[slot], attn_w_kv[slot]], axis=1).astype(BF16)
            qkv = qkv_project(cfg, xs, g_mix, mod, w_qkv, rope_cos, rope_sin,
                              attn_q_gain[slot].reshape(1, -1), attn_k_gain[slot].reshape(1, -1))
            o = attention(cfg, qkv)
            xs = project_residual(cfg, o, attn_w_o[slot].astype(BF16), zero_bias, xs, mod, 2, n_blocks)
        elif kind == 1:
            u = glu_project(cfg, xs, g_mix, mod, conv_w_pw1[slot].astype(BF16), conv_b_pw1[slot].reshape(1, -1))
            xs = conv_module(cfg, u, conv_w_dw[slot], conv_b_dw[slot].reshape(1, d), conv_ln_g[slot].reshape(1, d),
                             conv_ln_b[slot].reshape(1, d), conv_w_pw2[slot].astype(BF16),
                             conv_b_pw2[slot].reshape(1, d), xs, mod)
        else:
            p = in_project(cfg, xs, g_mix, mod, hy_w_in[slot].astype(BF16), hy_b_in[slot].reshape(1, -1))
            u = short_conv(cfg, p, hy_w_short[slot], hy_b_short[slot].reshape(1, -1))
            fp = (hy_f_w1[slot], hy_f_b1[slot], hy_f_freq1[slot], hy_f_w2[slot], hy_f_b2[slot],
                  hy_f_freq2[slot], hy_f_w3[slot])
            z2 = hyena_long_convs(cfg, u, fp, hy_long_bias[slot])
            xs = project_residual(cfg, z2, hy_w_out[slot].astype(BF16), hy_b_out[slot].reshape(1, d), xs, mod, 2,
                                  n_blocks)
        xs = _moe_layer(cfg, xs, norm_ffn_g[i].reshape(1, d), mod, moe_w_group[i], moe_b_group[i],
                        moe_w_router[i], moe_b_router[i], moe_w_gate[i], moe_w_up[i], moe_w_down[i], n_blocks)

    out = final_norm(cfg, xs, final_norm_g.reshape(1, d))
    return out.reshape(cfg.batch, cfg.seq, d)


def kernel(x, c, ctx, c_ctx, w_mod, b_mod, norm_mix_g, norm_ffn_g, attn_w_q, attn_w_kv, attn_q_gain, attn_k_gain, attn_w_o, conv_w_pw1, conv_b_pw1, conv_w_dw, conv_b_dw, conv_ln_g, conv_ln_b, conv_w_pw2, conv_b_pw2, hy_w_in, hy_b_in, hy_w_short, hy_b_short, hy_f_w1, hy_f_b1, hy_f_freq1, hy_f_w2, hy_f_b2, hy_f_freq2, hy_f_w3, hy_long_bias, hy_w_out, hy_b_out, moe_w_group, moe_b_group, moe_w_router, moe_b_router, moe_w_gate, moe_w_up, moe_w_down, final_norm_g):
    return _forward(CFG, x, c, ctx, c_ctx, w_mod, b_mod, norm_mix_g, norm_ffn_g, attn_w_q, attn_w_kv, attn_q_gain, attn_k_gain, attn_w_o, conv_w_pw1, conv_b_pw1, conv_w_dw, conv_b_dw, conv_ln_g, conv_ln_b, conv_w_pw2, conv_b_pw2, hy_w_in, hy_b_in, hy_w_short, hy_b_short, hy_f_w1, hy_f_b1, hy_f_freq1, hy_f_w2, hy_f_b2, hy_f_freq2, hy_f_w3, hy_long_bias, hy_w_out, hy_b_out, moe_w_group, moe_b_group, moe_w_router, moe_b_router, moe_w_gate, moe_w_up, moe_w_down, final_norm_g)
```

```python
import functools
import math
from typing import NamedTuple

import jax
import jax.numpy as jnp
from jax import lax
from jax.experimental import pallas as pl
from jax.experimental.pallas import tpu as pltpu

F32 = jnp.float32
BF16 = jnp.bfloat16
EPS = 1e-6
V7X_VMEM_LIMIT_BYTES = 56 * 1024 * 1024


class Cfg(NamedTuple):
    batch: int
    seq: int
    ctx: int
    d: int
    grid_w: int
    n_heads: int
    n_kv: int
    head_dim: int
    conv_w: int
    n_groups: int
    epg: int
    d_exp: int
    tb: int
    fft_n1: int
    fft_n2: int
    kv_chunk: int
    attn_tile: int

    @property
    def nlat(self):
        return self.batch * self.seq

    @property
    def nt(self):
        return self.batch * (self.seq + self.ctx)

    @property
    def nbl(self):
        return self.nlat // self.tb

    @property
    def nbc(self):
        return self.batch * self.ctx // self.tb

    @property
    def nb(self):
        return self.nbl + self.nbc

    @property
    def spb(self):
        return self.seq // self.tb

    @property
    def cpb(self):
        return self.ctx // self.tb


CFG = Cfg(batch=8, seq=4096, ctx=256, d=1024, grid_w=64, n_heads=8, n_kv=2, head_dim=128, conv_w=31,
          n_groups=4, epg=8, d_exp=256, tb=256, fft_n1=256, fft_n2=32, kv_chunk=1024, attn_tile=1024)

MOD_ROWS = 16


def _cparams(sem):
    return pltpu.CompilerParams(dimension_semantics=sem, vmem_limit_bytes=V7X_VMEM_LIMIT_BYTES)


def _split_bf16(a):
    hi = a.astype(BF16)
    lo = (a - hi.astype(F32)).astype(BF16)
    return hi, lo


def _dot(a, b):
    return jnp.dot(a, b, preferred_element_type=F32)


def _dot3(a, b):
    ah, al = _split_bf16(a)
    bh, bl = _split_bf16(b)
    return _dot(ah, bh) + (_dot(ah, bl) + _dot(al, bh))


def _modnorm(x, g, sh, sc):
    ms = jnp.mean(x * x, axis=-1, keepdims=True)
    y = x * lax.rsqrt(ms + EPS) * g
    return y * (1.0 + sc) + sh


def _silu(x):
    return x * jax.nn.sigmoid(x)


def _mod_row(cfg, j):
    return jnp.where(j < cfg.nbl, j // cfg.spb, cfg.batch)


def _mod_spec(cfg, chunk):
    return pl.BlockSpec((1, 1, cfg.d), lambda j: (_mod_row(cfg, j) * 6 + chunk, 0, 0))


def _row_spec(cfg, width):
    return pl.BlockSpec((cfg.tb, width), lambda j: (j, 0))


def _full_spec(shape):
    n = len(shape)
    return pl.BlockSpec(shape, lambda *_: (0,) * n)


def _mod_kernel(c_ref, w_ref, b_ref, o_ref):
    o_ref[0] = _dot3(_silu(c_ref[...]), w_ref[0]) + b_ref[0]


def modulation(c_all, w_mod, b_mod):
    depth, d, n6 = w_mod.shape
    bn = n6 // 4
    return pl.pallas_call(
        _mod_kernel,
        grid=(depth, n6 // bn),
        in_specs=[_full_spec((MOD_ROWS, d)),
                  pl.BlockSpec((1, d, bn), lambda i, n: (i, 0, n)),
                  pl.BlockSpec((1, 1, bn), lambda i, n: (i, 0, n))],
        out_specs=pl.BlockSpec((1, MOD_ROWS, bn), lambda i, n: (i, 0, n)),
        out_shape=jax.ShapeDtypeStruct((depth, MOD_ROWS, n6), F32),
        compiler_params=_cparams(("arbitrary", "arbitrary")),
        name="modulation",
    )(c_all, w_mod, b_mod.reshape(depth, 1, n6))


def _qkv_kernel(x_ref, g_ref, sh_ref, sc_ref, w_ref, cs_ref, sn_ref, qg_ref, kg_ref, o_ref, *, cfg):
    hd = cfg.head_dim
    h = _modnorm(x_ref[...], g_ref[...], sh_ref[0], sc_ref[0]).astype(BF16)
    y = _dot(h, w_ref[...])
    cs = cs_ref[...]
    sn = sn_ref[...]
    lane = lax.broadcasted_iota(jnp.int32, cs.shape, 1)
    first_half = (lane % (hd // 2)) < (hd // 4)

    def norm_rope(v, gain, scale):
        ms = jnp.mean(v * v, axis=-1, keepdims=True)
        v = v * lax.rsqrt(ms + EPS) * gain
        partner = jnp.where(first_half, pltpu.roll(v, hd - hd // 4, 1), pltpu.roll(v, hd // 4, 1))
        return (v * cs + partner * sn) * scale

    nq = cfg.n_heads
    for hh in range(nq):
        sl = slice(hh * hd, (hh + 1) * hd)
        o_ref[:, sl] = norm_rope(y[:, sl], qg_ref[...], hd ** -0.5).astype(o_ref.dtype)
    for hh in range(cfg.n_kv):
        sl = slice((nq + hh) * hd, (nq + hh + 1) * hd)
        o_ref[:, sl] = norm_rope(y[:, sl], kg_ref[...], 1.0).astype(o_ref.dtype)
    v0 = (nq + cfg.n_kv) * hd
    o_ref[:, v0:] = y[:, v0:].astype(o_ref.dtype)


def qkv_project(cfg, x, norm_g, mod, w_qkv, rope_cos, rope_sin, q_gain, k_gain):
    nqkv = w_qkv.shape[1]
    hd = cfg.head_dim

    def rope_idx(j):
        return (jnp.where(j < cfg.nbl, j % cfg.spb, cfg.spb), 0)

    return pl.pallas_call(
        functools.partial(_qkv_kernel, cfg=cfg),
        grid=(cfg.nb,),
        in_specs=[_row_spec(cfg, cfg.d), _full_spec((1, cfg.d)), _mod_spec(cfg, 0), _mod_spec(cfg, 1),
                  _full_spec((cfg.d, nqkv)),
                  pl.BlockSpec((cfg.tb, hd), rope_idx), pl.BlockSpec((cfg.tb, hd), rope_idx),
                  _full_spec((1, hd)), _full_spec((1, hd))],
        out_specs=_row_spec(cfg, nqkv),
        out_shape=jax.ShapeDtypeStruct((cfg.nt, nqkv), BF16),
        compiler_params=_cparams(("arbitrary",)),
        name="qkv_project",
    )(x, norm_g, mod, mod, w_qkv, rope_cos, rope_sin, q_gain, k_gain)


def _transpose_bf16(x):
    return x.astype(F32).T.astype(BF16)


def _attn_kernel(q_ref, kc_ref, vc_ref, kl_ref, vl_ref, o_ref, vtc_ref, vtl_ref, *, cfg, n_lat_chunks):
    hd = cfg.head_dim
    group = cfg.n_heads // cfg.n_kv
    tq = cfg.tb
    ch = cfg.kv_chunk

    @pl.when(pl.program_id(2) == 0)
    def _():
        vtc_ref[...] = _transpose_bf16(vc_ref[...])
        for c in range(n_lat_chunks):
            vtl_ref[c] = _transpose_bf16(vl_ref[c * ch:(c + 1) * ch, :])

    qt = jnp.concatenate([_transpose_bf16(q_ref[:, h * hd:(h + 1) * hd]) for h in range(group)], axis=1)

    s = _dot(kc_ref[...], qt)
    m = jnp.max(s, axis=0, keepdims=True)
    p = jnp.exp(s - m)
    l = jnp.sum(p, axis=0, keepdims=True)
    acc = _dot(vtc_ref[...], p.astype(BF16))

    tw = cfg.attn_tile

    def body(c, carry):
        m, l, acc = carry
        start = pl.multiple_of(c * ch, ch)
        k = kl_ref[pl.ds(start, ch), :]
        vt = vtl_ref[c]
        ms, ls, accs = [], [], []
        for j in range(group * tq // tw):
            sl = slice(j * tw, (j + 1) * tw)
            s = _dot(k, qt[:, sl])
            m_new = jnp.maximum(m[:, sl], jnp.max(s, axis=0, keepdims=True))
            alpha = jnp.exp(m[:, sl] - m_new)
            p = jnp.exp(s - m_new)
            ms.append(m_new)
            ls.append(alpha * l[:, sl] + jnp.sum(p, axis=0, keepdims=True))
            accs.append(alpha * acc[:, sl] + _dot(vt, p.astype(BF16)))
        return jnp.concatenate(ms, axis=1), jnp.concatenate(ls, axis=1), jnp.concatenate(accs, axis=1)

    n = jnp.where(pl.program_id(2) < cfg.spb, n_lat_chunks, 0)
    m, l, acc = lax.fori_loop(0, n, body, (m, l, acc))
    o = acc / l
    for h in range(group):
        o_ref[:, h * hd:(h + 1) * hd] = o[:, h * tq:(h + 1) * tq].T.astype(o_ref.dtype)


def attention(cfg, qkv):
    hd = cfg.head_dim
    group = cfg.n_heads // cfg.n_kv
    gw = group * hd
    kcol = cfg.n_heads
    vcol = cfg.n_heads + cfg.n_kv
    assert cfg.cpb == 1

    def qrow(b, qb):
        return jnp.where(qb < cfg.spb, b * cfg.spb + qb, cfg.nbl + b)

    return pl.pallas_call(
        functools.partial(_attn_kernel, cfg=cfg, n_lat_chunks=cfg.seq // cfg.kv_chunk),
        grid=(cfg.batch, cfg.n_kv, cfg.spb + 1),
        in_specs=[pl.BlockSpec((cfg.tb, gw), lambda b, k, qb: (qrow(b, qb), k)),
                  pl.BlockSpec((cfg.ctx, hd), lambda b, k, qb: (cfg.nlat // cfg.ctx + b, kcol + k)),
                  pl.BlockSpec((cfg.ctx, hd), lambda b, k, qb: (cfg.nlat // cfg.ctx + b, vcol + k)),
                  pl.BlockSpec((cfg.seq, hd), lambda b, k, qb: (b, kcol + k)),
                  pl.BlockSpec((cfg.seq, hd), lambda b, k, qb: (b, vcol + k))],
        out_specs=pl.BlockSpec((cfg.tb, gw), lambda b, k, qb: (qrow(b, qb), k)),
        out_shape=jax.ShapeDtypeStruct((cfg.nt, cfg.n_heads * hd), BF16),
        scratch_shapes=[pltpu.VMEM((hd, cfg.ctx), BF16),
                        pltpu.VMEM((cfg.seq // cfg.kv_chunk, hd, cfg.kv_chunk), BF16)],
        compiler_params=_cparams(("arbitrary", "arbitrary", "arbitrary")),
        name="attention",
    )(qkv, qkv, qkv, qkv, qkv)


def _proj_res_kernel(a_ref, w_ref, b_ref, x_ref, gate_ref, o_ref):
    y = _dot(a_ref[...].astype(BF16), w_ref[...]) + b_ref[...]
    o_ref[...] = x_ref[...] + gate_ref[0] * y


def project_residual(cfg, a, w, bias, x, mod, gate_chunk, n_blocks):
    return pl.pallas_call(
        _proj_res_kernel,
        grid=(n_blocks,),
        in_specs=[_row_spec(cfg, a.shape[1]), _full_spec(w.shape), _full_spec((1, cfg.d)),
                  _row_spec(cfg, cfg.d), _mod_spec(cfg, gate_chunk)],
        out_specs=_row_spec(cfg, cfg.d),
        out_shape=jax.ShapeDtypeStruct((n_blocks * cfg.tb, cfg.d), F32),
        compiler_params=_cparams(("arbitrary",)),
        name="project_residual",
    )(a, w, bias, x, mod)


def _glu_kernel(x_ref, g_ref, sh_ref, sc_ref, w_ref, b_ref, o_ref, *, d):
    h = _modnorm(x_ref[...], g_ref[...], sh_ref[0], sc_ref[0]).astype(BF16)
    y = _dot(h, w_ref[...]) + b_ref[...]
    o_ref[...] = y[:, :d] * jax.nn.sigmoid(y[:, d:])


def glu_project(cfg, x, norm_g, mod, w, bias):
    return pl.pallas_call(
        functools.partial(_glu_kernel, d=cfg.d),
        grid=(cfg.nb,),
        in_specs=[_row_spec(cfg, cfg.d), _full_spec((1, cfg.d)), _mod_spec(cfg, 0), _mod_spec(cfg, 1),
                  _full_spec(w.shape), _full_spec((1, w.shape[1]))],
        out_specs=_row_spec(cfg, cfg.d),
        out_shape=jax.ShapeDtypeStruct((cfg.nt, cfg.d), F32),
        compiler_params=_cparams(("arbitrary",)),
        name="glu_project",
    )(x, norm_g, mod, mod, w, bias)


def _seq_edges(cfg, j):
    lat = j < cfg.nbl
    first = jnp.where(lat, j % cfg.spb == 0, (j - cfg.nbl) % cfg.cpb == 0)
    last = jnp.where(lat, j % cfg.spb == cfg.spb - 1, (j - cfg.nbl) % cfg.cpb == cfg.cpb - 1)
    return first, last


def _halo_specs(cfg, width, halo):
    per = cfg.tb // halo
    last_blk = cfg.nt // halo - 1
    prev = pl.BlockSpec((halo, width), lambda j: (jnp.maximum(j * per - 1, 0), 0))
    nxt = pl.BlockSpec((halo, width), lambda j: (jnp.minimum((j + 1) * per, last_blk), 0))
    return prev, nxt


def _fill_ext(cfg, ext_ref, prev_ref, cur_ref, next_ref, halo):
    first, last = _seq_edges(cfg, pl.program_id(0))
    tb = cfg.tb
    ext_ref[0:halo, :] = jnp.where(first, 0.0, prev_ref[...])
    ext_ref[halo:halo + tb, :] = cur_ref[...]
    ext_ref[halo + tb:2 * halo + tb, :] = jnp.where(last, 0.0, next_ref[...])


def _conv_kernel(up_ref, uc_ref, un_ref, wdw_ref, bdw_ref, lng_ref, lnb_ref, w2_ref, b2_ref, x_ref, gate_ref,
                 o_ref, ext_ref, *, cfg, halo):
    tb = cfg.tb
    _fill_ext(cfg, ext_ref, up_ref, uc_ref, un_ref, halo)
    half = (cfg.conv_w - 1) // 2
    acc = jnp.zeros((tb, cfg.d), F32)
    for k in range(cfg.conv_w):
        off = halo - half + k
        acc = acc + wdw_ref[k:k + 1, :] * ext_ref[off:off + tb, :]
    u = acc + bdw_ref[...]
    mu = jnp.mean(u, axis=-1, keepdims=True)
    uc = u - mu
    u = uc * lax.rsqrt(jnp.mean(uc * uc, axis=-1, keepdims=True) + EPS) * lng_ref[...] + lnb_ref[...]
    u = _silu(u).astype(BF16)
    y = _dot(u, w2_ref[...]) + b2_ref[...]
    o_ref[...] = x_ref[...] + gate_ref[0] * y


def conv_module(cfg, u, w_dw, b_dw, ln_g, ln_b, w2, b2, x, mod):
    halo = 16
    prev, nxt = _halo_specs(cfg, cfg.d, halo)
    return pl.pallas_call(
        functools.partial(_conv_kernel, cfg=cfg, halo=halo),
        grid=(cfg.nb,),
        in_specs=[prev, _row_spec(cfg, cfg.d), nxt, _full_spec(w_dw.shape), _full_spec((1, cfg.d)),
                  _full_spec((1, cfg.d)), _full_spec((1, cfg.d)), _full_spec(w2.shape), _full_spec((1, cfg.d)),
                  _row_spec(cfg, cfg.d), _mod_spec(cfg, 2)],
        out_specs=_row_spec(cfg, cfg.d),
        out_shape=jax.ShapeDtypeStruct((cfg.nt, cfg.d), F32),
        scratch_shapes=[pltpu.VMEM((cfg.tb + 2 * halo, cfg.d), F32)],
        compiler_params=_cparams(("arbitrary",)),
        name="conv_module",
    )(u, u, u, w_dw, b_dw, ln_g, ln_b, w2, b2, x, mod)


def _inproj_kernel(x_ref, g_ref, sh_ref, sc_ref, w_ref, b_ref, o_ref):
    h = _modnorm(x_ref[...], g_ref[...], sh_ref[0], sc_ref[0]).astype(BF16)
    o_ref[...] = _dot(h, w_ref[...]) + b_ref[...]


def in_project(cfg, x, norm_g, mod, w, bias):
    n = w.shape[1]
    return pl.pallas_call(
        _inproj_kernel,
        grid=(cfg.nb,),
        in_specs=[_row_spec(cfg, cfg.d), _full_spec((1, cfg.d)), _mod_spec(cfg, 0), _mod_spec(cfg, 1),
                  _full_spec(w.shape), _full_spec((1, n))],
        out_specs=_row_spec(cfg, n),
        out_shape=jax.ShapeDtypeStruct((cfg.nt, n), F32),
        compiler_params=_cparams(("arbitrary",)),
        name="in_project",
    )(x, norm_g, mod, mod, w, bias)


def _short_conv_kernel(pp_ref, pc_ref, pn_ref, w_ref, b_ref, o_ref, ext_ref, *, cfg, halo, width):
    tb = cfg.tb
    _fill_ext(cfg, ext_ref, pp_ref, pc_ref, pn_ref, halo)
    half = (width - 1) // 2
    acc = b_ref[...]
    for k in range(width):
        off = halo - half + k
        acc = acc + w_ref[k:k + 1, :] * ext_ref[off:off + tb, :]
    o_ref[...] = acc


def short_conv(cfg, p, w, bias):
    halo = 8
    n = p.shape[1]
    bn = cfg.d
    per = cfg.tb // halo
    last_blk = cfg.nt // halo - 1
    return pl.pallas_call(
        functools.partial(_short_conv_kernel, cfg=cfg, halo=halo, width=w.shape[0]),
        grid=(cfg.nb, n // bn),
        in_specs=[pl.BlockSpec((halo, bn), lambda j, c: (jnp.maximum(j * per - 1, 0), c)),
                  pl.BlockSpec((cfg.tb, bn), lambda j, c: (j, c)),
                  pl.BlockSpec((halo, bn), lambda j, c: (jnp.minimum((j + 1) * per, last_blk), c)),
                  pl.BlockSpec((w.shape[0], bn), lambda j, c: (0, c)),
                  pl.BlockSpec((1, bn), lambda j, c: (0, c))],
        out_specs=pl.BlockSpec((cfg.tb, bn), lambda j, c: (j, c)),
        out_shape=jax.ShapeDtypeStruct((cfg.nt, n), F32),
        scratch_shapes=[pltpu.VMEM((cfg.tb + 2 * halo, bn), F32)],
        compiler_params=_cparams(("arbitrary", "arbitrary")),
        name="short_conv",
    )(p, p, p, w, bias)


def _filter_mlp_kernel(feat_ref, w1_ref, b1_ref, f1_ref, w2_ref, b2_ref, f2_ref, w3_ref, delta_ref,
                       h_ref, s_ref, *, length, rows):
    i = pl.program_id(0)
    z = jnp.sin(f1_ref[...] * (_dot3(feat_ref[...], w1_ref[...]) + b1_ref[...]))
    z = jnp.sin(f2_ref[...] * (_dot3(z, w2_ref[...]) + b2_ref[...]))
    h = _dot3(z, w3_ref[...])
    pos = (i * rows + lax.broadcasted_iota(jnp.int32, h.shape, 0)).astype(F32)
    t = pos / (length - 1)
    h = h * jnp.exp(-t * jnp.abs(delta_ref[...]))
    col = lax.broadcasted_iota(jnp.int32, h.shape, 1)
    h = jnp.where((pos == 0.0) & (col >= h.shape[1] // 2), 0.0, h)
    h_ref[...] = h
    part = jnp.sum(jnp.abs(h), axis=0, keepdims=True)

    @pl.when(i == 0)
    def _():
        s_ref[...] = part

    @pl.when(i > 0)
    def _():
        s_ref[...] = s_ref[...] + part


def filter_taps(length, feats, w1, b1, f1, w2, b2, f2, w3, deltas4):
    rows = min(length, 256)
    nh = w1.shape[1]
    n = w3.shape[1]
    return pl.pallas_call(
        functools.partial(_filter_mlp_kernel, length=length, rows=rows),
        grid=(length // rows,),
        in_specs=[pl.BlockSpec((rows, feats.shape[1]), lambda i: (i, 0)), _full_spec(w1.shape),
                  _full_spec((1, nh)), _full_spec((1, nh)), _full_spec(w2.shape), _full_spec((1, nh)),
                  _full_spec((1, nh)), _full_spec(w3.shape), _full_spec((1, n))],
        out_specs=[pl.BlockSpec((rows, n), lambda i: (i, 0)), _full_spec((1, n))],
        out_shape=[jax.ShapeDtypeStruct((length, n), F32), jax.ShapeDtypeStruct((1, n), F32)],
        compiler_params=_cparams(("arbitrary",)),
        name="filter_taps",
    )(feats, w1, b1, f1, w2, b2, f2, w3, deltas4)


def _fft_stage1_kernel(z_ref, m_ref, a_ref):
    a_ref[...] = _dot(m_ref[0], z_ref[...].astype(BF16)).astype(a_ref.dtype)


def fft_stage1(z_view, mats, n_pairs, n2, lane_stride, lane_off, width, out_dtype=BF16):
    rows_out, rows_in = mats.shape[1:]
    return pl.pallas_call(
        _fft_stage1_kernel,
        grid=(n_pairs, n2),
        in_specs=[pl.BlockSpec((rows_in, width), lambda p, s: (p, s * lane_stride + lane_off)),
                  pl.BlockSpec((1, rows_out, rows_in), lambda p, s: (s, 0, 0))],
        out_specs=pl.BlockSpec((rows_out, width), lambda p, s: (p, s)),
        out_shape=jax.ShapeDtypeStruct((n_pairs * rows_out, n2 * width), out_dtype),
        compiler_params=_cparams(("arbitrary", "arbitrary")),
        name="fft_stage1",
    )(z_view, mats)


def _fft_mid_kernel(a_ref, h_ref, bf_ref, bi_ref, o_ref):
    half = a_ref.shape[0] // 2
    x = _dot(bf_ref[...], a_ref[...])
    xr, xi = x[:half], x[half:]
    hr, hi = h_ref[:half, :], h_ref[half:, :]
    y = jnp.concatenate([xr * hr - xi * hi, xr * hi + xi * hr], axis=0).astype(BF16)
    o_ref[...] = _dot(bi_ref[...], y).astype(o_ref.dtype)


def fft_mid(a, h, h_col, bd_fwd, bd_inv, n_pairs, n_groups, width):
    rows = bd_fwd.shape[0]
    a2 = a.reshape(n_pairs * n_groups * rows, width)
    out = pl.pallas_call(
        _fft_mid_kernel,
        grid=(n_groups, n_pairs),
        in_specs=[pl.BlockSpec((rows, width), lambda q, p: (p * n_groups + q, 0)),
                  pl.BlockSpec((rows, width), lambda q, p: (q, h_col)),
                  _full_spec(bd_fwd.shape), _full_spec(bd_inv.shape)],
        out_specs=pl.BlockSpec((rows, width), lambda q, p: (p * n_groups + q, 0)),
        out_shape=jax.ShapeDtypeStruct(a2.shape, BF16),
        compiler_params=_cparams(("arbitrary", "arbitrary")),
        name="fft_mid",
    )(a2, h, bd_fwd, bd_inv)
    return out.reshape(a.shape)


def _fft_spec_kernel(af_ref, ab_ref, bf_ref, nrm_ref, h_ref):
    half = af_ref.shape[0] // 2
    xf = _dot(bf_ref[...], af_ref[...])
    xb = _dot(bf_ref[...], ab_ref[...])
    inv = 1.0 / nrm_ref[...]
    h_ref[:half, :] = (xf[:half] + xb[:half]) * inv
    h_ref[half:, :] = (xf[half:] - xb[half:]) * inv


def fft_filter_spectrum(a, bd_fwd, norm, n_groups, width, n_orders):
    rows = bd_fwd.shape[0]
    a2 = a.reshape(n_groups * rows, 2 * n_orders * width)
    return pl.pallas_call(
        _fft_spec_kernel,
        grid=(n_groups, n_orders),
        in_specs=[pl.BlockSpec((rows, width), lambda q, o: (q, o)),
                  pl.BlockSpec((rows, width), lambda q, o: (q, n_orders + o)),
                  _full_spec(bd_fwd.shape),
                  pl.BlockSpec((1, width), lambda q, o: (0, o))],
        out_specs=pl.BlockSpec((rows, width), lambda q, o: (q, o)),
        out_shape=jax.ShapeDtypeStruct((n_groups * rows, n_orders * width), F32),
        compiler_params=_cparams(("arbitrary", "arbitrary")),
        name="fft_filter_spectrum",
    )(a2, a2, bd_fwd, norm)


def _fft_inv_kernel(b_ref, m_ref, z_ref, gate_ref, bias_ref, o_ref):
    y = _dot(m_ref[0], b_ref[...])
    o_ref[...] = gate_ref[...] * (y + z_ref[...] * bias_ref[...])


def fft_inverse_gate(b, mats, z_view, z_stride, z_off, gate_view, g_stride, g_off, bias, n_pairs, n2, width,
                     out_rows):
    rows_out, rows_in = mats.shape[1:]
    return pl.pallas_call(
        _fft_inv_kernel,
        grid=(n_pairs, n2),
        in_specs=[pl.BlockSpec((rows_in, width), lambda p, s: (p, s)),
                  pl.BlockSpec((1, rows_out, rows_in), lambda p, s: (s, 0, 0)),
                  pl.BlockSpec((rows_out, width), lambda p, s: (p, s * z_stride + z_off)),
                  pl.BlockSpec((rows_out, width), lambda p, s: (p, s * g_stride + g_off)),
                  _full_spec((1, width))],
        out_specs=pl.BlockSpec((rows_out, width), lambda p, s: (p, s)),
        out_shape=jax.ShapeDtypeStruct((out_rows, n2 * width), F32),
        compiler_params=_cparams(("arbitrary", "arbitrary")),
        name="fft_inverse_gate",
    )(b, mats, z_view, gate_view, bias)


def _ctx_spec_kernel(hf_ref, hb_ref, f_ref, nrm_ref, h_ref):
    half = f_ref.shape[0] // 2
    xf = _dot3(f_ref[...], hf_ref[...])
    xb = _dot3(f_ref[...], hb_ref[...])
    inv = 1.0 / nrm_ref[...]
    h_ref[:half, :] = (xf[:half] + xb[:half]) * inv
    h_ref[half:, :] = (xf[half:] - xb[half:]) * inv


def ctx_filter_spectrum(taps, f_real, norm, width, n_orders):
    length = taps.shape[0]
    rows = f_real.shape[0]
    return pl.pallas_call(
        _ctx_spec_kernel,
        grid=(n_orders,),
        in_specs=[pl.BlockSpec((length, width), lambda o: (0, o)),
                  pl.BlockSpec((length, width), lambda o: (0, n_orders + o)),
                  _full_spec(f_real.shape),
                  pl.BlockSpec((1, width), lambda o: (0, o))],
        out_specs=pl.BlockSpec((rows, width), lambda o: (0, o)),
        out_shape=jax.ShapeDtypeStruct((rows, n_orders * width), F32),
        compiler_params=_cparams(("arbitrary",)),
        name="ctx_filter_spectrum",
    )(taps, taps, f_real, norm)


def _ctx_conv_kernel(v_ref, g1_ref, g2_ref, h_ref, ff_ref, fi_ref, bias_ref, zin_ref, o_ref, *, width):
    del zin_ref
    half = ff_ref.shape[0] // 2
    z = v_ref[...]
    for n, gate_ref in enumerate((g1_ref, g2_ref)):
        x = _dot(ff_ref[...], z.astype(BF16))
        xr, xi = x[:half], x[half:]
        hr = h_ref[:half, n * width:(n + 1) * width]
        hi = h_ref[half:, n * width:(n + 1) * width]
        y = jnp.concatenate([xr * hr - xi * hi, xr * hi + xi * hr], axis=0).astype(BF16)
        y = _dot(fi_ref[...], y)
        z = gate_ref[...] * (y + z * bias_ref[n:n + 1, :])
    o_ref[...] = z


def ctx_long_conv(cfg, u, h_ctx, f_fwd, f_inv, bias, z_out):
    d = cfg.d
    rows = 2 * cfg.ctx
    base = cfg.nlat // rows
    return pl.pallas_call(
        functools.partial(_ctx_conv_kernel, width=d),
        grid=(cfg.batch // 2,),
        in_specs=[pl.BlockSpec((rows, d), lambda p: (base + p, 0)),
                  pl.BlockSpec((rows, d), lambda p: (base + p, 1)),
                  pl.BlockSpec((rows, d), lambda p: (base + p, 2)),
                  _full_spec(h_ctx.shape), _full_spec(f_fwd.shape), _full_spec(f_inv.shape),
                  _full_spec(bias.shape), pl.BlockSpec(memory_space=pl.ANY)],
        out_specs=pl.BlockSpec((rows, d), lambda p: (base + p, 0)),
        out_shape=jax.ShapeDtypeStruct(z_out.shape, F32),
        input_output_aliases={7: 0},
        compiler_params=_cparams(("arbitrary",)),
        name="ctx_long_conv",
    )(u, u, u, h_ctx, f_fwd, f_inv, bias, z_out)


def _dft_tables(n1, n2, group):
    n = n1 * n2
    h1 = n1 // 2
    f1 = jnp.arange(n1, dtype=jnp.int32)[None, :, None]
    s1 = jnp.arange(h1, dtype=jnp.int32)[None, None, :]
    s2 = jnp.arange(n2, dtype=jnp.int32)[:, None, None]
    k = (f1 * (n2 * s1 + s2)) % n
    ang = (2.0 * math.pi / n) * k.astype(F32)
    er, ei = jnp.cos(ang), -jnp.sin(ang)
    fwd = jnp.concatenate([jnp.concatenate([er, -ei], axis=2),
                           jnp.concatenate([ei, er], axis=2)], axis=1)
    cr, ci = jnp.swapaxes(er, 1, 2) / n, -jnp.swapaxes(ei, 1, 2) / n
    inv = jnp.concatenate([jnp.concatenate([cr, -ci], axis=2),
                           jnp.concatenate([ci, cr], axis=2)], axis=1)
    q = jnp.arange(n1 // group)[:, None, None]
    ri = jnp.arange(2)[None, :, None]
    j = jnp.arange(group)[None, None, :]
    perm = (ri * n1 + q * group + j).reshape(-1)
    fwd = fwd[:, perm, :]
    inv = inv[:, :, perm]
    real_only = fwd[:, :, :h1]
    a = jnp.arange(n2, dtype=jnp.int32)
    ang2 = (2.0 * math.pi / n2) * ((a[:, None] * a[None, :]) % n2).astype(F32)
    f2r, f2i = jnp.cos(ang2), -jnp.sin(ang2)
    eye = jnp.eye(group, dtype=F32)
    kr, ki = jnp.kron(eye, f2r), jnp.kron(eye, f2i)
    bd_fwd = jnp.concatenate([jnp.concatenate([kr, -ki], axis=1), jnp.concatenate([ki, kr], axis=1)], axis=0)
    bd_inv = jnp.concatenate([jnp.concatenate([kr, ki], axis=1), jnp.concatenate([-ki, kr], axis=1)], axis=0)
    return (fwd.astype(BF16), inv.astype(BF16), real_only.astype(BF16), bd_fwd.astype(BF16),
            bd_inv.astype(BF16))


def _ctx_dft_tables(length):
    n = 2 * length
    f = jnp.arange(n, dtype=jnp.int32)[:, None]
    s = jnp.arange(length, dtype=jnp.int32)[None, :]
    ang = (2.0 * math.pi / n) * ((f * s) % n).astype(F32)
    fr, fi = jnp.cos(ang), -jnp.sin(ang)
    fwd = jnp.concatenate([jnp.concatenate([fr, -fi], axis=1), jnp.concatenate([fi, fr], axis=1)], axis=0)
    frt, fit = fr.T / n, fi.T / n
    inv = jnp.concatenate([jnp.concatenate([frt, fit], axis=1), jnp.concatenate([-fit, frt], axis=1)], axis=0)
    real_only = jnp.concatenate([fr, fi], axis=0)
    return fwd.astype(BF16), inv.astype(BF16), real_only


def _filter_features(length, bands):
    pos = jnp.arange(length, dtype=F32)[:, None]
    t = pos / (length - 1)
    w = 2.0 * math.pi * pos / length
    bnd = jnp.linspace(1e-4, bands - 1, bands, dtype=F32)
    feats = jnp.concatenate([t, jnp.cos(bnd * w), -jnp.sin(bnd * w)], axis=-1)
    return jnp.pad(feats, ((0, 0), (0, 128 - feats.shape[1])))


def hyena_long_convs(cfg, u, fp, long_bias):
    d = cfg.d
    n1, n2 = cfg.fft_n1, cfg.fft_n2
    group = 4
    n_groups = n1 // group
    n_orders = 2
    n_pairs = cfg.batch // 2
    nrow = cfg.nt // n2
    (w1, b1, fq1, w2, b2, fq2, w3) = fp
    nh = w1.shape[1]
    w1p = jnp.pad(w1, ((0, 128 - w1.shape[0]), (0, 0)))
    bands = (w1.shape[0] - 1) // 2
    deltas = jnp.linspace(math.log(1e-2) / 1.5, math.log(1e-2) / 0.3, d, dtype=F32)
    deltas4 = jnp.tile(deltas, 2 * n_orders)[None, :]
    mlp = (w1p, b1.reshape(1, nh), fq1.reshape(1, nh), w2, b2.reshape(1, nh), fq2.reshape(1, nh), w3, deltas4)

    fwd, inv, real_only, bd_fwd, bd_inv = _dft_tables(n1, n2, group)

    taps, sums = filter_taps(cfg.seq, _filter_features(cfg.seq, bands), *mlp)
    norm = sums[:, :n_orders * d] + sums[:, n_orders * d:]
    a_f = fft_stage1(taps.reshape(n1 // 2, n2 * 2 * n_orders * d), real_only, 1, n2, 1, 0, 2 * n_orders * d)
    h_lat = fft_filter_spectrum(a_f, bd_fwd, norm, n_groups, d, n_orders)

    cf_fwd, cf_inv, cf_real = _ctx_dft_tables(cfg.ctx)
    taps_c, sums_c = filter_taps(cfg.ctx, _filter_features(cfg.ctx, bands), *mlp)
    norm_c = sums_c[:, :n_orders * d] + sums_c[:, n_orders * d:]
    h_ctx = ctx_filter_spectrum(taps_c, cf_real, norm_c, d, n_orders)

    u_view = u.reshape(nrow, n2 * 3 * d)
    z_view, z_stride, z_off = u_view, 3, 0
    z = None
    for order in range(n_orders):
        a = fft_stage1(z_view, fwd, n_pairs, n2, z_stride, z_off, d)
        bmid = fft_mid(a, h_lat, order, bd_fwd, bd_inv, n_pairs, n_groups, d)
        z = fft_inverse_gate(bmid, inv, z_view, z_stride, z_off, u_view, 3, 1 + order,
                             long_bias[order:order + 1], n_pairs, n2, d, nrow)
        z_view, z_stride, z_off = z, 1, 0
    z2 = z.reshape(cfg.nt, d)
    return ctx_long_conv(cfg, u, h_ctx, cf_fwd, cf_inv, long_bias, z2)


META_E, META_W, META_RANK = 0, 2, 4


def _route_kernel(x_ref, g_ref, sh_ref, sc_ref, wr_ref, br_ref, h_ref, meta_ref, cnt_ref, run_ref, *, cfg):
    ne = cfg.n_groups * cfg.epg
    step = pl.program_id(0)
    h = _modnorm(x_ref[...], g_ref[...], sh_ref[0], sc_ref[0])
    h_ref[...] = h.astype(h_ref.dtype)
    logits = _dot3(h, wr_ref[...]) + br_ref[...]
    lane = lax.broadcasted_iota(jnp.int32, logits.shape, 1).astype(F32)
    neg = -jnp.inf
    big = 1e9

    def first_argmax(mask):
        v = jnp.where(mask, logits, neg)
        mx = jnp.max(v, axis=-1, keepdims=True)
        idx = jnp.min(jnp.where(mask & (logits == mx), lane, big), axis=-1, keepdims=True)
        return mx, idx

    gmask = (lane >= ne) & (lane < ne + cfg.n_groups)
    gmax, gidx = first_argmax(gmask)
    g_p = 1.0 / jnp.sum(jnp.where(gmask, jnp.exp(logits - gmax), 0.0), axis=-1, keepdims=True)
    e0 = (gidx - ne) * cfg.epg
    emask = (lane >= e0) & (lane < e0 + cfg.epg)
    m1, i1 = first_argmax(emask)
    m2, i2 = first_argmax(emask & (lane != i1))
    r = jnp.exp(m2 - m1)
    w1 = g_p / (1.0 + r)
    w2 = g_p * r / (1.0 + r)

    @pl.when(step == 0)
    def _():
        run_ref[...] = jnp.zeros_like(run_ref)

    hit1 = lane == i1
    hit2 = lane == i2
    onehot = jnp.where(hit1 | hit2, 1.0, 0.0)
    tb = onehot.shape[0]
    row = lax.broadcasted_iota(jnp.int32, (tb, tb), 0)
    col = lax.broadcasted_iota(jnp.int32, (tb, tb), 1)
    earlier = jnp.where(col < row, 1.0, 0.0).astype(BF16)
    before = _dot(earlier, onehot.astype(BF16)) + run_ref[...]
    rank1 = jnp.sum(jnp.where(hit1, before, 0.0), axis=-1, keepdims=True)
    rank2 = jnp.sum(jnp.where(hit2, before, 0.0), axis=-1, keepdims=True)
    run_ref[...] = run_ref[...] + jnp.sum(onehot, axis=0, keepdims=True)
    cnt_ref[...] = run_ref[...]

    meta = jnp.zeros_like(logits)
    for k, v in enumerate((i1, i2, w1, w2, rank1, rank2)):
        meta = jnp.where(lane == float(k), v, meta)
    meta_ref[...] = meta


def route(cfg, x, norm_g, mod, w_route, b_route, n_blocks):
    return pl.pallas_call(
        functools.partial(_route_kernel, cfg=cfg),
        grid=(n_blocks,),
        in_specs=[_row_spec(cfg, cfg.d), _full_spec((1, cfg.d)), _mod_spec(cfg, 3), _mod_spec(cfg, 4),
                  _full_spec(w_route.shape), _full_spec((1, 128))],
        out_specs=[_row_spec(cfg, cfg.d), _row_spec(cfg, 128), _full_spec((1, 128))],
        out_shape=[jax.ShapeDtypeStruct((n_blocks * cfg.tb, cfg.d), F32),
                   jax.ShapeDtypeStruct((n_blocks * cfg.tb, 128), F32),
                   jax.ShapeDtypeStruct((1, 128), F32)],
        scratch_shapes=[pltpu.VMEM((1, 128), F32)],
        compiler_params=_cparams(("arbitrary",)),
        name="route",
    )(x, norm_g, mod, mod, w_route, b_route)


def _row_copies(hbm, idx_ref, vmem, sem, n_rows, to_hbm):
    rows = vmem.shape[0]

    def copy(r):
        h = hbm.at[pl.ds(idx_ref[0, 0, r], 1), :]
        v = vmem.at[pl.ds(r % rows, 1), :]
        return pltpu.make_async_copy(v, h, sem) if to_hbm else pltpu.make_async_copy(h, v, sem)

    def start():
        for r in range(n_rows):
            copy(r).start(priority=r % 2)

    def wait():
        def body(r, carry):
            copy(r).wait()
            return carry
        lax.fori_loop(0, n_rows, body, 0, unroll=8)

    return start, wait


def _dispatch_kernel(pos_ref, h_ref, hs_hbm, sem, *, tb):
    start, wait = _row_copies(hs_hbm, pos_ref, h_ref, sem.at[0], 2 * tb, to_hbm=True)
    start()
    wait()


def dispatch_rows(cfg, h, pos, n_blocks, nblk):
    tb = cfg.tb
    return pl.pallas_call(
        functools.partial(_dispatch_kernel, tb=tb),
        grid=(n_blocks,),
        in_specs=[pl.BlockSpec((1, 1, 2 * tb), lambda j: (j, 0, 0), memory_space=pltpu.SMEM),
                  _row_spec(cfg, cfg.d)],
        out_specs=pl.BlockSpec(memory_space=pl.ANY),
        out_shape=jax.ShapeDtypeStruct((nblk * tb, cfg.d), F32),
        scratch_shapes=[pltpu.SemaphoreType.DMA((1,))],
        compiler_params=_cparams(("arbitrary",)),
        name="dispatch_rows",
    )(pos, h)


def _expert_kernel(be_ref, valid_ref, h_ref, wg_ref, wu_ref, wd_ref, y_ref):
    del be_ref
    row = lax.broadcasted_iota(jnp.int32, h_ref.shape, 0)
    hb = jnp.where(row < valid_ref[pl.program_id(0)], h_ref[...], 0.0).astype(BF16)
    a = _dot(hb, wg_ref[0].astype(BF16))
    b = _dot(hb, wu_ref[0].astype(BF16))
    hid = (_silu(a) * b).astype(BF16)
    y_ref[...] = _dot(hid, wd_ref[0].astype(BF16))


def expert_mlp(cfg, h_sorted, block_expert, block_valid, w_gate, w_up, w_down, nblk):
    tb = cfg.tb
    de = cfg.d_exp
    grid_spec = pltpu.PrefetchScalarGridSpec(
        num_scalar_prefetch=2,
        grid=(nblk,),
        in_specs=[pl.BlockSpec((tb, cfg.d), lambda i, be, bv: (i, 0)),
                  pl.BlockSpec((1, cfg.d, de), lambda i, be, bv: (be[i], 0, 0)),
                  pl.BlockSpec((1, cfg.d, de), lambda i, be, bv: (be[i], 0, 0)),
                  pl.BlockSpec((1, de, cfg.d), lambda i, be, bv: (be[i], 0, 0))],
        out_specs=pl.BlockSpec((tb, cfg.d), lambda i, be, bv: (i, 0)),
    )
    return pl.pallas_call(
        _expert_kernel,
        grid_spec=grid_spec,
        out_shape=jax.ShapeDtypeStruct((nblk * tb, cfg.d), F32),
        compiler_params=_cparams(("arbitrary",)),
        name="expert_mlp",
    )(block_expert, block_valid, h_sorted, w_gate, w_up, w_down)


def _combine_kernel(pos_ref, pos_next_ref, x_ref, meta_ref, gate_ref, y_hbm, o_ref, ybuf, sem, *, tb, nblk):
    i = pl.program_id(0)
    slot = i % 2
    start_cur, wait_cur = _row_copies(y_hbm, pos_ref, ybuf.at[slot], sem.at[slot], 2 * tb, to_hbm=False)
    start_next, _ = _row_copies(y_hbm, pos_next_ref, ybuf.at[1 - slot], sem.at[1 - slot], 2 * tb, to_hbm=False)

    @pl.when(i == 0)
    def _():
        start_cur()

    @pl.when(i + 1 < nblk)
    def _():
        start_next()

    wait_cur()
    meta = meta_ref[...]
    w1 = meta[:, META_W:META_W + 1]
    w2 = meta[:, META_W + 1:META_W + 2]
    y = w1 * ybuf[slot, 0:tb, :] + w2 * ybuf[slot, tb:2 * tb, :]
    o_ref[...] = x_ref[...] + gate_ref[0] * y


def combine_residual(cfg, x, y_sorted, pos, meta, mod, gate_chunk, n_blocks):
    tb = cfg.tb
    return pl.pallas_call(
        functools.partial(_combine_kernel, tb=tb, nblk=n_blocks),
        grid=(n_blocks,),
        in_specs=[pl.BlockSpec((1, 1, 2 * tb), lambda j: (j, 0, 0), memory_space=pltpu.SMEM),
                  pl.BlockSpec((1, 1, 2 * tb), lambda j: (jnp.minimum(j + 1, n_blocks - 1), 0, 0),
                               memory_space=pltpu.SMEM),
                  _row_spec(cfg, cfg.d), _row_spec(cfg, 128), _mod_spec(cfg, gate_chunk),
                  pl.BlockSpec(memory_space=pl.ANY)],
        out_specs=_row_spec(cfg, cfg.d),
        out_shape=jax.ShapeDtypeStruct((n_blocks * tb, cfg.d), F32),
        scratch_shapes=[pltpu.VMEM((2, 2 * tb, cfg.d), F32), pltpu.SemaphoreType.DMA((2,))],
        compiler_params=_cparams(("arbitrary",)),
        name="combine_residual",
    )(pos, pos, x, meta, mod, y_sorted)


def _final_norm_kernel(x_ref, g_ref, o_ref):
    x = x_ref[...]
    o_ref[...] = x * lax.rsqrt(jnp.mean(x * x, axis=-1, keepdims=True) + EPS) * g_ref[...]


def final_norm(cfg, x, g):
    return pl.pallas_call(
        _final_norm_kernel,
        grid=(cfg.nbl,),
        in_specs=[_row_spec(cfg, cfg.d), _full_spec((1, cfg.d))],
        out_specs=_row_spec(cfg, cfg.d),
        out_shape=jax.ShapeDtypeStruct((cfg.nlat, cfg.d), F32),
        compiler_params=_cparams(("arbitrary",)),
        name="final_norm",
    )(x, g)


def _rope_tables(cfg):
    hd = cfg.head_dim
    pairs = hd // 4
    rows = cfg.seq // cfg.grid_w
    row = jnp.repeat(jnp.arange(rows), cfg.grid_w).astype(F32)
    col = jnp.tile(jnp.arange(cfg.grid_w), rows).astype(F32)
    inv_freq = 10000.0 ** (-jnp.arange(pairs, dtype=F32) / pairs)
    ar, ac = row[:, None] * inv_freq, col[:, None] * inv_freq
    cos = jnp.concatenate([jnp.cos(ar), jnp.cos(ar), jnp.cos(ac), jnp.cos(ac)], axis=-1)
    sin = jnp.concatenate([-jnp.sin(ar), jnp.sin(ar), -jnp.sin(ac), jnp.sin(ac)], axis=-1)
    cos = jnp.concatenate([cos, jnp.ones((cfg.tb, hd), F32)], axis=0)
    sin = jnp.concatenate([sin, jnp.zeros((cfg.tb, hd), F32)], axis=0)
    return cos, sin


def _moe_layer(cfg, x, norm_g, mod, w_group, b_group, w_router, b_router, w_gate, w_up, w_down, n_blocks):
    ne = cfg.n_groups * cfg.epg
    w_route = jnp.pad(jnp.concatenate([w_router, w_group], axis=1), ((0, 0), (0, 128 - ne - cfg.n_groups)))
    b_route = jnp.pad(jnp.concatenate([b_router, b_group]), (0, 128 - ne - cfg.n_groups)).reshape(1, 128)
    h, meta, counts = route(cfg, x, norm_g, mod, w_route, b_route, n_blocks)

    tb = cfg.tb
    n = n_blocks * tb
    nblk = 2 * n_blocks + ne
    e_ids = meta[:, META_E:META_E + 2].astype(jnp.int32)
    ranks = meta[:, META_RANK:META_RANK + 2].astype(jnp.int32)
    cnt = counts[0, :ne].astype(jnp.int32)
    padded = ((cnt + tb - 1) // tb) * tb
    seg_end = jnp.cumsum(padded)
    seg_start = seg_end - padded
    onehot = e_ids[:, :, None] == jnp.arange(ne, dtype=jnp.int32)
    pos = jnp.sum(jnp.where(onehot, seg_start, 0), axis=-1) + ranks
    pos_blocks = pos.reshape(n_blocks, tb, 2).transpose(0, 2, 1).reshape(n_blocks, 1, 2 * tb)
    blk_row = jnp.arange(nblk, dtype=jnp.int32) * tb
    block_expert = jnp.minimum(jnp.sum(seg_end[None, :] <= blk_row[:, None], axis=-1), ne - 1).astype(jnp.int32)
    block_valid = jnp.clip(seg_start[block_expert] + cnt[block_expert] - blk_row, 0, tb).astype(jnp.int32)
    del n

    h_sorted = dispatch_rows(cfg, h, pos_blocks, n_blocks, nblk)
    y = expert_mlp(cfg, h_sorted, block_expert, block_valid, w_gate.reshape(ne, cfg.d, cfg.d_exp),
                   w_up.reshape(ne, cfg.d, cfg.d_exp), w_down.reshape(ne, cfg.d_exp, cfg.d), nblk)
    return combine_residual(cfg, x, y, pos_blocks, meta, mod, 5, n_blocks)


def _forward(cfg, x, c, ctx, c_ctx, w_mod, b_mod, norm_mix_g, norm_ffn_g,
             attn_w_q, attn_w_kv, attn_q_gain, attn_k_gain, attn_w_o,
             conv_w_pw1, conv_b_pw1, conv_w_dw, conv_b_dw, conv_ln_g, conv_ln_b, conv_w_pw2, conv_b_pw2,
             hy_w_in, hy_b_in, hy_w_short, hy_b_short, hy_f_w1, hy_f_b1, hy_f_freq1, hy_f_w2, hy_f_b2,
             hy_f_freq2, hy_f_w3, hy_long_bias, hy_w_out, hy_b_out,
             moe_w_group, moe_b_group, moe_w_router, moe_b_router, moe_w_gate, moe_w_up, moe_w_down,
             final_norm_g):
    d = cfg.d
    depth = w_mod.shape[0]
    xs = jnp.concatenate([x.reshape(-1, d), ctx.reshape(-1, d)], axis=0)
    c_all = jnp.concatenate([c, c_ctx[None, :], jnp.zeros((MOD_ROWS - cfg.batch - 1, d), F32)], axis=0)
    mods = modulation(c_all, w_mod, b_mod)
    rope_cos, rope_sin = _rope_tables(cfg)
    zero_bias = jnp.zeros((1, d), F32)

    for i in range(depth):
        kind, slot = i % 3, i // 3
        last = i == depth - 1
        n_blocks = cfg.nbl if last else cfg.nb
        mod = mods[i].reshape(MOD_ROWS * 6, 1, d)
        g_mix = norm_mix_g[i].reshape(1, d)
        if kind == 0:
            w_qkv = jnp.concatenate([attn_w_q[slot], attn_w_kv[slot]], axis=1).astype(BF16)
            qkv = qkv_project(cfg, xs, g_mix, mod, w_qkv, rope_cos, rope_sin,
                              attn_q_gain[slot].reshape(1, -1), attn_k_gain[slot].reshape(1, -1))
            o = attention(cfg, qkv)
            xs = project_residual(cfg, o, attn_w_o[slot].astype(BF16), zero_bias, xs, mod, 2, n_blocks)
        elif kind == 1:
            u = glu_project(cfg, xs, g_mix, mod, conv_w_pw1[slot].astype(BF16), conv_b_pw1[slot].reshape(1, -1))
            xs = conv_module(cfg, u, conv_w_dw[slot], conv_b_dw[slot].reshape(1, d), conv_ln_g[slot].reshape(1, d),
                             conv_ln_b[slot].reshape(1, d), conv_w_pw2[slot].astype(BF16),
                             conv_b_pw2[slot].reshape(1, d), xs, mod)
        else:
            p = in_project(cfg, xs, g_mix, mod, hy_w_in[slot].astype(BF16), hy_b_in[slot].reshape(1, -1))
            u = short_conv(cfg, p, hy_w_short[slot], hy_b_short[slot].reshape(1, -1))
            fp = (hy_f_w1[slot], hy_f_b1[slot], hy_f_freq1[slot], hy_f_w2[slot], hy_f_b2[slot],
                  hy_f_freq2[slot], hy_f_w3[slot])
            z2 = hyena_long_convs(cfg, u, fp, hy_long_bias[slot])
            xs = project_residual(cfg, z2, hy_w_out[slot].astype(BF16), hy_b_out[slot].reshape(1, d), xs, mod, 2,
                                  n_blocks)
        xs = _moe_layer(cfg, xs, norm_ffn_g[i].reshape(1, d), mod, moe_w_group[i], moe_b_group[i],
                        moe_w_router[i], moe_b_router[i], moe_w_gate[i], moe_w_up[i], moe_w_down[i], n_blocks)

    out = final_norm(cfg, xs, final_norm_g.reshape(1, d))
    return out.reshape(cfg.batch, cfg.seq, d)


def kernel(x, c, ctx, c_ctx, w_mod, b_mod, norm_mix_g, norm_ffn_g, attn_w_q, attn_w_kv, attn_q_gain, attn_k_gain, attn_w_o, conv_w_pw1, conv_b_pw1, conv_w_dw, conv_b_dw, conv_ln_g, conv_ln_b, conv_w_pw2, conv_b_pw2, hy_w_in, hy_b_in, hy_w_short, hy_b_short, hy_f_w1, hy_f_b1, hy_f_freq1, hy_f_w2, hy_f_b2, hy_f_freq2, hy_f_w3, hy_long_bias, hy_w_out, hy_b_out, moe_w_group, moe_b_group, moe_w_router, moe_b_router, moe_w_gate, moe_w_up, moe_w_down, final_norm_g):
    return _forward(CFG, x, c, ctx, c_ctx, w_mod, b_mod, norm_mix_g, norm_ffn_g, attn_w_q, attn_w_kv, attn_q_gain, attn_k_gain, attn_w_o, conv_w_pw1, conv_b_pw1, conv_w_dw, conv_b_dw, conv_ln_g, conv_ln_b, conv_w_pw2, conv_b_pw2, hy_w_in, hy_b_in, hy_w_short, hy_b_short, hy_f_w1, hy_f_b1, hy_f_freq1, hy_f_w2, hy_f_b2, hy_f_freq2, hy_f_w3, hy_long_bias, hy_w_out, hy_b_out, moe_w_group, moe_b_group, moe_w_router, moe_b_router, moe_w_gate, moe_w_up, moe_w_down, final_norm_g)
```

```python
import functools
import math
from typing import NamedTuple

import jax
import jax.numpy as jnp
from jax import lax
from jax.experimental import pallas as pl
from jax.experimental.pallas import tpu as pltpu

F32 = jnp.float32
BF16 = jnp.bfloat16
EPS = 1e-6
V7X_VMEM_LIMIT_BYTES = 56 * 1024 * 1024


class Cfg(NamedTuple):
    batch: int
    seq: int
    ctx: int
    d: int
    grid_w: int
    n_heads: int
    n_kv: int
    head_dim: int
    conv_w: int
    n_groups: int
    epg: int
    d_exp: int
    tb: int
    fft_n1: int
    fft_n2: int
    kv_chunk: int

    @property
    def nlat(self):
        return self.batch * self.seq

    @property
    def nt(self):
        return self.batch * (self.seq + self.ctx)

    @property
    def nbl(self):
        return self.nlat // self.tb

    @property
    def nbc(self):
        return self.batch * self.ctx // self.tb

    @property
    def nb(self):
        return self.nbl + self.nbc

    @property
    def spb(self):
        return self.seq // self.tb

    @property
    def cpb(self):
        return self.ctx // self.tb


CFG = Cfg(batch=8, seq=4096, ctx=256, d=1024, grid_w=64, n_heads=8, n_kv=2, head_dim=128, conv_w=31,
          n_groups=4, epg=8, d_exp=256, tb=256, fft_n1=256, fft_n2=32, kv_chunk=1024)

MOD_ROWS = 16


def _cparams(sem):
    return pltpu.CompilerParams(dimension_semantics=sem, vmem_limit_bytes=V7X_VMEM_LIMIT_BYTES)


def _split_bf16(a):
    hi = a.astype(BF16)
    lo = (a - hi.astype(F32)).astype(BF16)
    return hi, lo


def _dot(a, b):
    return jnp.dot(a, b, preferred_element_type=F32)


def _dot3(a, b):
    ah, al = _split_bf16(a)
    bh, bl = _split_bf16(b)
    return _dot(ah, bh) + (_dot(ah, bl) + _dot(al, bh))


def _modnorm(x, g, sh, sc):
    ms = jnp.mean(x * x, axis=-1, keepdims=True)
    y = x * lax.rsqrt(ms + EPS) * g
    return y * (1.0 + sc) + sh


def _silu(x):
    return x * jax.nn.sigmoid(x)


def _mod_row(cfg, j):
    return jnp.where(j < cfg.nbl, j // cfg.spb, cfg.batch)


def _mod_spec(cfg, chunk):
    return pl.BlockSpec((1, 1, cfg.d), lambda j: (_mod_row(cfg, j) * 6 + chunk, 0, 0))


def _row_spec(cfg, width):
    return pl.BlockSpec((cfg.tb, width), lambda j: (j, 0))


def _full_spec(shape):
    n = len(shape)
    return pl.BlockSpec(shape, lambda *_: (0,) * n)


def _mod_kernel(c_ref, w_ref, b_ref, o_ref):
    o_ref[0] = _dot3(_silu(c_ref[...]), w_ref[0]) + b_ref[0]


def modulation(c_all, w_mod, b_mod):
    depth, d, n6 = w_mod.shape
    bn = n6 // 4
    return pl.pallas_call(
        _mod_kernel,
        grid=(depth, n6 // bn),
        in_specs=[_full_spec((MOD_ROWS, d)),
                  pl.BlockSpec((1, d, bn), lambda i, n: (i, 0, n)),
                  pl.BlockSpec((1, 1, bn), lambda i, n: (i, 0, n))],
        out_specs=pl.BlockSpec((1, MOD_ROWS, bn), lambda i, n: (i, 0, n)),
        out_shape=jax.ShapeDtypeStruct((depth, MOD_ROWS, n6), F32),
        compiler_params=_cparams(("arbitrary", "arbitrary")),
        name="modulation",
    )(c_all, w_mod, b_mod.reshape(depth, 1, n6))


def _qkv_kernel(x_ref, g_ref, sh_ref, sc_ref, w_ref, cs_ref, sn_ref, qg_ref, kg_ref, o_ref, *, cfg):
    hd = cfg.head_dim
    h = _modnorm(x_ref[...], g_ref[...], sh_ref[0], sc_ref[0]).astype(BF16)
    y = _dot(h, w_ref[...])
    cs = cs_ref[...]
    sn = sn_ref[...]
    lane = lax.broadcasted_iota(jnp.int32, cs.shape, 1)
    first_half = (lane % (hd // 2)) < (hd // 4)

    def norm_rope(v, gain, scale):
        ms = jnp.mean(v * v, axis=-1, keepdims=True)
        v = v * lax.rsqrt(ms + EPS) * gain
        partner = jnp.where(first_half, pltpu.roll(v, hd - hd // 4, 1), pltpu.roll(v, hd // 4, 1))
        return (v * cs + partner * sn) * scale

    nq = cfg.n_heads
    for hh in range(nq):
        sl = slice(hh * hd, (hh + 1) * hd)
        o_ref[:, sl] = norm_rope(y[:, sl], qg_ref[...], hd ** -0.5 * math.log2(math.e)).astype(o_ref.dtype)
    for hh in range(cfg.n_kv):
        sl = slice((nq + hh) * hd, (nq + hh + 1) * hd)
        o_ref[:, sl] = norm_rope(y[:, sl], kg_ref[...], 1.0).astype(o_ref.dtype)
    v0 = (nq + cfg.n_kv) * hd
    o_ref[:, v0:] = y[:, v0:].astype(o_ref.dtype)


def qkv_project(cfg, x, norm_g, mod, w_qkv, rope_cos, rope_sin, q_gain, k_gain):
    nqkv = w_qkv.shape[1]
    hd = cfg.head_dim

    def rope_idx(j):
        return (jnp.where(j < cfg.nbl, j % cfg.spb, cfg.spb), 0)

    return pl.pallas_call(
        functools.partial(_qkv_kernel, cfg=cfg),
        grid=(cfg.nb,),
        in_specs=[_row_spec(cfg, cfg.d), _full_spec((1, cfg.d)), _mod_spec(cfg, 0), _mod_spec(cfg, 1),
                  _full_spec((cfg.d, nqkv)),
                  pl.BlockSpec((cfg.tb, hd), rope_idx), pl.BlockSpec((cfg.tb, hd), rope_idx),
                  _full_spec((1, hd)), _full_spec((1, hd))],
        out_specs=_row_spec(cfg, nqkv),
        out_shape=jax.ShapeDtypeStruct((cfg.nt, nqkv), BF16),
        compiler_params=_cparams(("arbitrary",)),
        name="qkv_project",
    )(x, norm_g, mod, mod, w_qkv, rope_cos, rope_sin, q_gain, k_gain)


def _transpose_bf16(x):
    return x.astype(F32).T.astype(BF16)


def _attn_kernel(q_ref, kc_ref, vc_ref, kl_ref, vl_ref, o_ref, vtc_ref, vtl_ref, *, cfg, n_lat_chunks):
    hd = cfg.head_dim
    group = cfg.n_heads // cfg.n_kv
    tq = cfg.tb
    ch = cfg.kv_chunk

    @pl.when(pl.program_id(2) == 0)
    def _():
        vtc_ref[...] = _transpose_bf16(vc_ref[...])
        for c in range(n_lat_chunks):
            vtl_ref[c] = _transpose_bf16(vl_ref[c * ch:(c + 1) * ch, :])

    qt = jnp.concatenate([_transpose_bf16(q_ref[:, h * hd:(h + 1) * hd]) for h in range(group)], axis=1)

    s = _dot(kc_ref[...], qt)
    m = jnp.max(s, axis=0, keepdims=True)
    p = jnp.exp2(s - m)
    l = jnp.sum(p, axis=0, keepdims=True)
    acc = _dot(vtc_ref[...], p.astype(BF16))

    def body(c, carry):
        m, l, acc = carry
        start = pl.multiple_of(c * ch, ch)
        s = _dot(kl_ref[pl.ds(start, ch), :], qt)
        m_new = jnp.maximum(m, jnp.max(s, axis=0, keepdims=True))
        alpha = jnp.exp2(m - m_new)
        p = jnp.exp2(s - m_new)
        l = alpha * l + jnp.sum(p, axis=0, keepdims=True)
        acc = alpha * acc + _dot(vtl_ref[c], p.astype(BF16))
        return m_new, l, acc

    n = jnp.where(pl.program_id(2) < cfg.spb, n_lat_chunks, 0)
    m, l, acc = lax.fori_loop(0, n, body, (m, l, acc))
    o = acc / l
    for h in range(group):
        o_ref[:, h * hd:(h + 1) * hd] = o[:, h * tq:(h + 1) * tq].T.astype(o_ref.dtype)


def attention(cfg, qkv):
    hd = cfg.head_dim
    group = cfg.n_heads // cfg.n_kv
    gw = group * hd
    kcol = cfg.n_heads
    vcol = cfg.n_heads + cfg.n_kv
    assert cfg.cpb == 1

    def qrow(b, qb):
        return jnp.where(qb < cfg.spb, b * cfg.spb + qb, cfg.nbl + b)

    return pl.pallas_call(
        functools.partial(_attn_kernel, cfg=cfg, n_lat_chunks=cfg.seq // cfg.kv_chunk),
        grid=(cfg.batch, cfg.n_kv, cfg.spb + 1),
        in_specs=[pl.BlockSpec((cfg.tb, gw), lambda b, k, qb: (qrow(b, qb), k)),
                  pl.BlockSpec((cfg.ctx, hd), lambda b, k, qb: (cfg.nlat // cfg.ctx + b, kcol + k)),
                  pl.BlockSpec((cfg.ctx, hd), lambda b, k, qb: (cfg.nlat // cfg.ctx + b, vcol + k)),
                  pl.BlockSpec((cfg.seq, hd), lambda b, k, qb: (b, kcol + k)),
                  pl.BlockSpec((cfg.seq, hd), lambda b, k, qb: (b, vcol + k))],
        out_specs=pl.BlockSpec((cfg.tb, gw), lambda b, k, qb: (qrow(b, qb), k)),
        out_shape=jax.ShapeDtypeStruct((cfg.nt, cfg.n_heads * hd), BF16),
        scratch_shapes=[pltpu.VMEM((hd, cfg.ctx), BF16),
                        pltpu.VMEM((cfg.seq // cfg.kv_chunk, hd, cfg.kv_chunk), BF16)],
        compiler_params=_cparams(("arbitrary", "arbitrary", "arbitrary")),
        name="attention",
    )(qkv, qkv, qkv, qkv, qkv)


def _proj_res_kernel(a_ref, w_ref, b_ref, x_ref, gate_ref, o_ref):
    y = _dot(a_ref[...].astype(BF16), w_ref[...]) + b_ref[...]
    o_ref[...] = x_ref[...] + gate_ref[0] * y


def _proj_res_folded_kernel(a_ref, w_ref, b_ref, x_ref, gate_ref, o_ref, nat_ref, *, fold):
    _unfold_to_rows(a_ref, nat_ref, fold)
    y = _dot(_load_tiles(nat_ref).astype(BF16), w_ref[...]) + b_ref[...]
    o_ref[...] = x_ref[...] + gate_ref[0] * y


def project_residual(cfg, a, w, bias, x, mod, gate_chunk, n_blocks, fold=1):
    if fold == 1:
        body, a_spec, scratch = _proj_res_kernel, _row_spec(cfg, a.shape[1]), []
    else:
        body = functools.partial(_proj_res_folded_kernel, fold=fold)
        a_spec = pl.BlockSpec((cfg.tb // fold, a.shape[1]), lambda j: (j, 0))
        scratch = [_lane_tiles(cfg.tb, a.shape[1] // fold)]
    return pl.pallas_call(
        body,
        grid=(n_blocks,),
        in_specs=[a_spec, _full_spec(w.shape), _full_spec((1, cfg.d)),
                  _row_spec(cfg, cfg.d), _mod_spec(cfg, gate_chunk)],
        out_specs=_row_spec(cfg, cfg.d),
        out_shape=jax.ShapeDtypeStruct((n_blocks * cfg.tb, cfg.d), F32),
        scratch_shapes=scratch,
        compiler_params=_cparams(("arbitrary",)),
        name="project_residual",
    )(a, w, bias, x, mod)


def _glu_kernel(x_ref, g_ref, sh_ref, sc_ref, w_ref, b_ref, o_ref, *, d):
    h = _modnorm(x_ref[...], g_ref[...], sh_ref[0], sc_ref[0]).astype(BF16)
    y = _dot(h, w_ref[...]) + b_ref[...]
    o_ref[...] = y[:, :d] * jax.nn.sigmoid(y[:, d:])


def glu_project(cfg, x, norm_g, mod, w, bias):
    return pl.pallas_call(
        functools.partial(_glu_kernel, d=cfg.d),
        grid=(cfg.nb,),
        in_specs=[_row_spec(cfg, cfg.d), _full_spec((1, cfg.d)), _mod_spec(cfg, 0), _mod_spec(cfg, 1),
                  _full_spec(w.shape), _full_spec((1, w.shape[1]))],
        out_specs=_row_spec(cfg, cfg.d),
        out_shape=jax.ShapeDtypeStruct((cfg.nt, cfg.d), F32),
        compiler_params=_cparams(("arbitrary",)),
        name="glu_project",
    )(x, norm_g, mod, mod, w, bias)


def _seq_edges(cfg, j):
    lat = j < cfg.nbl
    first = jnp.where(lat, j % cfg.spb == 0, (j - cfg.nbl) % cfg.cpb == 0)
    last = jnp.where(lat, j % cfg.spb == cfg.spb - 1, (j - cfg.nbl) % cfg.cpb == cfg.cpb - 1)
    return first, last


def _halo_specs(cfg, width, halo):
    per = cfg.tb // halo
    last_blk = cfg.nt // halo - 1
    prev = pl.BlockSpec((halo, width), lambda j: (jnp.maximum(j * per - 1, 0), 0))
    nxt = pl.BlockSpec((halo, width), lambda j: (jnp.minimum((j + 1) * per, last_blk), 0))
    return prev, nxt


def _fill_ext(cfg, ext_ref, prev_ref, cur_ref, next_ref, halo):
    first, last = _seq_edges(cfg, pl.program_id(0))
    tb = cfg.tb
    ext_ref[0:halo, :] = jnp.where(first, 0.0, prev_ref[...])
    ext_ref[halo:halo + tb, :] = cur_ref[...]
    ext_ref[halo + tb:2 * halo + tb, :] = jnp.where(last, 0.0, next_ref[...])


def _conv_kernel(up_ref, uc_ref, un_ref, wdw_ref, bdw_ref, lng_ref, lnb_ref, w2_ref, b2_ref, x_ref, gate_ref,
                 o_ref, ext_ref, *, cfg, halo):
    tb = cfg.tb
    _fill_ext(cfg, ext_ref, up_ref, uc_ref, un_ref, halo)
    half = (cfg.conv_w - 1) // 2
    acc = jnp.zeros((tb, cfg.d), F32)
    for k in range(cfg.conv_w):
        off = halo - half + k
        acc = acc + wdw_ref[k:k + 1, :] * ext_ref[off:off + tb, :]
    u = acc + bdw_ref[...]
    mu = jnp.mean(u, axis=-1, keepdims=True)
    uc = u - mu
    u = uc * lax.rsqrt(jnp.mean(uc * uc, axis=-1, keepdims=True) + EPS) * lng_ref[...] + lnb_ref[...]
    u = _silu(u).astype(BF16)
    y = _dot(u, w2_ref[...]) + b2_ref[...]
    o_ref[...] = x_ref[...] + gate_ref[0] * y


def conv_module(cfg, u, w_dw, b_dw, ln_g, ln_b, w2, b2, x, mod):
    halo = 16
    prev, nxt = _halo_specs(cfg, cfg.d, halo)
    return pl.pallas_call(
        functools.partial(_conv_kernel, cfg=cfg, halo=halo),
        grid=(cfg.nb,),
        in_specs=[prev, _row_spec(cfg, cfg.d), nxt, _full_spec(w_dw.shape), _full_spec((1, cfg.d)),
                  _full_spec((1, cfg.d)), _full_spec((1, cfg.d)), _full_spec(w2.shape), _full_spec((1, cfg.d)),
                  _row_spec(cfg, cfg.d), _mod_spec(cfg, 2)],
        out_specs=_row_spec(cfg, cfg.d),
        out_shape=jax.ShapeDtypeStruct((cfg.nt, cfg.d), F32),
        scratch_shapes=[pltpu.VMEM((cfg.tb + 2 * halo, cfg.d), F32)],
        compiler_params=_cparams(("arbitrary",)),
        name="conv_module",
    )(u, u, u, w_dw, b_dw, ln_g, ln_b, w2, b2, x, mod)


def _inproj_kernel(x_ref, g_ref, sh_ref, sc_ref, w_ref, b_ref, o_ref):
    h = _modnorm(x_ref[...], g_ref[...], sh_ref[0], sc_ref[0]).astype(BF16)
    o_ref[...] = _dot(h, w_ref[...]) + b_ref[...]


def in_project(cfg, x, norm_g, mod, w, bias):
    n = w.shape[1]
    return pl.pallas_call(
        _inproj_kernel,
        grid=(cfg.nb,),
        in_specs=[_row_spec(cfg, cfg.d), _full_spec((1, cfg.d)), _mod_spec(cfg, 0), _mod_spec(cfg, 1),
                  _full_spec(w.shape), _full_spec((1, n))],
        out_specs=_row_spec(cfg, n),
        out_shape=jax.ShapeDtypeStruct((cfg.nt, n), F32),
        compiler_params=_cparams(("arbitrary",)),
        name="in_project",
    )(x, norm_g, mod, mod, w, bias)


LANES = 128


def _lane_tiles(rows, width):
    return pltpu.VMEM((width // LANES, rows, LANES), F32)


def _store_tiles(tiles_ref, value):
    for c in range(tiles_ref.shape[0]):
        tiles_ref[c] = value[:, c * LANES:(c + 1) * LANES]


def _load_tiles(tiles_ref, c0=0, c1=None):
    c1 = tiles_ref.shape[0] if c1 is None else c1
    return jnp.concatenate([tiles_ref[c] for c in range(c0, c1)], axis=1)


def _fold_rows(tiles_ref, o_ref, fold):
    nt, rows, _ = tiles_ref.shape
    w = nt * LANES
    for s in range(fold):
        for c in range(nt):
            o_ref[:, s * w + c * LANES:s * w + (c + 1) * LANES] = tiles_ref[c, pl.ds(s, rows // fold, stride=fold), :]


def _unfold_to_rows(f_ref, tiles_ref, fold):
    nt, rows, _ = tiles_ref.shape
    w = nt * LANES
    for s in range(fold):
        for c in range(nt):
            tiles_ref[c, pl.ds(s, rows // fold, stride=fold), :] = f_ref[:, s * w + c * LANES:s * w + (c + 1) * LANES]


def _short_conv_kernel(pp_ref, pc_ref, pn_ref, w_ref, b_ref, o_ref, ext_ref, nat_ref, *, cfg, halo, width):
    tb = cfg.tb
    _fill_ext(cfg, ext_ref, pp_ref, pc_ref, pn_ref, halo)
    half = (width - 1) // 2
    acc = b_ref[...]
    for k in range(width):
        off = halo - half + k
        acc = acc + w_ref[k:k + 1, :] * ext_ref[off:off + tb, :]
    _store_tiles(nat_ref, acc)
    _fold_rows(nat_ref, o_ref, cfg.fft_n2)


def short_conv(cfg, p, w, bias):
    halo = 8
    n = p.shape[1]
    n2 = cfg.fft_n2
    per = cfg.tb // halo
    last_blk = cfg.nt // halo - 1
    return pl.pallas_call(
        functools.partial(_short_conv_kernel, cfg=cfg, halo=halo, width=w.shape[0]),
        grid=(cfg.nb,),
        in_specs=[pl.BlockSpec((halo, n), lambda j: (jnp.maximum(j * per - 1, 0), 0)),
                  pl.BlockSpec((cfg.tb, n), lambda j: (j, 0)),
                  pl.BlockSpec((halo, n), lambda j: (jnp.minimum((j + 1) * per, last_blk), 0)),
                  _full_spec(w.shape), _full_spec((1, n))],
        out_specs=pl.BlockSpec((cfg.tb // n2, n2 * n), lambda j: (j, 0)),
        out_shape=jax.ShapeDtypeStruct((cfg.nt // n2, n2 * n), F32),
        scratch_shapes=[pltpu.VMEM((cfg.tb + 2 * halo, n), F32), _lane_tiles(cfg.tb, n)],
        compiler_params=_cparams(("arbitrary",)),
        name="short_conv",
    )(p, p, p, w, bias)


def _filter_mlp_kernel(feat_ref, w1_ref, b1_ref, f1_ref, w2_ref, b2_ref, f2_ref, w3_ref, delta_ref,
                       h_ref, s_ref, *, length, row_mul, step_mul):
    i = pl.program_id(0)
    z = jnp.sin(f1_ref[...] * (_dot3(feat_ref[0], w1_ref[...]) + b1_ref[...]))
    z = jnp.sin(f2_ref[...] * (_dot3(z, w2_ref[...]) + b2_ref[...]))
    h = _dot3(z, w3_ref[...])
    pos = (i * step_mul + row_mul * lax.broadcasted_iota(jnp.int32, h.shape, 0)).astype(F32)
    t = pos / (length - 1)
    h = h * jnp.exp(-t * jnp.abs(delta_ref[...]))
    col = lax.broadcasted_iota(jnp.int32, h.shape, 1)
    h = jnp.where((pos == 0.0) & (col >= h.shape[1] // 2), 0.0, h)
    h_ref[...] = h
    part = jnp.sum(jnp.abs(h), axis=0, keepdims=True)

    @pl.when(i == 0)
    def _():
        s_ref[...] = part

    @pl.when(i > 0)
    def _():
        s_ref[...] = s_ref[...] + part


def filter_taps(length, fold, feats, w1, b1, f1, w2, b2, f2, w3, deltas4):
    nh = w1.shape[1]
    n = w3.shape[1]
    if fold == 1:
        rows = min(length, 256)
        steps = length // rows
        feats3 = feats.reshape(steps, rows, feats.shape[1])
        row_mul, step_mul = 1, rows
        out_spec = pl.BlockSpec((rows, n), lambda i: (i, 0))
        out_shape = (length, n)
    else:
        rows = length // fold
        steps = fold
        feats3 = feats.reshape(rows, fold, feats.shape[1]).transpose(1, 0, 2)
        row_mul, step_mul = fold, 1
        out_spec = pl.BlockSpec((rows, n), lambda i: (0, i))
        out_shape = (rows, fold * n)
    return pl.pallas_call(
        functools.partial(_filter_mlp_kernel, length=length, row_mul=row_mul, step_mul=step_mul),
        grid=(steps,),
        in_specs=[pl.BlockSpec((1, rows, feats.shape[1]), lambda i: (i, 0, 0)), _full_spec(w1.shape),
                  _full_spec((1, nh)), _full_spec((1, nh)), _full_spec(w2.shape), _full_spec((1, nh)),
                  _full_spec((1, nh)), _full_spec(w3.shape), _full_spec((1, n))],
        out_specs=[out_spec, _full_spec((1, n))],
        out_shape=[jax.ShapeDtypeStruct(out_shape, F32), jax.ShapeDtypeStruct((1, n), F32)],
        compiler_params=_cparams(("arbitrary",)),
        name="filter_taps",
    )(feats3, w1, b1, f1, w2, b2, f2, w3, deltas4)


def _fft_stage1_kernel(z_ref, m_ref, a_ref):
    a_ref[...] = _dot(m_ref[0], z_ref[...].astype(BF16)).astype(a_ref.dtype)


def fft_stage1(z_view, mats, n_pairs, n2, lane_stride, lane_off, width, out_dtype=BF16):
    rows_out, rows_in = mats.shape[1:]
    return pl.pallas_call(
        _fft_stage1_kernel,
        grid=(n_pairs, n2),
        in_specs=[pl.BlockSpec((rows_in, width), lambda p, s: (p, s * lane_stride + lane_off)),
                  pl.BlockSpec((1, rows_out, rows_in), lambda p, s: (s, 0, 0))],
        out_specs=pl.BlockSpec((rows_out, width), lambda p, s: (p, s)),
        out_shape=jax.ShapeDtypeStruct((n_pairs * rows_out, n2 * width), out_dtype),
        compiler_params=_cparams(("arbitrary", "arbitrary")),
        name="fft_stage1",
    )(z_view, mats)


def _unfold_rows(a_ref, n2, width, lane_stride, lane_off):
    return jnp.concatenate(
        [a_ref[:, (s * lane_stride + lane_off) * width:(s * lane_stride + lane_off + 1) * width] for s in range(n2)],
        axis=0)


def _fft_mid_kernel(a_ref, h_ref, bf_ref, bi_ref, o_ref, *, n2, width):
    g = a_ref.shape[0]
    x = _dot(bf_ref[...], _unfold_rows(a_ref, n2, width, 1, 0))
    half = x.shape[0] // 2
    xr, xi = x[:half], x[half:]
    hr, hi = h_ref[:half, :], h_ref[half:, :]
    y = jnp.concatenate([xr * hr - xi * hi, xr * hi + xi * hr], axis=0).astype(BF16)
    b = _dot(bi_ref[...], y).astype(o_ref.dtype)
    for s in range(n2):
        o_ref[:, s * width:(s + 1) * width] = b[s * g:(s + 1) * g]


def fft_mid(a, h, h_col, bd_fwd, bd_inv, n_pairs, n_groups, n2, width):
    rows = bd_fwd.shape[0]
    g = rows // n2
    return pl.pallas_call(
        functools.partial(_fft_mid_kernel, n2=n2, width=width),
        grid=(n_groups, n_pairs),
        in_specs=[pl.BlockSpec((g, n2 * width), lambda q, p: (p * n_groups + q, 0)),
                  pl.BlockSpec((rows, width), lambda q, p: (q, h_col)),
                  _full_spec(bd_fwd.shape), _full_spec(bd_inv.shape)],
        out_specs=pl.BlockSpec((g, n2 * width), lambda q, p: (p * n_groups + q, 0)),
        out_shape=jax.ShapeDtypeStruct(a.shape, BF16),
        compiler_params=_cparams(("arbitrary", "arbitrary")),
        name="fft_mid",
    )(a, h, bd_fwd, bd_inv)


def _fft_spec_kernel(a_ref, bf_ref, nrm_ref, h_ref, *, n2, width, n_orders):
    o = pl.program_id(1)
    for order in range(n_orders):
        @pl.when(o == order)
        def _():
            xf = _dot(bf_ref[...], _unfold_rows(a_ref, n2, width, 2 * n_orders, order))
            xb = _dot(bf_ref[...], _unfold_rows(a_ref, n2, width, 2 * n_orders, n_orders + order))
            half = xf.shape[0] // 2
            inv = 1.0 / nrm_ref[...]
            h_ref[:half, :] = (xf[:half] + xb[:half]) * inv
            h_ref[half:, :] = (xf[half:] - xb[half:]) * inv


def fft_filter_spectrum(a, bd_fwd, norm, n_groups, n2, width, n_orders):
    rows = bd_fwd.shape[0]
    g = rows // n2
    return pl.pallas_call(
        functools.partial(_fft_spec_kernel, n2=n2, width=width, n_orders=n_orders),
        grid=(n_groups, n_orders),
        in_specs=[pl.BlockSpec((g, n2 * 2 * n_orders * width), lambda q, o: (q, 0)),
                  _full_spec(bd_fwd.shape),
                  pl.BlockSpec((1, width), lambda q, o: (0, o))],
        out_specs=pl.BlockSpec((rows, width), lambda q, o: (q, o)),
        out_shape=jax.ShapeDtypeStruct((n_groups * rows, n_orders * width), F32),
        compiler_params=_cparams(("arbitrary", "arbitrary")),
        name="fft_filter_spectrum",
    )(a, bd_fwd, norm)


def _fft_inv_kernel(b_ref, m_ref, z_ref, gate_ref, bias_ref, o_ref):
    y = _dot(m_ref[0], b_ref[...])
    o_ref[...] = gate_ref[...] * (y + z_ref[...] * bias_ref[...])


def fft_inverse_gate(b, mats, z_view, z_stride, z_off, gate_view, g_stride, g_off, bias, n_pairs, n2, width,
                     out_rows):
    rows_out, rows_in = mats.shape[1:]
    return pl.pallas_call(
        _fft_inv_kernel,
        grid=(n_pairs, n2),
        in_specs=[pl.BlockSpec((rows_in, width), lambda p, s: (p, s)),
                  pl.BlockSpec((1, rows_out, rows_in), lambda p, s: (s, 0, 0)),
                  pl.BlockSpec((rows_out, width), lambda p, s: (p, s * z_stride + z_off)),
                  pl.BlockSpec((rows_out, width), lambda p, s: (p, s * g_stride + g_off)),
                  _full_spec((1, width))],
        out_specs=pl.BlockSpec((rows_out, width), lambda p, s: (p, s)),
        out_shape=jax.ShapeDtypeStruct((out_rows, n2 * width), F32),
        compiler_params=_cparams(("arbitrary", "arbitrary")),
        name="fft_inverse_gate",
    )(b, mats, z_view, gate_view, bias)


def _ctx_spec_kernel(hf_ref, hb_ref, f_ref, nrm_ref, h_ref):
    half = f_ref.shape[0] // 2
    xf = _dot3(f_ref[...], hf_ref[...])
    xb = _dot3(f_ref[...], hb_ref[...])
    inv = 1.0 / nrm_ref[...]
    h_ref[:half, :] = (xf[:half] + xb[:half]) * inv
    h_ref[half:, :] = (xf[half:] - xb[half:]) * inv


def ctx_filter_spectrum(taps, f_real, norm, width, n_orders):
    length = taps.shape[0]
    rows = f_real.shape[0]
    return pl.pallas_call(
        _ctx_spec_kernel,
        grid=(n_orders,),
        in_specs=[pl.BlockSpec((length, width), lambda o: (0, o)),
                  pl.BlockSpec((length, width), lambda o: (0, n_orders + o)),
                  _full_spec(f_real.shape),
                  pl.BlockSpec((1, width), lambda o: (0, o))],
        out_specs=pl.BlockSpec((rows, width), lambda o: (0, o)),
        out_shape=jax.ShapeDtypeStruct((rows, n_orders * width), F32),
        compiler_params=_cparams(("arbitrary",)),
        name="ctx_filter_spectrum",
    )(taps, taps, f_real, norm)


def _ctx_conv_kernel(u_ref, h_ref, ff_ref, fi_ref, bias_ref, zin_ref, o_ref, nat_ref, z_ref, *, width, fold):
    del zin_ref
    half = ff_ref.shape[0] // 2
    _unfold_to_rows(u_ref, nat_ref, fold)
    wt = width // LANES
    z = _load_tiles(nat_ref, 0, wt)
    for n in range(2):
        x = _dot(ff_ref[...], z.astype(BF16))
        xr, xi = x[:half], x[half:]
        hr = h_ref[:half, n * width:(n + 1) * width]
        hi = h_ref[half:, n * width:(n + 1) * width]
        y = jnp.concatenate([xr * hr - xi * hi, xr * hi + xi * hr], axis=0).astype(BF16)
        y = _dot(fi_ref[...], y)
        z = _load_tiles(nat_ref, (n + 1) * wt, (n + 2) * wt) * (y + z * bias_ref[n:n + 1, :])
    _store_tiles(z_ref, z)
    _fold_rows(z_ref, o_ref, fold)


def ctx_long_conv(cfg, u_f, h_ctx, f_fwd, f_inv, bias, z_out):
    d = cfg.d
    n2 = cfg.fft_n2
    rows = 2 * cfg.ctx
    g = rows // n2
    base = cfg.nlat // rows
    return pl.pallas_call(
        functools.partial(_ctx_conv_kernel, width=d, fold=n2),
        grid=(cfg.batch // 2,),
        in_specs=[pl.BlockSpec((g, n2 * 3 * d), lambda p: (base + p, 0)),
                  _full_spec(h_ctx.shape), _full_spec(f_fwd.shape), _full_spec(f_inv.shape),
                  _full_spec(bias.shape), pl.BlockSpec(memory_space=pl.ANY)],
        out_specs=pl.BlockSpec((g, n2 * d), lambda p: (base + p, 0)),
        out_shape=jax.ShapeDtypeStruct(z_out.shape, F32),
        scratch_shapes=[_lane_tiles(rows, 3 * d), _lane_tiles(rows, d)],
        input_output_aliases={5: 0},
        compiler_params=_cparams(("arbitrary",)),
        name="ctx_long_conv",
    )(u_f, h_ctx, f_fwd, f_inv, bias, z_out)


def _dft_tables(n1, n2, group):
    n = n1 * n2
    h1 = n1 // 2
    f1 = jnp.arange(n1, dtype=jnp.int32)[None, :, None]
    s1 = jnp.arange(h1, dtype=jnp.int32)[None, None, :]
    s2 = jnp.arange(n2, dtype=jnp.int32)[:, None, None]
    k = (f1 * (n2 * s1 + s2)) % n
    ang = (2.0 * math.pi / n) * k.astype(F32)
    er, ei = jnp.cos(ang), -jnp.sin(ang)
    fwd = jnp.concatenate([jnp.concatenate([er, -ei], axis=2),
                           jnp.concatenate([ei, er], axis=2)], axis=1)
    cr, ci = jnp.swapaxes(er, 1, 2) / n, -jnp.swapaxes(ei, 1, 2) / n
    inv = jnp.concatenate([jnp.concatenate([cr, -ci], axis=2),
                           jnp.concatenate([ci, cr], axis=2)], axis=1)
    q = jnp.arange(n1 // group)[:, None, None]
    ri = jnp.arange(2)[None, :, None]
    j = jnp.arange(group)[None, None, :]
    perm = (ri * n1 + q * group + j).reshape(-1)
    fwd = fwd[:, perm, :]
    inv = inv[:, :, perm]
    real_only = fwd[:, :, :h1]
    a = jnp.arange(n2, dtype=jnp.int32)
    ang2 = (2.0 * math.pi / n2) * ((a[:, None] * a[None, :]) % n2).astype(F32)
    f2r, f2i = jnp.cos(ang2), -jnp.sin(ang2)
    eye = jnp.eye(group, dtype=F32)
    t_fwd = jnp.stack([jnp.stack([f2r, -f2i]), jnp.stack([f2i, f2r])])
    t_inv = jnp.stack([jnp.stack([f2r, f2i]), jnp.stack([-f2i, f2r])])
    size = 2 * group * n2
    bd_fwd = jnp.einsum("abfs,jk->afjsbk", t_fwd, eye).reshape(size, size)
    bd_inv = jnp.einsum("absf,jk->sajbfk", t_inv, eye).reshape(size, size)
    return (fwd.astype(BF16), inv.astype(BF16), real_only.astype(BF16), bd_fwd.astype(BF16),
            bd_inv.astype(BF16))


def _ctx_dft_tables(length):
    n = 2 * length
    f = jnp.arange(n, dtype=jnp.int32)[:, None]
    s = jnp.arange(length, dtype=jnp.int32)[None, :]
    ang = (2.0 * math.pi / n) * ((f * s) % n).astype(F32)
    fr, fi = jnp.cos(ang), -jnp.sin(ang)
    fwd = jnp.concatenate([jnp.concatenate([fr, -fi], axis=1), jnp.concatenate([fi, fr], axis=1)], axis=0)
    frt, fit = fr.T / n, fi.T / n
    inv = jnp.concatenate([jnp.concatenate([frt, fit], axis=1), jnp.concatenate([-fit, frt], axis=1)], axis=0)
    real_only = jnp.concatenate([fr, fi], axis=0)
    return fwd.astype(BF16), inv.astype(BF16), real_only


def _filter_features(length, bands):
    pos = jnp.arange(length, dtype=F32)[:, None]
    t = pos / (length - 1)
    w = 2.0 * math.pi * pos / length
    bnd = jnp.linspace(1e-4, bands - 1, bands, dtype=F32)
    feats = jnp.concatenate([t, jnp.cos(bnd * w), -jnp.sin(bnd * w)], axis=-1)
    return jnp.pad(feats, ((0, 0), (0, 128 - feats.shape[1])))


def hyena_long_convs(cfg, u, fp, long_bias):
    d = cfg.d
    n1, n2 = cfg.fft_n1, cfg.fft_n2
    group = 8
    n_groups = n1 // group
    n_orders = 2
    n_pairs = cfg.batch // 2
    nrow = cfg.nt // n2
    (w1, b1, fq1, w2, b2, fq2, w3) = fp
    nh = w1.shape[1]
    w1p = jnp.pad(w1, ((0, 128 - w1.shape[0]), (0, 0)))
    bands = (w1.shape[0] - 1) // 2
    deltas = jnp.linspace(math.log(1e-2) / 1.5, math.log(1e-2) / 0.3, d, dtype=F32)
    deltas4 = jnp.tile(deltas, 2 * n_orders)[None, :]
    mlp = (w1p, b1.reshape(1, nh), fq1.reshape(1, nh), w2, b2.reshape(1, nh), fq2.reshape(1, nh), w3, deltas4)

    fwd, inv, real_only, bd_fwd, bd_inv = _dft_tables(n1, n2, group)

    taps, sums = filter_taps(cfg.seq, n2, _filter_features(cfg.seq, bands), *mlp)
    norm = sums[:, :n_orders * d] + sums[:, n_orders * d:]
    a_f = fft_stage1(taps, real_only, 1, n2, 1, 0, 2 * n_orders * d)
    h_lat = fft_filter_spectrum(a_f, bd_fwd, norm, n_groups, n2, d, n_orders)

    cf_fwd, cf_inv, cf_real = _ctx_dft_tables(cfg.ctx)
    taps_c, sums_c = filter_taps(cfg.ctx, 1, _filter_features(cfg.ctx, bands), *mlp)
    norm_c = sums_c[:, :n_orders * d] + sums_c[:, n_orders * d:]
    h_ctx = ctx_filter_spectrum(taps_c, cf_real, norm_c, d, n_orders)

    u_view = u
    z_view, z_stride, z_off = u_view, 3, 0
    z = None
    for order in range(n_orders):
        a = fft_stage1(z_view, fwd, n_pairs, n2, z_stride, z_off, d)
        bmid = fft_mid(a, h_lat, order, bd_fwd, bd_inv, n_pairs, n_groups, n2, d)
        z = fft_inverse_gate(bmid, inv, z_view, z_stride, z_off, u_view, 3, 1 + order,
                             long_bias[order:order + 1], n_pairs, n2, d, nrow)
        z_view, z_stride, z_off = z, 1, 0
    return ctx_long_conv(cfg, u, h_ctx, cf_fwd, cf_inv, long_bias, z)


META_E, META_W, META_RANK = 0, 2, 4


def _route_kernel(x_ref, g_ref, sh_ref, sc_ref, wr_ref, br_ref, h_ref, meta_ref, cnt_ref, run_ref, *, cfg):
    ne = cfg.n_groups * cfg.epg
    step = pl.program_id(0)
    h = _modnorm(x_ref[...], g_ref[...], sh_ref[0], sc_ref[0])
    h_ref[...] = h.astype(h_ref.dtype)
    logits = _dot3(h, wr_ref[...]) + br_ref[...]
    lane = lax.broadcasted_iota(jnp.int32, logits.shape, 1).astype(F32)
    neg = -jnp.inf
    big = 1e9

    def first_argmax(mask):
        v = jnp.where(mask, logits, neg)
        mx = jnp.max(v, axis=-1, keepdims=True)
        idx = jnp.min(jnp.where(mask & (logits == mx), lane, big), axis=-1, keepdims=True)
        return mx, idx

    gmask = (lane >= ne) & (lane < ne + cfg.n_groups)
    gmax, gidx = first_argmax(gmask)
    g_p = 1.0 / jnp.sum(jnp.where(gmask, jnp.exp(logits - gmax), 0.0), axis=-1, keepdims=True)
    e0 = (gidx - ne) * cfg.epg
    emask = (lane >= e0) & (lane < e0 + cfg.epg)
    m1, i1 = first_argmax(emask)
    m2, i2 = first_argmax(emask & (lane != i1))
    r = jnp.exp(m2 - m1)
    w1 = g_p / (1.0 + r)
    w2 = g_p * r / (1.0 + r)

    @pl.when(step == 0)
    def _():
        run_ref[...] = jnp.zeros_like(run_ref)

    hit1 = lane == i1
    hit2 = lane == i2
    onehot = jnp.where(hit1 | hit2, 1.0, 0.0)
    tb = onehot.shape[0]
    row = lax.broadcasted_iota(jnp.int32, (tb, tb), 0)
    col = lax.broadcasted_iota(jnp.int32, (tb, tb), 1)
    earlier = jnp.where(col < row, 1.0, 0.0).astype(BF16)
    before = _dot(earlier, onehot.astype(BF16)) + run_ref[...]
    rank1 = jnp.sum(jnp.where(hit1, before, 0.0), axis=-1, keepdims=True)
    rank2 = jnp.sum(jnp.where(hit2, before, 0.0), axis=-1, keepdims=True)
    run_ref[...] = run_ref[...] + jnp.sum(onehot, axis=0, keepdims=True)
    cnt_ref[...] = run_ref[...]

    meta = jnp.zeros_like(logits)
    for k, v in enumerate((i1, i2, w1, w2, rank1, rank2)):
        meta = jnp.where(lane == float(k), v, meta)
    meta_ref[...] = meta


def route(cfg, x, norm_g, mod, w_route, b_route, n_blocks):
    return pl.pallas_call(
        functools.partial(_route_kernel, cfg=cfg),
        grid=(n_blocks,),
        in_specs=[_row_spec(cfg, cfg.d), _full_spec((1, cfg.d)), _mod_spec(cfg, 3), _mod_spec(cfg, 4),
                  _full_spec(w_route.shape), _full_spec((1, 128))],
        out_specs=[_row_spec(cfg, cfg.d), _row_spec(cfg, 128), _full_spec((1, 128))],
        out_shape=[jax.ShapeDtypeStruct((n_blocks * cfg.tb, cfg.d), F32),
                   jax.ShapeDtypeStruct((n_blocks * cfg.tb, 128), F32),
                   jax.ShapeDtypeStruct((1, 128), F32)],
        scratch_shapes=[pltpu.VMEM((1, 128), F32)],
        compiler_params=_cparams(("arbitrary",)),
        name="route",
    )(x, norm_g, mod, mod, w_route, b_route)


def _row_copies(hbm, idx_ref, vmem, sem, n_rows, to_hbm):
    rows = vmem.shape[0]

    def copy(r):
        h = hbm.at[pl.ds(idx_ref[0, 0, r], 1), :]
        v = vmem.at[pl.ds(r % rows, 1), :]
        return pltpu.make_async_copy(v, h, sem) if to_hbm else pltpu.make_async_copy(h, v, sem)

    def start():
        for r in range(n_rows):
            copy(r).start(priority=r % 2)

    def wait():
        def body(r, carry):
            copy(r).wait()
            return carry
        lax.fori_loop(0, n_rows, body, 0, unroll=8)

    return start, wait


def _dispatch_kernel(pos_ref, h_ref, hs_hbm, sem, *, tb):
    start, wait = _row_copies(hs_hbm, pos_ref, h_ref, sem.at[0], 2 * tb, to_hbm=True)
    start()
    wait()


def dispatch_rows(cfg, h, pos, n_blocks, nblk):
    tb = cfg.tb
    return pl.pallas_call(
        functools.partial(_dispatch_kernel, tb=tb),
        grid=(n_blocks,),
        in_specs=[pl.BlockSpec((1, 1, 2 * tb), lambda j: (j, 0, 0), memory_space=pltpu.SMEM),
                  _row_spec(cfg, cfg.d)],
        out_specs=pl.BlockSpec(memory_space=pl.ANY),
        out_shape=jax.ShapeDtypeStruct((nblk * tb, cfg.d), F32),
        scratch_shapes=[pltpu.SemaphoreType.DMA((1,))],
        compiler_params=_cparams(("arbitrary",)),
        name="dispatch_rows",
    )(pos, h)


def _expert_kernel(be_ref, valid_ref, h_ref, wg_ref, wu_ref, wd_ref, y_ref):
    del be_ref
    row = lax.broadcasted_iota(jnp.int32, h_ref.shape, 0)
    hb = jnp.where(row < valid_ref[pl.program_id(0)], h_ref[...], 0.0).astype(BF16)
    a = _dot(hb, wg_ref[0].astype(BF16))
    b = _dot(hb, wu_ref[0].astype(BF16))
    hid = (_silu(a) * b).astype(BF16)
    y_ref[...] = _dot(hid, wd_ref[0].astype(BF16))


def expert_mlp(cfg, h_sorted, block_expert, block_valid, w_gate, w_up, w_down, nblk):
    tb = cfg.tb
    de = cfg.d_exp
    grid_spec = pltpu.PrefetchScalarGridSpec(
        num_scalar_prefetch=2,
        grid=(nblk,),
        in_specs=[pl.BlockSpec((tb, cfg.d), lambda i, be, bv: (i, 0)),
                  pl.BlockSpec((1, cfg.d, de), lambda i, be, bv: (be[i], 0, 0)),
                  pl.BlockSpec((1, cfg.d, de), lambda i, be, bv: (be[i], 0, 0)),
                  pl.BlockSpec((1, de, cfg.d), lambda i, be, bv: (be[i], 0, 0))],
        out_specs=pl.BlockSpec((tb, cfg.d), lambda i, be, bv: (i, 0)),
    )
    return pl.pallas_call(
        _expert_kernel,
        grid_spec=grid_spec,
        out_shape=jax.ShapeDtypeStruct((nblk * tb, cfg.d), F32),
        compiler_params=_cparams(("arbitrary",)),
        name="expert_mlp",
    )(block_expert, block_valid, h_sorted, w_gate, w_up, w_down)


def _combine_kernel(pos_ref, pos_next_ref, x_ref, meta_ref, gate_ref, y_hbm, o_ref, ybuf, sem, *, tb, nblk):
    i = pl.program_id(0)
    slot = i % 2
    start_cur, wait_cur = _row_copies(y_hbm, pos_ref, ybuf.at[slot], sem.at[slot], 2 * tb, to_hbm=False)
    start_next, _ = _row_copies(y_hbm, pos_next_ref, ybuf.at[1 - slot], sem.at[1 - slot], 2 * tb, to_hbm=False)

    @pl.when(i == 0)
    def _():
        start_cur()

    @pl.when(i + 1 < nblk)
    def _():
        start_next()

    wait_cur()
    meta = meta_ref[...]
    w1 = meta[:, META_W:META_W + 1]
    w2 = meta[:, META_W + 1:META_W + 2]
    y = w1 * ybuf[slot, 0:tb, :] + w2 * ybuf[slot, tb:2 * tb, :]
    o_ref[...] = x_ref[...] + gate_ref[0] * y


def combine_residual(cfg, x, y_sorted, pos, meta, mod, gate_chunk, n_blocks):
    tb = cfg.tb
    return pl.pallas_call(
        functools.partial(_combine_kernel, tb=tb, nblk=n_blocks),
        grid=(n_blocks,),
        in_specs=[pl.BlockSpec((1, 1, 2 * tb), lambda j: (j, 0, 0), memory_space=pltpu.SMEM),
                  pl.BlockSpec((1, 1, 2 * tb), lambda j: (jnp.minimum(j + 1, n_blocks - 1), 0, 0),
                               memory_space=pltpu.SMEM),
                  _row_spec(cfg, cfg.d), _row_spec(cfg, 128), _mod_spec(cfg, gate_chunk),
                  pl.BlockSpec(memory_space=pl.ANY)],
        out_specs=_row_spec(cfg, cfg.d),
        out_shape=jax.ShapeDtypeStruct((n_blocks * tb, cfg.d), F32),
        scratch_shapes=[pltpu.VMEM((2, 2 * tb, cfg.d), F32), pltpu.SemaphoreType.DMA((2,))],
        compiler_params=_cparams(("arbitrary",)),
        name="combine_residual",
    )(pos, pos, x, meta, mod, y_sorted)


def _final_norm_kernel(x_ref, g_ref, o_ref):
    x = x_ref[...]
    o_ref[...] = x * lax.rsqrt(jnp.mean(x * x, axis=-1, keepdims=True) + EPS) * g_ref[...]


def final_norm(cfg, x, g):
    return pl.pallas_call(
        _final_norm_kernel,
        grid=(cfg.nbl,),
        in_specs=[_row_spec(cfg, cfg.d), _full_spec((1, cfg.d))],
        out_specs=_row_spec(cfg, cfg.d),
        out_shape=jax.ShapeDtypeStruct((cfg.nlat, cfg.d), F32),
        compiler_params=_cparams(("arbitrary",)),
        name="final_norm",
    )(x, g)


def _rope_tables(cfg):
    hd = cfg.head_dim
    pairs = hd // 4
    rows = cfg.seq // cfg.grid_w
    row = jnp.repeat(jnp.arange(rows), cfg.grid_w).astype(F32)
    col = jnp.tile(jnp.arange(cfg.grid_w), rows).astype(F32)
    inv_freq = 10000.0 ** (-jnp.arange(pairs, dtype=F32) / pairs)
    ar, ac = row[:, None] * inv_freq, col[:, None] * inv_freq
    cos = jnp.concatenate([jnp.cos(ar), jnp.cos(ar), jnp.cos(ac), jnp.cos(ac)], axis=-1)
    sin = jnp.concatenate([-jnp.sin(ar), jnp.sin(ar), -jnp.sin(ac), jnp.sin(ac)], axis=-1)
    cos = jnp.concatenate([cos, jnp.ones((cfg.tb, hd), F32)], axis=0)
    sin = jnp.concatenate([sin, jnp.zeros((cfg.tb, hd), F32)], axis=0)
    return cos, sin


def _moe_layer(cfg, x, norm_g, mod, w_group, b_group, w_router, b_router, w_gate, w_up, w_down, n_blocks):
    ne = cfg.n_groups * cfg.epg
    w_route = jnp.pad(jnp.concatenate([w_router, w_group], axis=1), ((0, 0), (0, 128 - ne - cfg.n_groups)))
    b_route = jnp.pad(jnp.concatenate([b_router, b_group]), (0, 128 - ne - cfg.n_groups)).reshape(1, 128)
    h, meta, counts = route(cfg, x, norm_g, mod, w_route, b_route, n_blocks)

    tb = cfg.tb
    n = n_blocks * tb
    nblk = 2 * n_blocks + ne
    e_ids = meta[:, META_E:META_E + 2].astype(jnp.int32)
    ranks = meta[:, META_RANK:META_RANK + 2].astype(jnp.int32)
    cnt = counts[0, :ne].astype(jnp.int32)
    padded = ((cnt + tb - 1) // tb) * tb
    seg_end = jnp.cumsum(padded)
    seg_start = seg_end - padded
    onehot = e_ids[:, :, None] == jnp.arange(ne, dtype=jnp.int32)
    pos = jnp.sum(jnp.where(onehot, seg_start, 0), axis=-1) + ranks
    pos_blocks = pos.reshape(n_blocks, tb, 2).transpose(0, 2, 1).reshape(n_blocks, 1, 2 * tb)
    blk_row = jnp.arange(nblk, dtype=jnp.int32) * tb
    block_expert = jnp.minimum(jnp.sum(seg_end[None, :] <= blk_row[:, None], axis=-1), ne - 1).astype(jnp.int32)
    block_valid = jnp.clip(seg_start[block_expert] + cnt[block_expert] - blk_row, 0, tb).astype(jnp.int32)
    del n

    h_sorted = dispatch_rows(cfg, h, pos_blocks, n_blocks, nblk)
    y = expert_mlp(cfg, h_sorted, block_expert, block_valid, w_gate.reshape(ne, cfg.d, cfg.d_exp),
                   w_up.reshape(ne, cfg.d, cfg.d_exp), w_down.reshape(ne, cfg.d_exp, cfg.d), nblk)
    return combine_residual(cfg, x, y, pos_blocks, meta, mod, 5, n_blocks)


def _forward(cfg, x, c, ctx, c_ctx, w_mod, b_mod, norm_mix_g, norm_ffn_g,
             attn_w_q, attn_w_kv, attn_q_gain, attn_k_gain, attn_w_o,
             conv_w_pw1, conv_b_pw1, conv_w_dw, conv_b_dw, conv_ln_g, conv_ln_b, conv_w_pw2, conv_b_pw2,
             hy_w_in, hy_b_in, hy_w_short, hy_b_short, hy_f_w1, hy_f_b1, hy_f_freq1, hy_f_w2, hy_f_b2,
             hy_f_freq2, hy_f_w3, hy_long_bias, hy_w_out, hy_b_out,
             moe_w_group, moe_b_group, moe_w_router, moe_b_router, moe_w_gate, moe_w_up, moe_w_down,
             final_norm_g):
    d = cfg.d
    depth = w_mod.shape[0]
    xs = jnp.concatenate([x.reshape(-1, d), ctx.reshape(-1, d)], axis=0)
    c_all = jnp.concatenate([c, c_ctx[None, :], jnp.zeros((MOD_ROWS - cfg.batch - 1, d), F32)], axis=0)
    mods = modulation(c_all, w_mod, b_mod)
    rope_cos, rope_sin = _rope_tables(cfg)
    zero_bias = jnp.zeros((1, d), F32)

    for i in range(depth):
        kind, slot = i % 3, i // 3
        last = i == depth - 1
        n_blocks = cfg.nbl if last else cfg.nb
        mod = mods[i].reshape(MOD_ROWS * 6, 1, d)
        g_mix = norm_mix_g[i].reshape(1, d)
        if kind == 0:
            w_qkv = jnp.concatenate([attn_w_q[slot], attn_w_kv[slot]], axis=1).astype(BF16)
            qkv = qkv_project(cfg, xs, g_mix, mod, w_qkv, rope_cos, rope_sin,
                              attn_q_gain[slot].reshape(1, -1), attn_k_gain[slot].reshape(1, -1))
            o = attention(cfg, qkv)
            xs = project_residual(cfg, o, attn_w_o[slot].astype(BF16), zero_bias, xs, mod, 2, n_blocks)
        elif kind == 1:
            u = glu_project(cfg, xs, g_mix, mod, conv_w_pw1[slot].astype(BF16), conv_b_pw1[slot].reshape(1, -1))
            xs = conv_module(cfg, u, conv_w_dw[slot], conv_b_dw[slot].reshape(1, d), conv_ln_g[slot].reshape(1, d),
                             conv_ln_b[slot].reshape(1, d), conv_w_pw2[slot].astype(BF16),
                             conv_b_pw2[slot].reshape(1, d), xs, mod)
        else:
            p = in_project(cfg, xs, g_mix, mod, hy_w_in[slot].astype(BF16), hy_b_in[slot].reshape(1, -1))
            u = short_conv(cfg, p, hy_w_short[slot], hy_b_short[slot].reshape(1, -1))
            fp = (hy_f_w1[slot], hy_f_b1[slot], hy_f_freq1[slot], hy_f_w2[slot], hy_f_b2[slot],
                  hy_f_freq2[slot], hy_f_w3[slot])
            z2 = hyena_long_convs(cfg, u, fp, hy_long_bias[slot])
            xs = project_residual(cfg, z2, hy_w_out[slot].astype(BF16), hy_b_out[slot].reshape(1, d), xs, mod, 2,
                                  n_blocks, fold=cfg.fft_n2)
        xs = _moe_layer(cfg, xs, norm_ffn_g[i].reshape(1, d), mod, moe_w_group[i], moe_b_group[i],
                        moe_w_router[i], moe_b_router[i], moe_w_gate[i], moe_w_up[i], moe_w_down[i], n_blocks)

    out = final_norm(cfg, xs, final_norm_g.reshape(1, d))
    return out.reshape(cfg.batch, cfg.seq, d)


def kernel(x, c, ctx, c_ctx, w_mod, b_mod, norm_mix_g, norm_ffn_g, attn_w_q, attn_w_kv, attn_q_gain, attn_k_gain, attn_w_o, conv_w_pw1, conv_b_pw1, conv_w_dw, conv_b_dw, conv_ln_g, conv_ln_b, conv_w_pw2, conv_b_pw2, hy_w_in, hy_b_in, hy_w_short, hy_b_short, hy_f_w1, hy_f_b1, hy_f_freq1, hy_f_w2, hy_f_b2, hy_f_freq2, hy_f_w3, hy_long_bias, hy_w_out, hy_b_out, moe_w_group, moe_b_group, moe_w_router, moe_b_router, moe_w_gate, moe_w_up, moe_w_down, final_norm_g):
    return _forward(CFG, x, c, ctx, c_ctx, w_mod, b_mod, norm_mix_g, norm_ffn_g, attn_w_q, attn_w_kv, attn_q_gain, attn_k_gain, attn_w_o, conv_w_pw1, conv_b_pw1, conv_w_dw, conv_b_dw, conv_ln_g, conv_ln_b, conv_w_pw2, conv_b_pw2, hy_w_in, hy_b_in, hy_w_short, hy_b_short, hy_f_w1, hy_f_b1, hy_f_freq1, hy_f_w2, hy_f_b2, hy_f_freq2, hy_f_w3, hy_long_bias, hy_w_out, hy_b_out, moe_w_group, moe_b_group, moe_w_router, moe_b_router, moe_w_gate, moe_w_up, moe_w_down, final_norm_g)
```

```python
import functools
import math
from typing import NamedTuple

import jax
import jax.numpy as jnp
from jax import lax
from jax.experimental import pallas as pl
from jax.experimental.pallas import tpu as pltpu

F32 = jnp.float32
BF16 = jnp.bfloat16
EPS = 1e-6
V7X_VMEM_LIMIT_BYTES = 56 * 1024 * 1024


class Cfg(NamedTuple):
    batch: int
    seq: int
    ctx: int
    d: int
    grid_w: int
    n_heads: int
    n_kv: int
    head_dim: int
    conv_w: int
    n_groups: int
    epg: int
    d_exp: int
    tb: int
    fft_n1: int
    fft_n2: int
    kv_chunk: int

    @property
    def nlat(self):
        return self.batch * self.seq

    @property
    def nt(self):
        return self.batch * (self.seq + self.ctx)

    @property
    def nbl(self):
        return self.nlat // self.tb

    @property
    def nbc(self):
        return self.batch * self.ctx // self.tb

    @property
    def nb(self):
        return self.nbl + self.nbc

    @property
    def spb(self):
        return self.seq // self.tb

    @property
    def cpb(self):
        return self.ctx // self.tb


CFG = Cfg(batch=8, seq=4096, ctx=256, d=1024, grid_w=64, n_heads=8, n_kv=2, head_dim=128, conv_w=31,
          n_groups=4, epg=8, d_exp=256, tb=256, fft_n1=256, fft_n2=32, kv_chunk=1024)

MOD_ROWS = 16


def _cparams(sem):
    return pltpu.CompilerParams(dimension_semantics=sem, vmem_limit_bytes=V7X_VMEM_LIMIT_BYTES)


def _split_bf16(a):
    hi = a.astype(BF16)
    lo = (a - hi.astype(F32)).astype(BF16)
    return hi, lo


def _dot(a, b):
    return jnp.dot(a, b, preferred_element_type=F32)


def _dot3(a, b):
    ah, al = _split_bf16(a)
    bh, bl = _split_bf16(b)
    return _dot(ah, bh) + (_dot(ah, bl) + _dot(al, bh))


def _modnorm(x, g, sh, sc):
    ms = jnp.mean(x * x, axis=-1, keepdims=True)
    y = x * lax.rsqrt(ms + EPS) * g
    return y * (1.0 + sc) + sh


def _silu(x):
    return x * jax.nn.sigmoid(x)


def _mod_row(cfg, j):
    return jnp.where(j < cfg.nbl, j // cfg.spb, cfg.batch)


def _mod_spec(cfg, chunk):
    return pl.BlockSpec((1, 1, cfg.d), lambda j: (_mod_row(cfg, j) * 6 + chunk, 0, 0))


def _row_spec(cfg, width):
    return pl.BlockSpec((cfg.tb, width), lambda j: (j, 0))


def _full_spec(shape):
    n = len(shape)
    return pl.BlockSpec(shape, lambda *_: (0,) * n)


def _mod_kernel(c_ref, w_ref, b_ref, o_ref):
    o_ref[0] = _dot3(_silu(c_ref[...]), w_ref[0]) + b_ref[0]


def modulation(c_all, w_mod, b_mod):
    depth, d, n6 = w_mod.shape
    bn = n6 // 4
    return pl.pallas_call(
        _mod_kernel,
        grid=(depth, n6 // bn),
        in_specs=[_full_spec((MOD_ROWS, d)),
                  pl.BlockSpec((1, d, bn), lambda i, n: (i, 0, n)),
                  pl.BlockSpec((1, 1, bn), lambda i, n: (i, 0, n))],
        out_specs=pl.BlockSpec((1, MOD_ROWS, bn), lambda i, n: (i, 0, n)),
        out_shape=jax.ShapeDtypeStruct((depth, MOD_ROWS, n6), F32),
        compiler_params=_cparams(("arbitrary", "arbitrary")),
        name="modulation",
    )(c_all, w_mod, b_mod.reshape(depth, 1, n6))


def _qkv_kernel(x_ref, g_ref, sh_ref, sc_ref, w_ref, cs_ref, sn_ref, qg_ref, kg_ref, o_ref, *, cfg):
    hd = cfg.head_dim
    h = _modnorm(x_ref[...], g_ref[...], sh_ref[0], sc_ref[0]).astype(BF16)
    y = _dot(h, w_ref[...])
    cs = cs_ref[...]
    sn = sn_ref[...]
    lane = lax.broadcasted_iota(jnp.int32, cs.shape, 1)
    first_half = (lane % (hd // 2)) < (hd // 4)

    def norm_rope(v, gain, scale):
        ms = jnp.mean(v * v, axis=-1, keepdims=True)
        v = v * lax.rsqrt(ms + EPS) * gain
        partner = jnp.where(first_half, pltpu.roll(v, hd - hd // 4, 1), pltpu.roll(v, hd // 4, 1))
        return (v * cs + partner * sn) * scale

    nq = cfg.n_heads
    for hh in range(nq):
        sl = slice(hh * hd, (hh + 1) * hd)
        o_ref[:, sl] = norm_rope(y[:, sl], qg_ref[...], hd ** -0.5 * math.log2(math.e)).astype(o_ref.dtype)
    for hh in range(cfg.n_kv):
        sl = slice((nq + hh) * hd, (nq + hh + 1) * hd)
        o_ref[:, sl] = norm_rope(y[:, sl], kg_ref[...], 1.0).astype(o_ref.dtype)
    v0 = (nq + cfg.n_kv) * hd
    o_ref[:, v0:] = y[:, v0:].astype(o_ref.dtype)


def qkv_project(cfg, x, norm_g, mod, w_qkv, rope_cos, rope_sin, q_gain, k_gain):
    nqkv = w_qkv.shape[1]
    hd = cfg.head_dim

    def rope_idx(j):
        return (jnp.where(j < cfg.nbl, j % cfg.spb, cfg.spb), 0)

    return pl.pallas_call(
        functools.partial(_qkv_kernel, cfg=cfg),
        grid=(cfg.nb,),
        in_specs=[_row_spec(cfg, cfg.d), _full_spec((1, cfg.d)), _mod_spec(cfg, 0), _mod_spec(cfg, 1),
                  _full_spec((cfg.d, nqkv)),
                  pl.BlockSpec((cfg.tb, hd), rope_idx), pl.BlockSpec((cfg.tb, hd), rope_idx),
                  _full_spec((1, hd)), _full_spec((1, hd))],
        out_specs=_row_spec(cfg, nqkv),
        out_shape=jax.ShapeDtypeStruct((cfg.nt, nqkv), BF16),
        compiler_params=_cparams(("arbitrary",)),
        name="qkv_project",
    )(x, norm_g, mod, mod, w_qkv, rope_cos, rope_sin, q_gain, k_gain)


def _transpose_bf16(x):
    return x.astype(F32).T.astype(BF16)


def _attn_kernel(q_ref, kc_ref, vc_ref, kl_ref, vl_ref, o_ref, vtc_ref, vtl_ref, *, cfg, n_lat_chunks):
    hd = cfg.head_dim
    group = cfg.n_heads // cfg.n_kv
    tq = cfg.tb
    ch = cfg.kv_chunk

    @pl.when(pl.program_id(2) == 0)
    def _():
        vtc_ref[...] = _transpose_bf16(vc_ref[...])
        for c in range(n_lat_chunks):
            vtl_ref[c] = _transpose_bf16(vl_ref[c * ch:(c + 1) * ch, :])

    qt = jnp.concatenate([_transpose_bf16(q_ref[:, h * hd:(h + 1) * hd]) for h in range(group)], axis=1)

    s = _dot(kc_ref[...], qt)
    m = jnp.max(s, axis=0, keepdims=True)
    p = jnp.exp2(s - m)
    l = jnp.sum(p, axis=0, keepdims=True)
    acc = _dot(vtc_ref[...], p.astype(BF16))

    def body(c, carry):
        m, l, acc = carry
        start = pl.multiple_of(c * ch, ch)
        s = _dot(kl_ref[pl.ds(start, ch), :], qt)
        m_new = jnp.maximum(m, jnp.max(s, axis=0, keepdims=True))
        alpha = jnp.exp2(m - m_new)
        p = jnp.exp2(s - m_new)
        l = alpha * l + jnp.sum(p, axis=0, keepdims=True)
        acc = alpha * acc + _dot(vtl_ref[c], p.astype(BF16))
        return m_new, l, acc

    n = jnp.where(pl.program_id(2) < cfg.spb, n_lat_chunks, 0)
    m, l, acc = lax.fori_loop(0, n, body, (m, l, acc))
    o = acc / l
    for h in range(group):
        o_ref[:, h * hd:(h + 1) * hd] = o[:, h * tq:(h + 1) * tq].T.astype(o_ref.dtype)


def attention(cfg, qkv):
    hd = cfg.head_dim
    group = cfg.n_heads // cfg.n_kv
    gw = group * hd
    kcol = cfg.n_heads
    vcol = cfg.n_heads + cfg.n_kv
    assert cfg.cpb == 1

    def qrow(b, qb):
        return jnp.where(qb < cfg.spb, b * cfg.spb + qb, cfg.nbl + b)

    return pl.pallas_call(
        functools.partial(_attn_kernel, cfg=cfg, n_lat_chunks=cfg.seq // cfg.kv_chunk),
        grid=(cfg.batch, cfg.n_kv, cfg.spb + 1),
        in_specs=[pl.BlockSpec((cfg.tb, gw), lambda b, k, qb: (qrow(b, qb), k)),
                  pl.BlockSpec((cfg.ctx, hd), lambda b, k, qb: (cfg.nlat // cfg.ctx + b, kcol + k)),
                  pl.BlockSpec((cfg.ctx, hd), lambda b, k, qb: (cfg.nlat // cfg.ctx + b, vcol + k)),
                  pl.BlockSpec((cfg.seq, hd), lambda b, k, qb: (b, kcol + k)),
                  pl.BlockSpec((cfg.seq, hd), lambda b, k, qb: (b, vcol + k))],
        out_specs=pl.BlockSpec((cfg.tb, gw), lambda b, k, qb: (qrow(b, qb), k)),
        out_shape=jax.ShapeDtypeStruct((cfg.nt, cfg.n_heads * hd), BF16),
        scratch_shapes=[pltpu.VMEM((hd, cfg.ctx), BF16),
                        pltpu.VMEM((cfg.seq // cfg.kv_chunk, hd, cfg.kv_chunk), BF16)],
        compiler_params=_cparams(("arbitrary", "arbitrary", "arbitrary")),
        name="attention",
    )(qkv, qkv, qkv, qkv, qkv)


def _proj_res_kernel(a_ref, w_ref, b_ref, x_ref, gate_ref, o_ref):
    y = _dot(a_ref[...].astype(BF16), w_ref[...]) + b_ref[...]
    o_ref[...] = x_ref[...] + gate_ref[0] * y


def _proj_res_folded_kernel(a_ref, w_ref, b_ref, x_ref, gate_ref, o_ref, nat_ref, *, fold):
    _unfold_to_rows(a_ref, nat_ref, fold)
    y = _dot(_load_tiles(nat_ref).astype(BF16), w_ref[...]) + b_ref[...]
    o_ref[...] = x_ref[...] + gate_ref[0] * y


def project_residual(cfg, a, w, bias, x, mod, gate_chunk, n_blocks, fold=1):
    if fold == 1:
        body, a_spec, scratch = _proj_res_kernel, _row_spec(cfg, a.shape[1]), []
    else:
        body = functools.partial(_proj_res_folded_kernel, fold=fold)
        a_spec = pl.BlockSpec((cfg.tb // fold, a.shape[1]), lambda j: (j, 0))
        scratch = [_lane_tiles(cfg.tb, a.shape[1] // fold)]
    return pl.pallas_call(
        body,
        grid=(n_blocks,),
        in_specs=[a_spec, _full_spec(w.shape), _full_spec((1, cfg.d)),
                  _row_spec(cfg, cfg.d), _mod_spec(cfg, gate_chunk)],
        out_specs=_row_spec(cfg, cfg.d),
        out_shape=jax.ShapeDtypeStruct((n_blocks * cfg.tb, cfg.d), F32),
        scratch_shapes=scratch,
        compiler_params=_cparams(("arbitrary",)),
        name="project_residual",
    )(a, w, bias, x, mod)


def _glu_kernel(x_ref, g_ref, sh_ref, sc_ref, w_ref, b_ref, o_ref, *, d):
    h = _modnorm(x_ref[...], g_ref[...], sh_ref[0], sc_ref[0]).astype(BF16)
    y = _dot(h, w_ref[...]) + b_ref[...]
    o_ref[...] = (y[:, :d] * jax.nn.sigmoid(y[:, d:])).astype(o_ref.dtype)


def glu_project(cfg, x, norm_g, mod, w, bias):
    return pl.pallas_call(
        functools.partial(_glu_kernel, d=cfg.d),
        grid=(cfg.nb,),
        in_specs=[_row_spec(cfg, cfg.d), _full_spec((1, cfg.d)), _mod_spec(cfg, 0), _mod_spec(cfg, 1),
                  _full_spec(w.shape), _full_spec((1, w.shape[1]))],
        out_specs=_row_spec(cfg, cfg.d),
        out_shape=jax.ShapeDtypeStruct((cfg.nt, cfg.d), BF16),
        compiler_params=_cparams(("arbitrary",)),
        name="glu_project",
    )(x, norm_g, mod, mod, w, bias)


def _seq_edges(cfg, j):
    lat = j < cfg.nbl
    first = jnp.where(lat, j % cfg.spb == 0, (j - cfg.nbl) % cfg.cpb == 0)
    last = jnp.where(lat, j % cfg.spb == cfg.spb - 1, (j - cfg.nbl) % cfg.cpb == cfg.cpb - 1)
    return first, last


def _halo_specs(cfg, width, halo):
    per = cfg.tb // halo
    last_blk = cfg.nt // halo - 1
    prev = pl.BlockSpec((halo, width), lambda j: (jnp.maximum(j * per - 1, 0), 0))
    nxt = pl.BlockSpec((halo, width), lambda j: (jnp.minimum((j + 1) * per, last_blk), 0))
    return prev, nxt


def _fill_ext(cfg, ext_ref, prev_ref, cur_ref, next_ref, halo):
    first, last = _seq_edges(cfg, pl.program_id(0))
    tb = cfg.tb
    ext_ref[0:halo, :] = jnp.where(first, 0.0, prev_ref[...].astype(F32))
    ext_ref[halo:halo + tb, :] = cur_ref[...].astype(F32)
    ext_ref[halo + tb:2 * halo + tb, :] = jnp.where(last, 0.0, next_ref[...].astype(F32))


def _conv_kernel(up_ref, uc_ref, un_ref, wdw_ref, bdw_ref, lng_ref, lnb_ref, w2_ref, b2_ref, x_ref, gate_ref,
                 o_ref, ext_ref, *, cfg, halo):
    tb = cfg.tb
    _fill_ext(cfg, ext_ref, up_ref, uc_ref, un_ref, halo)
    half = (cfg.conv_w - 1) // 2
    sub = 8
    acc = jnp.zeros((tb, cfg.d), F32)
    for r in range(sub):
        part = None
        for k in range(cfg.conv_w):
            off = halo - half + k
            if off % sub != r:
                continue
            base = off - r
            term = wdw_ref[k:k + 1, :] * ext_ref[base:base + tb + sub, :]
            part = term if part is None else part + term
        if part is not None:
            acc = acc + part[r:r + tb]
    u = acc + bdw_ref[...]
    mu = jnp.mean(u, axis=-1, keepdims=True)
    uc = u - mu
    u = uc * lax.rsqrt(jnp.mean(uc * uc, axis=-1, keepdims=True) + EPS) * lng_ref[...] + lnb_ref[...]
    u = _silu(u).astype(BF16)
    y = _dot(u, w2_ref[...]) + b2_ref[...]
    o_ref[...] = x_ref[...] + gate_ref[0] * y


def conv_module(cfg, u, w_dw, b_dw, ln_g, ln_b, w2, b2, x, mod):
    halo = 16
    prev, nxt = _halo_specs(cfg, cfg.d, halo)
    return pl.pallas_call(
        functools.partial(_conv_kernel, cfg=cfg, halo=halo),
        grid=(cfg.nb,),
        in_specs=[prev, _row_spec(cfg, cfg.d), nxt, _full_spec(w_dw.shape), _full_spec((1, cfg.d)),
                  _full_spec((1, cfg.d)), _full_spec((1, cfg.d)), _full_spec(w2.shape), _full_spec((1, cfg.d)),
                  _row_spec(cfg, cfg.d), _mod_spec(cfg, 2)],
        out_specs=_row_spec(cfg, cfg.d),
        out_shape=jax.ShapeDtypeStruct((cfg.nt, cfg.d), F32),
        scratch_shapes=[pltpu.VMEM((cfg.tb + 2 * halo, cfg.d), F32)],
        compiler_params=_cparams(("arbitrary",)),
        name="conv_module",
    )(u, u, u, w_dw, b_dw, ln_g, ln_b, w2, b2, x, mod)


def _inproj_kernel(x_ref, g_ref, sh_ref, sc_ref, w_ref, b_ref, o_ref):
    h = _modnorm(x_ref[...], g_ref[...], sh_ref[0], sc_ref[0]).astype(BF16)
    o_ref[...] = (_dot(h, w_ref[...]) + b_ref[...]).astype(o_ref.dtype)


def in_project(cfg, x, norm_g, mod, w, bias):
    n = w.shape[1]
    return pl.pallas_call(
        _inproj_kernel,
        grid=(cfg.nb,),
        in_specs=[_row_spec(cfg, cfg.d), _full_spec((1, cfg.d)), _mod_spec(cfg, 0), _mod_spec(cfg, 1),
                  _full_spec(w.shape), _full_spec((1, n))],
        out_specs=_row_spec(cfg, n),
        out_shape=jax.ShapeDtypeStruct((cfg.nt, n), BF16),
        compiler_params=_cparams(("arbitrary",)),
        name="in_project",
    )(x, norm_g, mod, mod, w, bias)


LANES = 128


def _lane_tiles(rows, width):
    return pltpu.VMEM((width // LANES, rows, LANES), F32)


def _store_tiles(tiles_ref, value):
    for c in range(tiles_ref.shape[0]):
        tiles_ref[c] = value[:, c * LANES:(c + 1) * LANES]


def _load_tiles(tiles_ref, c0=0, c1=None):
    c1 = tiles_ref.shape[0] if c1 is None else c1
    return jnp.concatenate([tiles_ref[c] for c in range(c0, c1)], axis=1)


def _fold_rows(tiles_ref, o_ref, fold):
    nt, rows, _ = tiles_ref.shape
    w = nt * LANES
    for s in range(fold):
        for c in range(nt):
            o_ref[:, s * w + c * LANES:s * w + (c + 1) * LANES] = tiles_ref[c, pl.ds(s, rows // fold, stride=fold), :]


def _unfold_to_rows(f_ref, tiles_ref, fold):
    nt, rows, _ = tiles_ref.shape
    w = nt * LANES
    for s in range(fold):
        for c in range(nt):
            tiles_ref[c, pl.ds(s, rows // fold, stride=fold), :] = f_ref[:, s * w + c * LANES:s * w + (c + 1) * LANES]


def _short_conv_kernel(pp_ref, pc_ref, pn_ref, w_ref, b_ref, o_ref, ext_ref, nat_ref, *, cfg, halo, width):
    tb = cfg.tb
    _fill_ext(cfg, ext_ref, pp_ref, pc_ref, pn_ref, halo)
    half = (width - 1) // 2
    acc = b_ref[...]
    for k in range(width):
        off = halo - half + k
        acc = acc + w_ref[k:k + 1, :] * ext_ref[off:off + tb, :]
    _store_tiles(nat_ref, acc)
    _fold_rows(nat_ref, o_ref, cfg.fft_n2)


def short_conv(cfg, p, w, bias):
    halo = 16
    n = p.shape[1]
    n2 = cfg.fft_n2
    per = cfg.tb // halo
    last_blk = cfg.nt // halo - 1
    return pl.pallas_call(
        functools.partial(_short_conv_kernel, cfg=cfg, halo=halo, width=w.shape[0]),
        grid=(cfg.nb,),
        in_specs=[pl.BlockSpec((halo, n), lambda j: (jnp.maximum(j * per - 1, 0), 0)),
                  pl.BlockSpec((cfg.tb, n), lambda j: (j, 0)),
                  pl.BlockSpec((halo, n), lambda j: (jnp.minimum((j + 1) * per, last_blk), 0)),
                  _full_spec(w.shape), _full_spec((1, n))],
        out_specs=pl.BlockSpec((cfg.tb // n2, n2 * n), lambda j: (j, 0)),
        out_shape=jax.ShapeDtypeStruct((cfg.nt // n2, n2 * n), F32),
        scratch_shapes=[pltpu.VMEM((cfg.tb + 2 * halo, n), F32), _lane_tiles(cfg.tb, n)],
        compiler_params=_cparams(("arbitrary",)),
        name="short_conv",
    )(p, p, p, w, bias)


def _filter_mlp_kernel(feat_ref, w1_ref, b1_ref, f1_ref, w2_ref, b2_ref, f2_ref, w3_ref, delta_ref,
                       h_ref, s_ref, *, length, row_mul, step_mul):
    i = pl.program_id(0)
    z = jnp.sin(f1_ref[...] * (_dot3(feat_ref[0], w1_ref[...]) + b1_ref[...]))
    z = jnp.sin(f2_ref[...] * (_dot3(z, w2_ref[...]) + b2_ref[...]))
    h = _dot3(z, w3_ref[...])
    pos = (i * step_mul + row_mul * lax.broadcasted_iota(jnp.int32, h.shape, 0)).astype(F32)
    t = pos / (length - 1)
    h = h * jnp.exp(-t * jnp.abs(delta_ref[...]))
    col = lax.broadcasted_iota(jnp.int32, h.shape, 1)
    h = jnp.where((pos == 0.0) & (col >= h.shape[1] // 2), 0.0, h)
    h_ref[...] = h
    part = jnp.sum(jnp.abs(h), axis=0, keepdims=True)

    @pl.when(i == 0)
    def _():
        s_ref[...] = part

    @pl.when(i > 0)
    def _():
        s_ref[...] = s_ref[...] + part


def filter_taps(length, fold, feats, w1, b1, f1, w2, b2, f2, w3, deltas4):
    nh = w1.shape[1]
    n = w3.shape[1]
    if fold == 1:
        rows = min(length, 256)
        steps = length // rows
        feats3 = feats.reshape(steps, rows, feats.shape[1])
        row_mul, step_mul = 1, rows
        out_spec = pl.BlockSpec((rows, n), lambda i: (i, 0))
        out_shape = (length, n)
    else:
        rows = length // fold
        steps = fold
        feats3 = feats.reshape(rows, fold, feats.shape[1]).transpose(1, 0, 2)
        row_mul, step_mul = fold, 1
        out_spec = pl.BlockSpec((rows, n), lambda i: (0, i))
        out_shape = (rows, fold * n)
    return pl.pallas_call(
        functools.partial(_filter_mlp_kernel, length=length, row_mul=row_mul, step_mul=step_mul),
        grid=(steps,),
        in_specs=[pl.BlockSpec((1, rows, feats.shape[1]), lambda i: (i, 0, 0)), _full_spec(w1.shape),
                  _full_spec((1, nh)), _full_spec((1, nh)), _full_spec(w2.shape), _full_spec((1, nh)),
                  _full_spec((1, nh)), _full_spec(w3.shape), _full_spec((1, n))],
        out_specs=[out_spec, _full_spec((1, n))],
        out_shape=[jax.ShapeDtypeStruct(out_shape, F32), jax.ShapeDtypeStruct((1, n), F32)],
        compiler_params=_cparams(("arbitrary",)),
        name="filter_taps",
    )(feats3, w1, b1, f1, w2, b2, f2, w3, deltas4)


def _fft_stage1_kernel(z_ref, m_ref, a_ref):
    a_ref[...] = _dot(m_ref[0], z_ref[...].astype(BF16)).astype(a_ref.dtype)


def fft_stage1(z_view, mats, n_pairs, n2, lane_stride, lane_off, width, out_dtype=BF16):
    rows_out, rows_in = mats.shape[1:]
    return pl.pallas_call(
        _fft_stage1_kernel,
        grid=(n_pairs, n2),
        in_specs=[pl.BlockSpec((rows_in, width), lambda p, s: (p, s * lane_stride + lane_off)),
                  pl.BlockSpec((1, rows_out, rows_in), lambda p, s: (s, 0, 0))],
        out_specs=pl.BlockSpec((rows_out, width), lambda p, s: (p, s)),
        out_shape=jax.ShapeDtypeStruct((n_pairs * rows_out, n2 * width), out_dtype),
        compiler_params=_cparams(("arbitrary", "arbitrary")),
        name="fft_stage1",
    )(z_view, mats)


def _unfold_rows(a_ref, n2, width, lane_stride, lane_off):
    return jnp.concatenate(
        [a_ref[:, (s * lane_stride + lane_off) * width:(s * lane_stride + lane_off + 1) * width] for s in range(n2)],
        axis=0)


def _fft_mid_kernel(a_ref, h_ref, bf_ref, bi_ref, o_ref, *, n2, width):
    g = a_ref.shape[0]
    x = _dot(bf_ref[...], _unfold_rows(a_ref, n2, width, 1, 0))
    half = x.shape[0] // 2
    xr, xi = x[:half], x[half:]
    hr, hi = h_ref[:half, :], h_ref[half:, :]
    y = jnp.concatenate([xr * hr - xi * hi, xr * hi + xi * hr], axis=0).astype(BF16)
    b = _dot(bi_ref[...], y).astype(o_ref.dtype)
    for s in range(n2):
        o_ref[:, s * width:(s + 1) * width] = b[s * g:(s + 1) * g]


def fft_mid(a, h, h_col, bd_fwd, bd_inv, n_pairs, n_groups, n2, width):
    rows = bd_fwd.shape[0]
    g = rows // n2
    return pl.pallas_call(
        functools.partial(_fft_mid_kernel, n2=n2, width=width),
        grid=(n_groups, n_pairs),
        in_specs=[pl.BlockSpec((g, n2 * width), lambda q, p: (p * n_groups + q, 0)),
                  pl.BlockSpec((rows, width), lambda q, p: (q, h_col)),
                  _full_spec(bd_fwd.shape), _full_spec(bd_inv.shape)],
        out_specs=pl.BlockSpec((g, n2 * width), lambda q, p: (p * n_groups + q, 0)),
        out_shape=jax.ShapeDtypeStruct(a.shape, BF16),
        compiler_params=_cparams(("arbitrary", "arbitrary")),
        name="fft_mid",
    )(a, h, bd_fwd, bd_inv)


def _fft_spec_kernel(a_ref, bf_ref, nrm_ref, h_ref, *, n2, width, n_orders):
    o = pl.program_id(1)
    for order in range(n_orders):
        @pl.when(o == order)
        def _():
            xf = _dot(bf_ref[...], _unfold_rows(a_ref, n2, width, 2 * n_orders, order))
            xb = _dot(bf_ref[...], _unfold_rows(a_ref, n2, width, 2 * n_orders, n_orders + order))
            half = xf.shape[0] // 2
            inv = 1.0 / nrm_ref[...]
            h_ref[:half, :] = (xf[:half] + xb[:half]) * inv
            h_ref[half:, :] = (xf[half:] - xb[half:]) * inv


def fft_filter_spectrum(a, bd_fwd, norm, n_groups, n2, width, n_orders):
    rows = bd_fwd.shape[0]
    g = rows // n2
    return pl.pallas_call(
        functools.partial(_fft_spec_kernel, n2=n2, width=width, n_orders=n_orders),
        grid=(n_groups, n_orders),
        in_specs=[pl.BlockSpec((g, n2 * 2 * n_orders * width), lambda q, o: (q, 0)),
                  _full_spec(bd_fwd.shape),
                  pl.BlockSpec((1, width), lambda q, o: (0, o))],
        out_specs=pl.BlockSpec((rows, width), lambda q, o: (q, o)),
        out_shape=jax.ShapeDtypeStruct((n_groups * rows, n_orders * width), F32),
        compiler_params=_cparams(("arbitrary", "arbitrary")),
        name="fft_filter_spectrum",
    )(a, bd_fwd, norm)


def _fft_inv_kernel(b_ref, m_ref, z_ref, gate_ref, bias_ref, o_ref):
    y = _dot(m_ref[0], b_ref[...])
    o_ref[...] = gate_ref[...] * (y + z_ref[...] * bias_ref[...])


def fft_inverse_gate(b, mats, z_view, z_stride, z_off, gate_view, g_stride, g_off, bias, n_pairs, n2, width,
                     out_rows):
    rows_out, rows_in = mats.shape[1:]
    return pl.pallas_call(
        _fft_inv_kernel,
        grid=(n_pairs, n2),
        in_specs=[pl.BlockSpec((rows_in, width), lambda p, s: (p, s)),
                  pl.BlockSpec((1, rows_out, rows_in), lambda p, s: (s, 0, 0)),
                  pl.BlockSpec((rows_out, width), lambda p, s: (p, s * z_stride + z_off)),
                  pl.BlockSpec((rows_out, width), lambda p, s: (p, s * g_stride + g_off)),
                  _full_spec((1, width))],
        out_specs=pl.BlockSpec((rows_out, width), lambda p, s: (p, s)),
        out_shape=jax.ShapeDtypeStruct((out_rows, n2 * width), F32),
        compiler_params=_cparams(("arbitrary", "arbitrary")),
        name="fft_inverse_gate",
    )(b, mats, z_view, gate_view, bias)


def _ctx_spec_kernel(hf_ref, hb_ref, f_ref, nrm_ref, h_ref):
    half = f_ref.shape[0] // 2
    xf = _dot3(f_ref[...], hf_ref[...])
    xb = _dot3(f_ref[...], hb_ref[...])
    inv = 1.0 / nrm_ref[...]
    h_ref[:half, :] = (xf[:half] + xb[:half]) * inv
    h_ref[half:, :] = (xf[half:] - xb[half:]) * inv


def ctx_filter_spectrum(taps, f_real, norm, width, n_orders):
    length = taps.shape[0]
    rows = f_real.shape[0]
    return pl.pallas_call(
        _ctx_spec_kernel,
        grid=(n_orders,),
        in_specs=[pl.BlockSpec((length, width), lambda o: (0, o)),
                  pl.BlockSpec((length, width), lambda o: (0, n_orders + o)),
                  _full_spec(f_real.shape),
                  pl.BlockSpec((1, width), lambda o: (0, o))],
        out_specs=pl.BlockSpec((rows, width), lambda o: (0, o)),
        out_shape=jax.ShapeDtypeStruct((rows, n_orders * width), F32),
        compiler_params=_cparams(("arbitrary",)),
        name="ctx_filter_spectrum",
    )(taps, taps, f_real, norm)


def _ctx_conv_kernel(u_ref, h_ref, ff_ref, fi_ref, bias_ref, zin_ref, o_ref, nat_ref, z_ref, *, width, fold):
    del zin_ref
    half = ff_ref.shape[0] // 2
    _unfold_to_rows(u_ref, nat_ref, fold)
    wt = width // LANES
    z = _load_tiles(nat_ref, 0, wt)
    for n in range(2):
        x = _dot(ff_ref[...], z.astype(BF16))
        xr, xi = x[:half], x[half:]
        hr = h_ref[:half, n * width:(n + 1) * width]
        hi = h_ref[half:, n * width:(n + 1) * width]
        y = jnp.concatenate([xr * hr - xi * hi, xr * hi + xi * hr], axis=0).astype(BF16)
        y = _dot(fi_ref[...], y)
        z = _load_tiles(nat_ref, (n + 1) * wt, (n + 2) * wt) * (y + z * bias_ref[n:n + 1, :])
    _store_tiles(z_ref, z)
    _fold_rows(z_ref, o_ref, fold)


def ctx_long_conv(cfg, u_f, h_ctx, f_fwd, f_inv, bias, z_out):
    d = cfg.d
    n2 = cfg.fft_n2
    rows = 2 * cfg.ctx
    g = rows // n2
    base = cfg.nlat // rows
    return pl.pallas_call(
        functools.partial(_ctx_conv_kernel, width=d, fold=n2),
        grid=(cfg.batch // 2,),
        in_specs=[pl.BlockSpec((g, n2 * 3 * d), lambda p: (base + p, 0)),
                  _full_spec(h_ctx.shape), _full_spec(f_fwd.shape), _full_spec(f_inv.shape),
                  _full_spec(bias.shape), pl.BlockSpec(memory_space=pl.ANY)],
        out_specs=pl.BlockSpec((g, n2 * d), lambda p: (base + p, 0)),
        out_shape=jax.ShapeDtypeStruct(z_out.shape, F32),
        scratch_shapes=[_lane_tiles(rows, 3 * d), _lane_tiles(rows, d)],
        input_output_aliases={5: 0},
        compiler_params=_cparams(("arbitrary",)),
        name="ctx_long_conv",
    )(u_f, h_ctx, f_fwd, f_inv, bias, z_out)


def _dft_tables(n1, n2, group):
    n = n1 * n2
    h1 = n1 // 2
    f1 = jnp.arange(n1, dtype=jnp.int32)[None, :, None]
    s1 = jnp.arange(h1, dtype=jnp.int32)[None, None, :]
    s2 = jnp.arange(n2, dtype=jnp.int32)[:, None, None]
    k = (f1 * (n2 * s1 + s2)) % n
    ang = (2.0 * math.pi / n) * k.astype(F32)
    er, ei = jnp.cos(ang), -jnp.sin(ang)
    fwd = jnp.concatenate([jnp.concatenate([er, -ei], axis=2),
                           jnp.concatenate([ei, er], axis=2)], axis=1)
    cr, ci = jnp.swapaxes(er, 1, 2) / n, -jnp.swapaxes(ei, 1, 2) / n
    inv = jnp.concatenate([jnp.concatenate([cr, -ci], axis=2),
                           jnp.concatenate([ci, cr], axis=2)], axis=1)
    q = jnp.arange(n1 // group)[:, None, None]
    ri = jnp.arange(2)[None, :, None]
    j = jnp.arange(group)[None, None, :]
    perm = (ri * n1 + q * group + j).reshape(-1)
    fwd = fwd[:, perm, :]
    inv = inv[:, :, perm]
    real_only = fwd[:, :, :h1]
    a = jnp.arange(n2, dtype=jnp.int32)
    ang2 = (2.0 * math.pi / n2) * ((a[:, None] * a[None, :]) % n2).astype(F32)
    f2r, f2i = jnp.cos(ang2), -jnp.sin(ang2)
    eye = jnp.eye(group, dtype=F32)
    t_fwd = jnp.stack([jnp.stack([f2r, -f2i]), jnp.stack([f2i, f2r])])
    t_inv = jnp.stack([jnp.stack([f2r, f2i]), jnp.stack([-f2i, f2r])])
    size = 2 * group * n2
    bd_fwd = jnp.einsum("abfs,jk->afjsbk", t_fwd, eye).reshape(size, size)
    bd_inv = jnp.einsum("absf,jk->sajbfk", t_inv, eye).reshape(size, size)
    return (fwd.astype(BF16), inv.astype(BF16), real_only.astype(BF16), bd_fwd.astype(BF16),
            bd_inv.astype(BF16))


def _ctx_dft_tables(length):
    n = 2 * length
    f = jnp.arange(n, dtype=jnp.int32)[:, None]
    s = jnp.arange(length, dtype=jnp.int32)[None, :]
    ang = (2.0 * math.pi / n) * ((f * s) % n).astype(F32)
    fr, fi = jnp.cos(ang), -jnp.sin(ang)
    fwd = jnp.concatenate([jnp.concatenate([fr, -fi], axis=1), jnp.concatenate([fi, fr], axis=1)], axis=0)
    frt, fit = fr.T / n, fi.T / n
    inv = jnp.concatenate([jnp.concatenate([frt, fit], axis=1), jnp.concatenate([-fit, frt], axis=1)], axis=0)
    real_only = jnp.concatenate([fr, fi], axis=0)
    return fwd.astype(BF16), inv.astype(BF16), real_only


def _filter_features(length, bands):
    pos = jnp.arange(length, dtype=F32)[:, None]
    t = pos / (length - 1)
    w = 2.0 * math.pi * pos / length
    bnd = jnp.linspace(1e-4, bands - 1, bands, dtype=F32)
    feats = jnp.concatenate([t, jnp.cos(bnd * w), -jnp.sin(bnd * w)], axis=-1)
    return jnp.pad(feats, ((0, 0), (0, 128 - feats.shape[1])))


def hyena_long_convs(cfg, u, fp, long_bias):
    d = cfg.d
    n1, n2 = cfg.fft_n1, cfg.fft_n2
    group = 8
    n_groups = n1 // group
    n_orders = 2
    n_pairs = cfg.batch // 2
    nrow = cfg.nt // n2
    (w1, b1, fq1, w2, b2, fq2, w3) = fp
    nh = w1.shape[1]
    w1p = jnp.pad(w1, ((0, 128 - w1.shape[0]), (0, 0)))
    bands = (w1.shape[0] - 1) // 2
    deltas = jnp.linspace(math.log(1e-2) / 1.5, math.log(1e-2) / 0.3, d, dtype=F32)
    deltas4 = jnp.tile(deltas, 2 * n_orders)[None, :]
    mlp = (w1p, b1.reshape(1, nh), fq1.reshape(1, nh), w2, b2.reshape(1, nh), fq2.reshape(1, nh), w3, deltas4)

    fwd, inv, real_only, bd_fwd, bd_inv = _dft_tables(n1, n2, group)

    taps, sums = filter_taps(cfg.seq, n2, _filter_features(cfg.seq, bands), *mlp)
    norm = sums[:, :n_orders * d] + sums[:, n_orders * d:]
    a_f = fft_stage1(taps, real_only, 1, n2, 1, 0, 2 * n_orders * d)
    h_lat = fft_filter_spectrum(a_f, bd_fwd, norm, n_groups, n2, d, n_orders)

    cf_fwd, cf_inv, cf_real = _ctx_dft_tables(cfg.ctx)
    taps_c, sums_c = filter_taps(cfg.ctx, 1, _filter_features(cfg.ctx, bands), *mlp)
    norm_c = sums_c[:, :n_orders * d] + sums_c[:, n_orders * d:]
    h_ctx = ctx_filter_spectrum(taps_c, cf_real, norm_c, d, n_orders)

    u_view = u
    z_view, z_stride, z_off = u_view, 3, 0
    z = None
    for order in range(n_orders):
        a = fft_stage1(z_view, fwd, n_pairs, n2, z_stride, z_off, d)
        bmid = fft_mid(a, h_lat, order, bd_fwd, bd_inv, n_pairs, n_groups, n2, d)
        z = fft_inverse_gate(bmid, inv, z_view, z_stride, z_off, u_view, 3, 1 + order,
                             long_bias[order:order + 1], n_pairs, n2, d, nrow)
        z_view, z_stride, z_off = z, 1, 0
    return ctx_long_conv(cfg, u, h_ctx, cf_fwd, cf_inv, long_bias, z)


META_E, META_W, META_RANK = 0, 2, 4


def _route_kernel(x_ref, g_ref, sh_ref, sc_ref, wr_ref, br_ref, h_ref, meta_ref, cnt_ref, run_ref, *, cfg):
    ne = cfg.n_groups * cfg.epg
    step = pl.program_id(0)
    h = _modnorm(x_ref[...], g_ref[...], sh_ref[0], sc_ref[0])
    h_ref[...] = h.astype(h_ref.dtype)
    logits = _dot3(h, wr_ref[...]) + br_ref[...]
    lane = lax.broadcasted_iota(jnp.int32, logits.shape, 1).astype(F32)
    neg = -jnp.inf
    big = 1e9

    def first_argmax(mask):
        v = jnp.where(mask, logits, neg)
        mx = jnp.max(v, axis=-1, keepdims=True)
        idx = jnp.min(jnp.where(mask & (logits == mx), lane, big), axis=-1, keepdims=True)
        return mx, idx

    gmask = (lane >= ne) & (lane < ne + cfg.n_groups)
    gmax, gidx = first_argmax(gmask)
    g_p = 1.0 / jnp.sum(jnp.where(gmask, jnp.exp(logits - gmax), 0.0), axis=-1, keepdims=True)
    e0 = (gidx - ne) * cfg.epg
    emask = (lane >= e0) & (lane < e0 + cfg.epg)
    m1, i1 = first_argmax(emask)
    m2, i2 = first_argmax(emask & (lane != i1))
    r = jnp.exp(m2 - m1)
    w1 = g_p / (1.0 + r)
    w2 = g_p * r / (1.0 + r)

    @pl.when(step == 0)
    def _():
        run_ref[...] = jnp.zeros_like(run_ref)

    hit1 = lane == i1
    hit2 = lane == i2
    onehot = jnp.where(hit1 | hit2, 1.0, 0.0)
    tb = onehot.shape[0]
    row = lax.broadcasted_iota(jnp.int32, (tb, tb), 0)
    col = lax.broadcasted_iota(jnp.int32, (tb, tb), 1)
    earlier = jnp.where(col < row, 1.0, 0.0).astype(BF16)
    before = _dot(earlier, onehot.astype(BF16)) + run_ref[...]
    rank1 = jnp.sum(jnp.where(hit1, before, 0.0), axis=-1, keepdims=True)
    rank2 = jnp.sum(jnp.where(hit2, before, 0.0), axis=-1, keepdims=True)
    run_ref[...] = run_ref[...] + jnp.sum(onehot, axis=0, keepdims=True)
    cnt_ref[...] = run_ref[...]

    meta = jnp.zeros_like(logits)
    for k, v in enumerate((i1, i2, w1, w2, rank1, rank2)):
        meta = jnp.where(lane == float(k), v, meta)
    meta_ref[...] = meta


def route(cfg, x, norm_g, mod, w_route, b_route, n_blocks):
    return pl.pallas_call(
        functools.partial(_route_kernel, cfg=cfg),
        grid=(n_blocks,),
        in_specs=[_row_spec(cfg, cfg.d), _full_spec((1, cfg.d)), _mod_spec(cfg, 3), _mod_spec(cfg, 4),
                  _full_spec(w_route.shape), _full_spec((1, 128))],
        out_specs=[_row_spec(cfg, cfg.d), _row_spec(cfg, 128), _full_spec((1, 128))],
        out_shape=[jax.ShapeDtypeStruct((n_blocks * cfg.tb, cfg.d), F32),
                   jax.ShapeDtypeStruct((n_blocks * cfg.tb, 128), F32),
                   jax.ShapeDtypeStruct((1, 128), F32)],
        scratch_shapes=[pltpu.VMEM((1, 128), F32)],
        compiler_params=_cparams(("arbitrary",)),
        name="route",
    )(x, norm_g, mod, mod, w_route, b_route)


def _row_copies(hbm, idx_ref, vmem, sem, n_rows, to_hbm):
    rows = vmem.shape[0]

    def copy(r):
        h = hbm.at[pl.ds(idx_ref[0, 0, r], 1), :]
        v = vmem.at[pl.ds(r % rows, 1), :]
        return pltpu.make_async_copy(v, h, sem) if to_hbm else pltpu.make_async_copy(h, v, sem)

    def start():
        for r in range(n_rows):
            copy(r).start(priority=r % 2)

    def wait():
        def body(r, carry):
            copy(r).wait()
            return carry
        lax.fori_loop(0, n_rows, body, 0, unroll=8)

    return start, wait


def _dispatch_kernel(pos_ref, h_ref, hs_hbm, sem, *, tb):
    start, wait = _row_copies(hs_hbm, pos_ref, h_ref, sem.at[0], 2 * tb, to_hbm=True)
    start()
    wait()


def dispatch_rows(cfg, h, pos, n_blocks, nblk):
    tb = cfg.tb
    return pl.pallas_call(
        functools.partial(_dispatch_kernel, tb=tb),
        grid=(n_blocks,),
        in_specs=[pl.BlockSpec((1, 1, 2 * tb), lambda j: (j, 0, 0), memory_space=pltpu.SMEM),
                  _row_spec(cfg, cfg.d)],
        out_specs=pl.BlockSpec(memory_space=pl.ANY),
        out_shape=jax.ShapeDtypeStruct((nblk * tb, cfg.d), F32),
        scratch_shapes=[pltpu.SemaphoreType.DMA((1,))],
        compiler_params=_cparams(("arbitrary",)),
        name="dispatch_rows",
    )(pos, h)


def _expert_kernel(be_ref, valid_ref, h_ref, wg_ref, wu_ref, wd_ref, y_ref):
    del be_ref
    row = lax.broadcasted_iota(jnp.int32, h_ref.shape, 0)
    hb = jnp.where(row < valid_ref[pl.program_id(0)], h_ref[...], 0.0).astype(BF16)
    a = _dot(hb, wg_ref[0].astype(BF16))
    b = _dot(hb, wu_ref[0].astype(BF16))
    hid = (_silu(a) * b).astype(BF16)
    y_ref[...] = _dot(hid, wd_ref[0].astype(BF16))


def expert_mlp(cfg, h_sorted, block_expert, block_valid, w_gate, w_up, w_down, nblk):
    tb = cfg.tb
    de = cfg.d_exp
    grid_spec = pltpu.PrefetchScalarGridSpec(
        num_scalar_prefetch=2,
        grid=(nblk,),
        in_specs=[pl.BlockSpec((tb, cfg.d), lambda i, be, bv: (i, 0)),
                  pl.BlockSpec((1, cfg.d, de), lambda i, be, bv: (be[i], 0, 0)),
                  pl.BlockSpec((1, cfg.d, de), lambda i, be, bv: (be[i], 0, 0)),
                  pl.BlockSpec((1, de, cfg.d), lambda i, be, bv: (be[i], 0, 0))],
        out_specs=pl.BlockSpec((tb, cfg.d), lambda i, be, bv: (i, 0)),
    )
    return pl.pallas_call(
        _expert_kernel,
        grid_spec=grid_spec,
        out_shape=jax.ShapeDtypeStruct((nblk * tb, cfg.d), F32),
        compiler_params=_cparams(("arbitrary",)),
        name="expert_mlp",
    )(block_expert, block_valid, h_sorted, w_gate, w_up, w_down)


def _combine_kernel(pos_ref, pos_next_ref, x_ref, meta_ref, gate_ref, y_hbm, o_ref, ybuf, sem, *, tb, nblk):
    i = pl.program_id(0)
    slot = i % 2
    start_cur, wait_cur = _row_copies(y_hbm, pos_ref, ybuf.at[slot], sem.at[slot], 2 * tb, to_hbm=False)
    start_next, _ = _row_copies(y_hbm, pos_next_ref, ybuf.at[1 - slot], sem.at[1 - slot], 2 * tb, to_hbm=False)

    @pl.when(i == 0)
    def _():
        start_cur()

    @pl.when(i + 1 < nblk)
    def _():
        start_next()

    wait_cur()
    meta = meta_ref[...]
    w1 = meta[:, META_W:META_W + 1]
    w2 = meta[:, META_W + 1:META_W + 2]
    y = w1 * ybuf[slot, 0:tb, :] + w2 * ybuf[slot, tb:2 * tb, :]
    o_ref[...] = x_ref[...] + gate_ref[0] * y


def combine_residual(cfg, x, y_sorted, pos, meta, mod, gate_chunk, n_blocks):
    tb = cfg.tb
    return pl.pallas_call(
        functools.partial(_combine_kernel, tb=tb, nblk=n_blocks),
        grid=(n_blocks,),
        in_specs=[pl.BlockSpec((1, 1, 2 * tb), lambda j: (j, 0, 0), memory_space=pltpu.SMEM),
                  pl.BlockSpec((1, 1, 2 * tb), lambda j: (jnp.minimum(j + 1, n_blocks - 1), 0, 0),
                               memory_space=pltpu.SMEM),
                  _row_spec(cfg, cfg.d), _row_spec(cfg, 128), _mod_spec(cfg, gate_chunk),
                  pl.BlockSpec(memory_space=pl.ANY)],
        out_specs=_row_spec(cfg, cfg.d),
        out_shape=jax.ShapeDtypeStruct((n_blocks * tb, cfg.d), F32),
        scratch_shapes=[pltpu.VMEM((2, 2 * tb, cfg.d), F32), pltpu.SemaphoreType.DMA((2,))],
        compiler_params=_cparams(("arbitrary",)),
        name="combine_residual",
    )(pos, pos, x, meta, mod, y_sorted)


def _final_norm_kernel(x_ref, g_ref, o_ref):
    x = x_ref[...]
    o_ref[...] = x * lax.rsqrt(jnp.mean(x * x, axis=-1, keepdims=True) + EPS) * g_ref[...]


def final_norm(cfg, x, g):
    return pl.pallas_call(
        _final_norm_kernel,
        grid=(cfg.nbl,),
        in_specs=[_row_spec(cfg, cfg.d), _full_spec((1, cfg.d))],
        out_specs=_row_spec(cfg, cfg.d),
        out_shape=jax.ShapeDtypeStruct((cfg.nlat, cfg.d), F32),
        compiler_params=_cparams(("arbitrary",)),
        name="final_norm",
    )(x, g)


def _rope_tables(cfg):
    hd = cfg.head_dim
    pairs = hd // 4
    rows = cfg.seq // cfg.grid_w
    row = jnp.repeat(jnp.arange(rows), cfg.grid_w).astype(F32)
    col = jnp.tile(jnp.arange(cfg.grid_w), rows).astype(F32)
    inv_freq = 10000.0 ** (-jnp.arange(pairs, dtype=F32) / pairs)
    ar, ac = row[:, None] * inv_freq, col[:, None] * inv_freq
    cos = jnp.concatenate([jnp.cos(ar), jnp.cos(ar), jnp.cos(ac), jnp.cos(ac)], axis=-1)
    sin = jnp.concatenate([-jnp.sin(ar), jnp.sin(ar), -jnp.sin(ac), jnp.sin(ac)], axis=-1)
    cos = jnp.concatenate([cos, jnp.ones((cfg.tb, hd), F32)], axis=0)
    sin = jnp.concatenate([sin, jnp.zeros((cfg.tb, hd), F32)], axis=0)
    return cos, sin


def _moe_layer(cfg, x, norm_g, mod, w_group, b_group, w_router, b_router, w_gate, w_up, w_down, layer, n_blocks):
    ne = cfg.n_groups * cfg.epg
    w_route = jnp.pad(jnp.concatenate([w_router, w_group], axis=1), ((0, 0), (0, 128 - ne - cfg.n_groups)))
    b_route = jnp.pad(jnp.concatenate([b_router, b_group]), (0, 128 - ne - cfg.n_groups)).reshape(1, 128)
    h, meta, counts = route(cfg, x, norm_g, mod, w_route, b_route, n_blocks)

    tb = cfg.tb
    n = n_blocks * tb
    nblk = 2 * n_blocks + ne
    e_ids = meta[:, META_E:META_E + 2].astype(jnp.int32)
    ranks = meta[:, META_RANK:META_RANK + 2].astype(jnp.int32)
    cnt = counts[0, :ne].astype(jnp.int32)
    padded = ((cnt + tb - 1) // tb) * tb
    seg_end = jnp.cumsum(padded)
    seg_start = seg_end - padded
    onehot = e_ids[:, :, None] == jnp.arange(ne, dtype=jnp.int32)
    pos = jnp.sum(jnp.where(onehot, seg_start, 0), axis=-1) + ranks
    pos_blocks = pos.reshape(n_blocks, tb, 2).transpose(0, 2, 1).reshape(n_blocks, 1, 2 * tb)
    blk_row = jnp.arange(nblk, dtype=jnp.int32) * tb
    block_expert = jnp.minimum(jnp.sum(seg_end[None, :] <= blk_row[:, None], axis=-1), ne - 1).astype(jnp.int32)
    block_valid = jnp.clip(seg_start[block_expert] + cnt[block_expert] - blk_row, 0, tb).astype(jnp.int32)
    del n

    h_sorted = dispatch_rows(cfg, h, pos_blocks, n_blocks, nblk)
    y = expert_mlp(cfg, h_sorted, block_expert + layer * ne, block_valid, w_gate.reshape(-1, cfg.d, cfg.d_exp),
                   w_up.reshape(-1, cfg.d, cfg.d_exp), w_down.reshape(-1, cfg.d_exp, cfg.d), nblk)
    return combine_residual(cfg, x, y, pos_blocks, meta, mod, 5, n_blocks)


def _forward(cfg, x, c, ctx, c_ctx, w_mod, b_mod, norm_mix_g, norm_ffn_g,
             attn_w_q, attn_w_kv, attn_q_gain, attn_k_gain, attn_w_o,
             conv_w_pw1, conv_b_pw1, conv_w_dw, conv_b_dw, conv_ln_g, conv_ln_b, conv_w_pw2, conv_b_pw2,
             hy_w_in, hy_b_in, hy_w_short, hy_b_short, hy_f_w1, hy_f_b1, hy_f_freq1, hy_f_w2, hy_f_b2,
             hy_f_freq2, hy_f_w3, hy_long_bias, hy_w_out, hy_b_out,
             moe_w_group, moe_b_group, moe_w_router, moe_b_router, moe_w_gate, moe_w_up, moe_w_down,
             final_norm_g):
    d = cfg.d
    depth = w_mod.shape[0]
    xs = jnp.concatenate([x.reshape(-1, d), ctx.reshape(-1, d)], axis=0)
    c_all = jnp.concatenate([c, c_ctx[None, :], jnp.zeros((MOD_ROWS - cfg.batch - 1, d), F32)], axis=0)
    mods = modulation(c_all, w_mod, b_mod)
    rope_cos, rope_sin = _rope_tables(cfg)
    zero_bias = jnp.zeros((1, d), F32)

    for i in range(depth):
        kind, slot = i % 3, i // 3
        last = i == depth - 1
        n_blocks = cfg.nbl if last else cfg.nb
        mod = mods[i].reshape(MOD_ROWS * 6, 1, d)
        g_mix = norm_mix_g[i].reshape(1, d)
        if kind == 0:
            w_qkv = jnp.concatenate([attn_w_q[slot], attn_w_kv[slot]], axis=1).astype(BF16)
            qkv = qkv_project(cfg, xs, g_mix, mod, w_qkv, rope_cos, rope_sin,
                              attn_q_gain[slot].reshape(1, -1), attn_k_gain[slot].reshape(1, -1))
            o = attention(cfg, qkv)
            xs = project_residual(cfg, o, attn_w_o[slot].astype(BF16), zero_bias, xs, mod, 2, n_blocks)
        elif kind == 1:
            u = glu_project(cfg, xs, g_mix, mod, conv_w_pw1[slot].astype(BF16), conv_b_pw1[slot].reshape(1, -1))
            xs = conv_module(cfg, u, conv_w_dw[slot], conv_b_dw[slot].reshape(1, d), conv_ln_g[slot].reshape(1, d),
                             conv_ln_b[slot].reshape(1, d), conv_w_pw2[slot].astype(BF16),
                             conv_b_pw2[slot].reshape(1, d), xs, mod)
        else:
            p = in_project(cfg, xs, g_mix, mod, hy_w_in[slot].astype(BF16), hy_b_in[slot].reshape(1, -1))
            u = short_conv(cfg, p, hy_w_short[slot], hy_b_short[slot].reshape(1, -1))
            fp = (hy_f_w1[slot], hy_f_b1[slot], hy_f_freq1[slot], hy_f_w2[slot], hy_f_b2[slot],
                  hy_f_freq2[slot], hy_f_w3[slot])
            z2 = hyena_long_convs(cfg, u, fp, hy_long_bias[slot])
            xs = project_residual(cfg, z2, hy_w_out[slot].astype(BF16), hy_b_out[slot].reshape(1, d), xs, mod, 2,
                                  n_blocks, fold=cfg.fft_n2)
        xs = _moe_layer(cfg, xs, norm_ffn_g[i].reshape(1, d), mod, moe_w_group[i], moe_b_group[i],
                        moe_w_router[i], moe_b_router[i], moe_w_gate, moe_w_up, moe_w_down, i, n_blocks)

    out = final_norm(cfg, xs, final_norm_g.reshape(1, d))
    return out.reshape(cfg.batch, cfg.seq, d)


def kernel(x, c, ctx, c_ctx, w_mod, b_mod, norm_mix_g, norm_ffn_g, attn_w_q, attn_w_kv, attn_q_gain, attn_k_gain, attn_w_o, conv_w_pw1, conv_b_pw1, conv_w_dw, conv_b_dw, conv_ln_g, conv_ln_b, conv_w_pw2, conv_b_pw2, hy_w_in, hy_b_in, hy_w_short, hy_b_short, hy_f_w1, hy_f_b1, hy_f_freq1, hy_f_w2, hy_f_b2, hy_f_freq2, hy_f_w3, hy_long_bias, hy_w_out, hy_b_out, moe_w_group, moe_b_group, moe_w_router, moe_b_router, moe_w_gate, moe_w_up, moe_w_down, final_norm_g):
    return _forward(CFG, x, c, ctx, c_ctx, w_mod, b_mod, norm_mix_g, norm_ffn_g, attn_w_q, attn_w_kv, attn_q_gain, attn_k_gain, attn_w_o, conv_w_pw1, conv_b_pw1, conv_w_dw, conv_b_dw, conv_ln_g, conv_ln_b, conv_w_pw2, conv_b_pw2, hy_w_in, hy_b_in, hy_w_short, hy_b_short, hy_f_w1, hy_f_b1, hy_f_freq1, hy_f_w2, hy_f_b2, hy_f_freq2, hy_f_w3, hy_long_bias, hy_w_out, hy_b_out, moe_w_group, moe_b_group, moe_w_router, moe_b_router, moe_w_gate, moe_w_up, moe_w_down, final_norm_g)
```

```python
import functools
import math
from typing import NamedTuple

import jax
import jax.numpy as jnp
from jax import lax
from jax.experimental import pallas as pl
from jax.experimental.pallas import tpu as pltpu

F32 = jnp.float32
BF16 = jnp.bfloat16
EPS = 1e-6
V7X_VMEM_LIMIT_BYTES = 56 * 1024 * 1024


class Cfg(NamedTuple):
    batch: int
    seq: int
    ctx: int
    d: int
    grid_w: int
    n_heads: int
    n_kv: int
    head_dim: int
    conv_w: int
    n_groups: int
    epg: int
    d_exp: int
    tb: int
    fft_n1: int
    fft_n2: int
    kv_chunk: int

    @property
    def nlat(self):
        return self.batch * self.seq

    @property
    def nt(self):
        return self.batch * (self.seq + self.ctx)

    @property
    def nbl(self):
        return self.nlat // self.tb

    @property
    def nbc(self):
        return self.batch * self.ctx // self.tb

    @property
    def nb(self):
        return self.nbl + self.nbc

    @property
    def spb(self):
        return self.seq // self.tb

    @property
    def cpb(self):
        return self.ctx // self.tb


CFG = Cfg(batch=8, seq=4096, ctx=256, d=1024, grid_w=64, n_heads=8, n_kv=2, head_dim=128, conv_w=31,
          n_groups=4, epg=8, d_exp=256, tb=256, fft_n1=256, fft_n2=32, kv_chunk=1024)

MOD_ROWS = 16


def _cparams(sem):
    return pltpu.CompilerParams(dimension_semantics=sem, vmem_limit_bytes=V7X_VMEM_LIMIT_BYTES)


def _split_bf16(a):
    hi = a.astype(BF16)
    lo = (a - hi.astype(F32)).astype(BF16)
    return hi, lo


def _dot(a, b):
    return jnp.dot(a, b, preferred_element_type=F32)


def _dot3(a, b):
    ah, al = _split_bf16(a)
    bh, bl = _split_bf16(b)
    return _dot(ah, bh) + (_dot(ah, bl) + _dot(al, bh))


def _modnorm(x, g, sh, sc):
    ms = jnp.mean(x * x, axis=-1, keepdims=True)
    y = x * lax.rsqrt(ms + EPS) * g
    return y * (1.0 + sc) + sh


def _silu(x):
    return x * jax.nn.sigmoid(x)


def _mod_row(cfg, j):
    return jnp.where(j < cfg.nbl, j // cfg.spb, cfg.batch)


def _mod_spec(cfg, chunk):
    return pl.BlockSpec((1, 1, cfg.d), lambda j: (_mod_row(cfg, j) * 6 + chunk, 0, 0))


def _row_spec(cfg, width):
    return pl.BlockSpec((cfg.tb, width), lambda j: (j, 0))


def _full_spec(shape):
    n = len(shape)
    return pl.BlockSpec(shape, lambda *_: (0,) * n)


def _mod_kernel(c_ref, w_ref, b_ref, o_ref):
    o_ref[0] = _dot3(_silu(c_ref[...]), w_ref[0]) + b_ref[0]


def modulation(c_all, w_mod, b_mod):
    depth, d, n6 = w_mod.shape
    bn = n6 // 4
    return pl.pallas_call(
        _mod_kernel,
        grid=(depth, n6 // bn),
        in_specs=[_full_spec((MOD_ROWS, d)),
                  pl.BlockSpec((1, d, bn), lambda i, n: (i, 0, n)),
                  pl.BlockSpec((1, 1, bn), lambda i, n: (i, 0, n))],
        out_specs=pl.BlockSpec((1, MOD_ROWS, bn), lambda i, n: (i, 0, n)),
        out_shape=jax.ShapeDtypeStruct((depth, MOD_ROWS, n6), F32),
        compiler_params=_cparams(("arbitrary", "arbitrary")),
        name="modulation",
    )(c_all, w_mod, b_mod.reshape(depth, 1, n6))


def _qkv_kernel(x_ref, g_ref, sh_ref, sc_ref, w_ref, cs_ref, sn_ref, qg_ref, kg_ref, o_ref, *, cfg):
    hd = cfg.head_dim
    h = _modnorm(x_ref[...], g_ref[...], sh_ref[0], sc_ref[0]).astype(BF16)
    y = _dot(h, w_ref[...])
    cs = cs_ref[...]
    sn = sn_ref[...]
    lane = lax.broadcasted_iota(jnp.int32, cs.shape, 1)
    first_half = (lane % (hd // 2)) < (hd // 4)

    def norm_rope(v, gain, scale):
        ms = jnp.mean(v * v, axis=-1, keepdims=True)
        v = v * lax.rsqrt(ms + EPS) * gain
        partner = jnp.where(first_half, pltpu.roll(v, hd - hd // 4, 1), pltpu.roll(v, hd // 4, 1))
        return (v * cs + partner * sn) * scale

    nq = cfg.n_heads
    for hh in range(nq):
        sl = slice(hh * hd, (hh + 1) * hd)
        o_ref[:, sl] = norm_rope(y[:, sl], qg_ref[...], hd ** -0.5 * math.log2(math.e)).astype(o_ref.dtype)
    for hh in range(cfg.n_kv):
        sl = slice((nq + hh) * hd, (nq + hh + 1) * hd)
        o_ref[:, sl] = norm_rope(y[:, sl], kg_ref[...], 1.0).astype(o_ref.dtype)
    v0 = (nq + cfg.n_kv) * hd
    o_ref[:, v0:] = y[:, v0:].astype(o_ref.dtype)


def qkv_project(cfg, x, norm_g, mod, w_qkv, rope_cos, rope_sin, q_gain, k_gain):
    nqkv = w_qkv.shape[1]
    hd = cfg.head_dim

    def rope_idx(j):
        return (jnp.where(j < cfg.nbl, j % cfg.spb, cfg.spb), 0)

    return pl.pallas_call(
        functools.partial(_qkv_kernel, cfg=cfg),
        grid=(cfg.nb,),
        in_specs=[_row_spec(cfg, cfg.d), _full_spec((1, cfg.d)), _mod_spec(cfg, 0), _mod_spec(cfg, 1),
                  _full_spec((cfg.d, nqkv)),
                  pl.BlockSpec((cfg.tb, hd), rope_idx), pl.BlockSpec((cfg.tb, hd), rope_idx),
                  _full_spec((1, hd)), _full_spec((1, hd))],
        out_specs=_row_spec(cfg, nqkv),
        out_shape=jax.ShapeDtypeStruct((cfg.nt, nqkv), BF16),
        compiler_params=_cparams(("arbitrary",)),
        name="qkv_project",
    )(x, norm_g, mod, mod, w_qkv, rope_cos, rope_sin, q_gain, k_gain)


def _transpose_bf16(x):
    return x.astype(F32).T.astype(BF16)


def _attn_kernel(q_ref, kc_ref, vc_ref, kl_ref, vl_ref, o_ref, vtc_ref, vtl_ref, *, cfg, n_lat_chunks):
    hd = cfg.head_dim
    group = cfg.n_heads // cfg.n_kv
    tq = cfg.tb
    ch = cfg.kv_chunk

    @pl.when(pl.program_id(2) == 0)
    def _():
        vtc_ref[...] = _transpose_bf16(vc_ref[...])
        for c in range(n_lat_chunks):
            vtl_ref[c] = _transpose_bf16(vl_ref[c * ch:(c + 1) * ch, :])

    qt = jnp.concatenate([_transpose_bf16(q_ref[:, h * hd:(h + 1) * hd]) for h in range(group)], axis=1)

    s = _dot(kc_ref[...], qt)
    m = jnp.max(s, axis=0, keepdims=True)
    p = jnp.exp2(s - m)
    l = jnp.sum(p, axis=0, keepdims=True)
    acc = _dot(vtc_ref[...], p.astype(BF16))

    def body(c, carry):
        m, l, acc = carry
        start = pl.multiple_of(c * ch, ch)
        s = _dot(kl_ref[pl.ds(start, ch), :], qt)
        m_new = jnp.maximum(m, jnp.max(s, axis=0, keepdims=True))
        alpha = jnp.exp2(m - m_new)
        p = jnp.exp2(s - m_new)
        l = alpha * l + jnp.sum(p, axis=0, keepdims=True)
        acc = alpha * acc + _dot(vtl_ref[c], p.astype(BF16))
        return m_new, l, acc

    n = jnp.where(pl.program_id(2) < cfg.spb, n_lat_chunks, 0)
    m, l, acc = lax.fori_loop(0, n, body, (m, l, acc))
    o = acc / l
    for h in range(group):
        o_ref[:, h * hd:(h + 1) * hd] = o[:, h * tq:(h + 1) * tq].T.astype(o_ref.dtype)


def attention(cfg, qkv):
    hd = cfg.head_dim
    group = cfg.n_heads // cfg.n_kv
    gw = group * hd
    kcol = cfg.n_heads
    vcol = cfg.n_heads + cfg.n_kv
    assert cfg.cpb == 1

    def qrow(b, qb):
        return jnp.where(qb < cfg.spb, b * cfg.spb + qb, cfg.nbl + b)

    return pl.pallas_call(
        functools.partial(_attn_kernel, cfg=cfg, n_lat_chunks=cfg.seq // cfg.kv_chunk),
        grid=(cfg.batch, cfg.n_kv, cfg.spb + 1),
        in_specs=[pl.BlockSpec((cfg.tb, gw), lambda b, k, qb: (qrow(b, qb), k)),
                  pl.BlockSpec((cfg.ctx, hd), lambda b, k, qb: (cfg.nlat // cfg.ctx + b, kcol + k)),
                  pl.BlockSpec((cfg.ctx, hd), lambda b, k, qb: (cfg.nlat // cfg.ctx + b, vcol + k)),
                  pl.BlockSpec((cfg.seq, hd), lambda b, k, qb: (b, kcol + k)),
                  pl.BlockSpec((cfg.seq, hd), lambda b, k, qb: (b, vcol + k))],
        out_specs=pl.BlockSpec((cfg.tb, gw), lambda b, k, qb: (qrow(b, qb), k)),
        out_shape=jax.ShapeDtypeStruct((cfg.nt, cfg.n_heads * hd), BF16),
        scratch_shapes=[pltpu.VMEM((hd, cfg.ctx), BF16),
                        pltpu.VMEM((cfg.seq // cfg.kv_chunk, hd, cfg.kv_chunk), BF16)],
        compiler_params=_cparams(("arbitrary", "arbitrary", "arbitrary")),
        name="attention",
    )(qkv, qkv, qkv, qkv, qkv)


def _proj_res_kernel(a_ref, w_ref, b_ref, x_ref, gate_ref, o_ref):
    y = _dot(a_ref[...].astype(BF16), w_ref[...]) + b_ref[...]
    o_ref[...] = x_ref[...] + gate_ref[0] * y


def _proj_res_folded_kernel(a_ref, w_ref, b_ref, x_ref, gate_ref, o_ref, nat_ref, *, fold):
    _unfold_to_rows(a_ref, nat_ref, fold)
    y = _dot(_load_tiles(nat_ref).astype(BF16), w_ref[...]) + b_ref[...]
    o_ref[...] = x_ref[...] + gate_ref[0] * y


def project_residual(cfg, a, w, bias, x, mod, gate_chunk, n_blocks, fold=1):
    if fold == 1:
        body, a_spec, scratch = _proj_res_kernel, _row_spec(cfg, a.shape[1]), []
    else:
        body = functools.partial(_proj_res_folded_kernel, fold=fold)
        a_spec = pl.BlockSpec((cfg.tb // fold, a.shape[1]), lambda j: (j, 0))
        scratch = [_lane_tiles(cfg.tb, a.shape[1] // fold)]
    return pl.pallas_call(
        body,
        grid=(n_blocks,),
        in_specs=[a_spec, _full_spec(w.shape), _full_spec((1, cfg.d)),
                  _row_spec(cfg, cfg.d), _mod_spec(cfg, gate_chunk)],
        out_specs=_row_spec(cfg, cfg.d),
        out_shape=jax.ShapeDtypeStruct((n_blocks * cfg.tb, cfg.d), F32),
        scratch_shapes=scratch,
        compiler_params=_cparams(("arbitrary",)),
        name="project_residual",
    )(a, w, bias, x, mod)


def _glu_kernel(x_ref, g_ref, sh_ref, sc_ref, w_ref, b_ref, o_ref, *, d):
    h = _modnorm(x_ref[...], g_ref[...], sh_ref[0], sc_ref[0]).astype(BF16)
    y = _dot(h, w_ref[...]) + b_ref[...]
    o_ref[...] = (y[:, :d] * jax.nn.sigmoid(y[:, d:])).astype(o_ref.dtype)


def glu_project(cfg, x, norm_g, mod, w, bias):
    return pl.pallas_call(
        functools.partial(_glu_kernel, d=cfg.d),
        grid=(cfg.nb,),
        in_specs=[_row_spec(cfg, cfg.d), _full_spec((1, cfg.d)), _mod_spec(cfg, 0), _mod_spec(cfg, 1),
                  _full_spec(w.shape), _full_spec((1, w.shape[1]))],
        out_specs=_row_spec(cfg, cfg.d),
        out_shape=jax.ShapeDtypeStruct((cfg.nt, cfg.d), BF16),
        compiler_params=_cparams(("arbitrary",)),
        name="glu_project",
    )(x, norm_g, mod, mod, w, bias)


def _seq_edges(cfg, j):
    lat = j < cfg.nbl
    first = jnp.where(lat, j % cfg.spb == 0, (j - cfg.nbl) % cfg.cpb == 0)
    last = jnp.where(lat, j % cfg.spb == cfg.spb - 1, (j - cfg.nbl) % cfg.cpb == cfg.cpb - 1)
    return first, last


def _halo_specs(cfg, width, halo):
    per = cfg.tb // halo
    last_blk = cfg.nt // halo - 1
    prev = pl.BlockSpec((halo, width), lambda j: (jnp.maximum(j * per - 1, 0), 0))
    nxt = pl.BlockSpec((halo, width), lambda j: (jnp.minimum((j + 1) * per, last_blk), 0))
    return prev, nxt


def _fill_ext(cfg, ext_ref, prev_ref, cur_ref, next_ref, halo):
    first, last = _seq_edges(cfg, pl.program_id(0))
    tb = cfg.tb
    ext_ref[0:halo, :] = jnp.where(first, 0.0, prev_ref[...].astype(F32))
    ext_ref[halo:halo + tb, :] = cur_ref[...].astype(F32)
    ext_ref[halo + tb:2 * halo + tb, :] = jnp.where(last, 0.0, next_ref[...].astype(F32))


def _conv_kernel(up_ref, uc_ref, un_ref, wdw_ref, bdw_ref, lng_ref, lnb_ref, w2_ref, b2_ref, x_ref, gate_ref,
                 o_ref, ext_ref, *, cfg, halo):
    tb = cfg.tb
    _fill_ext(cfg, ext_ref, up_ref, uc_ref, un_ref, halo)
    half = (cfg.conv_w - 1) // 2
    sub = 8
    acc = jnp.zeros((tb, cfg.d), F32)
    for r in range(sub):
        part = None
        for k in range(cfg.conv_w):
            off = halo - half + k
            if off % sub != r:
                continue
            base = off - r
            term = wdw_ref[k:k + 1, :] * ext_ref[base:base + tb + sub, :]
            part = term if part is None else part + term
        if part is not None:
            acc = acc + part[r:r + tb]
    u = acc + bdw_ref[...]
    mu = jnp.mean(u, axis=-1, keepdims=True)
    uc = u - mu
    u = uc * lax.rsqrt(jnp.mean(uc * uc, axis=-1, keepdims=True) + EPS) * lng_ref[...] + lnb_ref[...]
    u = _silu(u).astype(BF16)
    y = _dot(u, w2_ref[...]) + b2_ref[...]
    o_ref[...] = x_ref[...] + gate_ref[0] * y


def conv_module(cfg, u, w_dw, b_dw, ln_g, ln_b, w2, b2, x, mod):
    halo = 16
    prev, nxt = _halo_specs(cfg, cfg.d, halo)
    return pl.pallas_call(
        functools.partial(_conv_kernel, cfg=cfg, halo=halo),
        grid=(cfg.nb,),
        in_specs=[prev, _row_spec(cfg, cfg.d), nxt, _full_spec(w_dw.shape), _full_spec((1, cfg.d)),
                  _full_spec((1, cfg.d)), _full_spec((1, cfg.d)), _full_spec(w2.shape), _full_spec((1, cfg.d)),
                  _row_spec(cfg, cfg.d), _mod_spec(cfg, 2)],
        out_specs=_row_spec(cfg, cfg.d),
        out_shape=jax.ShapeDtypeStruct((cfg.nt, cfg.d), F32),
        scratch_shapes=[pltpu.VMEM((cfg.tb + 2 * halo, cfg.d), F32)],
        compiler_params=_cparams(("arbitrary",)),
        name="conv_module",
    )(u, u, u, w_dw, b_dw, ln_g, ln_b, w2, b2, x, mod)


def _inproj_kernel(x_ref, g_ref, sh_ref, sc_ref, w_ref, b_ref, o_ref):
    h = _modnorm(x_ref[...], g_ref[...], sh_ref[0], sc_ref[0]).astype(BF16)
    o_ref[...] = (_dot(h, w_ref[...]) + b_ref[...]).astype(o_ref.dtype)


def in_project(cfg, x, norm_g, mod, w, bias):
    n = w.shape[1]
    return pl.pallas_call(
        _inproj_kernel,
        grid=(cfg.nb,),
        in_specs=[_row_spec(cfg, cfg.d), _full_spec((1, cfg.d)), _mod_spec(cfg, 0), _mod_spec(cfg, 1),
                  _full_spec(w.shape), _full_spec((1, n))],
        out_specs=_row_spec(cfg, n),
        out_shape=jax.ShapeDtypeStruct((cfg.nt, n), BF16),
        compiler_params=_cparams(("arbitrary",)),
        name="in_project",
    )(x, norm_g, mod, mod, w, bias)


LANES = 128


def _lane_tiles(rows, width):
    return pltpu.VMEM((width // LANES, rows, LANES), F32)


def _store_tiles(tiles_ref, value):
    for c in range(tiles_ref.shape[0]):
        tiles_ref[c] = value[:, c * LANES:(c + 1) * LANES]


def _load_tiles(tiles_ref, c0=0, c1=None):
    c1 = tiles_ref.shape[0] if c1 is None else c1
    return jnp.concatenate([tiles_ref[c] for c in range(c0, c1)], axis=1)


def _fold_rows(tiles_ref, o_ref, fold):
    nt, rows, _ = tiles_ref.shape
    w = nt * LANES
    for s in range(fold):
        for c in range(nt):
            o_ref[:, s * w + c * LANES:s * w + (c + 1) * LANES] = tiles_ref[c, pl.ds(s, rows // fold, stride=fold), :]


def _unfold_to_rows(f_ref, tiles_ref, fold):
    nt, rows, _ = tiles_ref.shape
    w = nt * LANES
    for s in range(fold):
        for c in range(nt):
            tiles_ref[c, pl.ds(s, rows // fold, stride=fold), :] = f_ref[:, s * w + c * LANES:s * w + (c + 1) * LANES]


def _short_conv_kernel(pp_ref, pc_ref, pn_ref, w_ref, b_ref, o_ref, ext_ref, nat_ref, *, cfg, halo, width):
    tb = cfg.tb
    _fill_ext(cfg, ext_ref, pp_ref, pc_ref, pn_ref, halo)
    half = (width - 1) // 2
    acc = b_ref[...]
    for k in range(width):
        off = halo - half + k
        acc = acc + w_ref[k:k + 1, :] * ext_ref[off:off + tb, :]
    _store_tiles(nat_ref, acc)
    _fold_rows(nat_ref, o_ref, cfg.fft_n2)


def short_conv(cfg, p, w, bias):
    halo = 16
    n = p.shape[1]
    n2 = cfg.fft_n2
    per = cfg.tb // halo
    last_blk = cfg.nt // halo - 1
    return pl.pallas_call(
        functools.partial(_short_conv_kernel, cfg=cfg, halo=halo, width=w.shape[0]),
        grid=(cfg.nb,),
        in_specs=[pl.BlockSpec((halo, n), lambda j: (jnp.maximum(j * per - 1, 0), 0)),
                  pl.BlockSpec((cfg.tb, n), lambda j: (j, 0)),
                  pl.BlockSpec((halo, n), lambda j: (jnp.minimum((j + 1) * per, last_blk), 0)),
                  _full_spec(w.shape), _full_spec((1, n))],
        out_specs=pl.BlockSpec((cfg.tb // n2, n2 * n), lambda j: (j, 0)),
        out_shape=jax.ShapeDtypeStruct((cfg.nt // n2, n2 * n), F32),
        scratch_shapes=[pltpu.VMEM((cfg.tb + 2 * halo, n), F32), _lane_tiles(cfg.tb, n)],
        compiler_params=_cparams(("arbitrary",)),
        name="short_conv",
    )(p, p, p, w, bias)


def _filter_mlp_kernel(feat_ref, w1_ref, b1_ref, f1_ref, w2_ref, b2_ref, f2_ref, w3_ref, delta_ref,
                       h_ref, s_ref, *, length, row_mul, step_mul):
    i = pl.program_id(0)
    z = jnp.sin(f1_ref[...] * (_dot3(feat_ref[0], w1_ref[...]) + b1_ref[...]))
    z = jnp.sin(f2_ref[...] * (_dot3(z, w2_ref[...]) + b2_ref[...]))
    h = _dot3(z, w3_ref[...])
    pos = (i * step_mul + row_mul * lax.broadcasted_iota(jnp.int32, h.shape, 0)).astype(F32)
    t = pos / (length - 1)
    h = h * jnp.exp(-t * jnp.abs(delta_ref[...]))
    col = lax.broadcasted_iota(jnp.int32, h.shape, 1)
    h = jnp.where((pos == 0.0) & (col >= h.shape[1] // 2), 0.0, h)
    h_ref[...] = h
    part = jnp.sum(jnp.abs(h), axis=0, keepdims=True)

    @pl.when(i == 0)
    def _():
        s_ref[...] = part

    @pl.when(i > 0)
    def _():
        s_ref[...] = s_ref[...] + part


def filter_taps(length, fold, feats, w1, b1, f1, w2, b2, f2, w3, deltas4):
    nh = w1.shape[1]
    n = w3.shape[1]
    if fold == 1:
        rows = min(length, 256)
        steps = length // rows
        feats3 = feats.reshape(steps, rows, feats.shape[1])
        row_mul, step_mul = 1, rows
        out_spec = pl.BlockSpec((rows, n), lambda i: (i, 0))
        out_shape = (length, n)
    else:
        rows = length // fold
        steps = fold
        feats3 = feats.reshape(rows, fold, feats.shape[1]).transpose(1, 0, 2)
        row_mul, step_mul = fold, 1
        out_spec = pl.BlockSpec((rows, n), lambda i: (0, i))
        out_shape = (rows, fold * n)
    return pl.pallas_call(
        functools.partial(_filter_mlp_kernel, length=length, row_mul=row_mul, step_mul=step_mul),
        grid=(steps,),
        in_specs=[pl.BlockSpec((1, rows, feats.shape[1]), lambda i: (i, 0, 0)), _full_spec(w1.shape),
                  _full_spec((1, nh)), _full_spec((1, nh)), _full_spec(w2.shape), _full_spec((1, nh)),
                  _full_spec((1, nh)), _full_spec(w3.shape), _full_spec((1, n))],
        out_specs=[out_spec, _full_spec((1, n))],
        out_shape=[jax.ShapeDtypeStruct(out_shape, F32), jax.ShapeDtypeStruct((1, n), F32)],
        compiler_params=_cparams(("arbitrary",)),
        name="filter_taps",
    )(feats3, w1, b1, f1, w2, b2, f2, w3, deltas4)


def _fft_stage1_kernel(z_ref, m_ref, a_ref):
    a_ref[...] = _dot(m_ref[0], z_ref[...].astype(BF16)).astype(a_ref.dtype)


def fft_stage1(z_view, mats, n_pairs, n2, lane_stride, lane_off, width, out_dtype=BF16):
    rows_out, rows_in = mats.shape[1:]
    return pl.pallas_call(
        _fft_stage1_kernel,
        grid=(n_pairs, n2),
        in_specs=[pl.BlockSpec((rows_in, width), lambda p, s: (p, s * lane_stride + lane_off)),
                  pl.BlockSpec((1, rows_out, rows_in), lambda p, s: (s, 0, 0))],
        out_specs=pl.BlockSpec((rows_out, width), lambda p, s: (p, s)),
        out_shape=jax.ShapeDtypeStruct((n_pairs * rows_out, n2 * width), out_dtype),
        compiler_params=_cparams(("arbitrary", "arbitrary")),
        name="fft_stage1",
    )(z_view, mats)


def _unfold_rows(a_ref, n2, width, lane_stride, lane_off):
    return jnp.concatenate(
        [a_ref[:, (s * lane_stride + lane_off) * width:(s * lane_stride + lane_off + 1) * width] for s in range(n2)],
        axis=0)


def _fft_mid_kernel(a_ref, h_ref, bf_ref, bi_ref, o_ref, *, n2, width):
    g = a_ref.shape[0]
    x = _dot(bf_ref[...], _unfold_rows(a_ref, n2, width, 1, 0))
    half = x.shape[0] // 2
    xr, xi = x[:half], x[half:]
    hr, hi = h_ref[:half, :], h_ref[half:, :]
    y = jnp.concatenate([xr * hr - xi * hi, xr * hi + xi * hr], axis=0).astype(BF16)
    b = _dot(bi_ref[...], y).astype(o_ref.dtype)
    for s in range(n2):
        o_ref[:, s * width:(s + 1) * width] = b[s * g:(s + 1) * g]


def fft_mid(a, h, h_col, bd_fwd, bd_inv, n_pairs, n_groups, n2, width):
    rows = bd_fwd.shape[0]
    g = rows // n2
    return pl.pallas_call(
        functools.partial(_fft_mid_kernel, n2=n2, width=width),
        grid=(n_groups, n_pairs),
        in_specs=[pl.BlockSpec((g, n2 * width), lambda q, p: (p * n_groups + q, 0)),
                  pl.BlockSpec((rows, width), lambda q, p: (q, h_col)),
                  _full_spec(bd_fwd.shape), _full_spec(bd_inv.shape)],
        out_specs=pl.BlockSpec((g, n2 * width), lambda q, p: (p * n_groups + q, 0)),
        out_shape=jax.ShapeDtypeStruct(a.shape, BF16),
        compiler_params=_cparams(("arbitrary", "arbitrary")),
        name="fft_mid",
    )(a, h, bd_fwd, bd_inv)


def _fft_spec_kernel(a_ref, bf_ref, nrm_ref, h_ref, *, n2, width, n_orders):
    o = pl.program_id(1)
    for order in range(n_orders):
        @pl.when(o == order)
        def _():
            xf = _dot(bf_ref[...], _unfold_rows(a_ref, n2, width, 2 * n_orders, order))
            xb = _dot(bf_ref[...], _unfold_rows(a_ref, n2, width, 2 * n_orders, n_orders + order))
            half = xf.shape[0] // 2
            inv = 1.0 / nrm_ref[...]
            h_ref[:half, :] = (xf[:half] + xb[:half]) * inv
            h_ref[half:, :] = (xf[half:] - xb[half:]) * inv


def fft_filter_spectrum(a, bd_fwd, norm, n_groups, n2, width, n_orders):
    rows = bd_fwd.shape[0]
    g = rows // n2
    return pl.pallas_call(
        functools.partial(_fft_spec_kernel, n2=n2, width=width, n_orders=n_orders),
        grid=(n_groups, n_orders),
        in_specs=[pl.BlockSpec((g, n2 * 2 * n_orders * width), lambda q, o: (q, 0)),
                  _full_spec(bd_fwd.shape),
                  pl.BlockSpec((1, width), lambda q, o: (0, o))],
        out_specs=pl.BlockSpec((rows, width), lambda q, o: (q, o)),
        out_shape=jax.ShapeDtypeStruct((n_groups * rows, n_orders * width), F32),
        compiler_params=_cparams(("arbitrary", "arbitrary")),
        name="fft_filter_spectrum",
    )(a, bd_fwd, norm)


def _fft_inv_kernel(b_ref, m_ref, z_ref, gate_ref, bias_ref, o_ref):
    y = _dot(m_ref[0], b_ref[...])
    o_ref[...] = gate_ref[...] * (y + z_ref[...] * bias_ref[...])


def fft_inverse_gate(b, mats, z_view, z_stride, z_off, gate_view, g_stride, g_off, bias, n_pairs, n2, width,
                     out_rows):
    rows_out, rows_in = mats.shape[1:]
    return pl.pallas_call(
        _fft_inv_kernel,
        grid=(n_pairs, n2),
        in_specs=[pl.BlockSpec((rows_in, width), lambda p, s: (p, s)),
                  pl.BlockSpec((1, rows_out, rows_in), lambda p, s: (s, 0, 0)),
                  pl.BlockSpec((rows_out, width), lambda p, s: (p, s * z_stride + z_off)),
                  pl.BlockSpec((rows_out, width), lambda p, s: (p, s * g_stride + g_off)),
                  _full_spec((1, width))],
        out_specs=pl.BlockSpec((rows_out, width), lambda p, s: (p, s)),
        out_shape=jax.ShapeDtypeStruct((out_rows, n2 * width), F32),
        compiler_params=_cparams(("arbitrary", "arbitrary")),
        name="fft_inverse_gate",
    )(b, mats, z_view, gate_view, bias)


def _ctx_spec_kernel(hf_ref, hb_ref, f_ref, nrm_ref, h_ref):
    half = f_ref.shape[0] // 2
    xf = _dot3(f_ref[...], hf_ref[...])
    xb = _dot3(f_ref[...], hb_ref[...])
    inv = 1.0 / nrm_ref[...]
    h_ref[:half, :] = (xf[:half] + xb[:half]) * inv
    h_ref[half:, :] = (xf[half:] - xb[half:]) * inv


def ctx_filter_spectrum(taps, f_real, norm, width, n_orders):
    length = taps.shape[0]
    rows = f_real.shape[0]
    return pl.pallas_call(
        _ctx_spec_kernel,
        grid=(n_orders,),
        in_specs=[pl.BlockSpec((length, width), lambda o: (0, o)),
                  pl.BlockSpec((length, width), lambda o: (0, n_orders + o)),
                  _full_spec(f_real.shape),
                  pl.BlockSpec((1, width), lambda o: (0, o))],
        out_specs=pl.BlockSpec((rows, width), lambda o: (0, o)),
        out_shape=jax.ShapeDtypeStruct((rows, n_orders * width), F32),
        compiler_params=_cparams(("arbitrary",)),
        name="ctx_filter_spectrum",
    )(taps, taps, f_real, norm)


def _ctx_conv_kernel(u_ref, h_ref, ff_ref, fi_ref, bias_ref, zin_ref, o_ref, nat_ref, z_ref, *, width, fold):
    del zin_ref
    half = ff_ref.shape[0] // 2
    _unfold_to_rows(u_ref, nat_ref, fold)
    wt = width // LANES
    z = _load_tiles(nat_ref, 0, wt)
    for n in range(2):
        x = _dot(ff_ref[...], z.astype(BF16))
        xr, xi = x[:half], x[half:]
        hr = h_ref[:half, n * width:(n + 1) * width]
        hi = h_ref[half:, n * width:(n + 1) * width]
        y = jnp.concatenate([xr * hr - xi * hi, xr * hi + xi * hr], axis=0).astype(BF16)
        y = _dot(fi_ref[...], y)
        z = _load_tiles(nat_ref, (n + 1) * wt, (n + 2) * wt) * (y + z * bias_ref[n:n + 1, :])
    _store_tiles(z_ref, z)
    _fold_rows(z_ref, o_ref, fold)


def ctx_long_conv(cfg, u_f, h_ctx, f_fwd, f_inv, bias, z_out):
    d = cfg.d
    n2 = cfg.fft_n2
    rows = 2 * cfg.ctx
    g = rows // n2
    base = cfg.nlat // rows
    return pl.pallas_call(
        functools.partial(_ctx_conv_kernel, width=d, fold=n2),
        grid=(cfg.batch // 2,),
        in_specs=[pl.BlockSpec((g, n2 * 3 * d), lambda p: (base + p, 0)),
                  _full_spec(h_ctx.shape), _full_spec(f_fwd.shape), _full_spec(f_inv.shape),
                  _full_spec(bias.shape), pl.BlockSpec(memory_space=pl.ANY)],
        out_specs=pl.BlockSpec((g, n2 * d), lambda p: (base + p, 0)),
        out_shape=jax.ShapeDtypeStruct(z_out.shape, F32),
        scratch_shapes=[_lane_tiles(rows, 3 * d), _lane_tiles(rows, d)],
        input_output_aliases={5: 0},
        compiler_params=_cparams(("arbitrary",)),
        name="ctx_long_conv",
    )(u_f, h_ctx, f_fwd, f_inv, bias, z_out)


def _dft_tables(n1, n2, group):
    n = n1 * n2
    h1 = n1 // 2
    f1 = jnp.arange(n1, dtype=jnp.int32)[None, :, None]
    s1 = jnp.arange(h1, dtype=jnp.int32)[None, None, :]
    s2 = jnp.arange(n2, dtype=jnp.int32)[:, None, None]
    k = (f1 * (n2 * s1 + s2)) % n
    ang = (2.0 * math.pi / n) * k.astype(F32)
    er, ei = jnp.cos(ang), -jnp.sin(ang)
    fwd = jnp.concatenate([jnp.concatenate([er, -ei], axis=2),
                           jnp.concatenate([ei, er], axis=2)], axis=1)
    cr, ci = jnp.swapaxes(er, 1, 2) / n, -jnp.swapaxes(ei, 1, 2) / n
    inv = jnp.concatenate([jnp.concatenate([cr, -ci], axis=2),
                           jnp.concatenate([ci, cr], axis=2)], axis=1)
    q = jnp.arange(n1 // group)[:, None, None]
    ri = jnp.arange(2)[None, :, None]
    j = jnp.arange(group)[None, None, :]
    perm = (ri * n1 + q * group + j).reshape(-1)
    fwd = fwd[:, perm, :]
    inv = inv[:, :, perm]
    real_only = fwd[:, :, :h1]
    a = jnp.arange(n2, dtype=jnp.int32)
    ang2 = (2.0 * math.pi / n2) * ((a[:, None] * a[None, :]) % n2).astype(F32)
    f2r, f2i = jnp.cos(ang2), -jnp.sin(ang2)
    eye = jnp.eye(group, dtype=F32)
    t_fwd = jnp.stack([jnp.stack([f2r, -f2i]), jnp.stack([f2i, f2r])])
    t_inv = jnp.stack([jnp.stack([f2r, f2i]), jnp.stack([-f2i, f2r])])
    size = 2 * group * n2
    bd_fwd = jnp.einsum("abfs,jk->afjsbk", t_fwd, eye).reshape(size, size)
    bd_inv = jnp.einsum("absf,jk->sajbfk", t_inv, eye).reshape(size, size)
    return (fwd.astype(BF16), inv.astype(BF16), real_only.astype(BF16), bd_fwd.astype(BF16),
            bd_inv.astype(BF16))


def _ctx_dft_tables(length):
    n = 2 * length
    f = jnp.arange(n, dtype=jnp.int32)[:, None]
    s = jnp.arange(length, dtype=jnp.int32)[None, :]
    ang = (2.0 * math.pi / n) * ((f * s) % n).astype(F32)
    fr, fi = jnp.cos(ang), -jnp.sin(ang)
    fwd = jnp.concatenate([jnp.concatenate([fr, -fi], axis=1), jnp.concatenate([fi, fr], axis=1)], axis=0)
    frt, fit = fr.T / n, fi.T / n
    inv = jnp.concatenate([jnp.concatenate([frt, fit], axis=1), jnp.concatenate([-fit, frt], axis=1)], axis=0)
    real_only = jnp.concatenate([fr, fi], axis=0)
    return fwd.astype(BF16), inv.astype(BF16), real_only


def _filter_features(length, bands):
    pos = jnp.arange(length, dtype=F32)[:, None]
    t = pos / (length - 1)
    w = 2.0 * math.pi * pos / length
    bnd = jnp.linspace(1e-4, bands - 1, bands, dtype=F32)
    feats = jnp.concatenate([t, jnp.cos(bnd * w), -jnp.sin(bnd * w)], axis=-1)
    return jnp.pad(feats, ((0, 0), (0, 128 - feats.shape[1])))


def hyena_long_convs(cfg, u, fp, long_bias):
    d = cfg.d
    n1, n2 = cfg.fft_n1, cfg.fft_n2
    group = 8
    n_groups = n1 // group
    n_orders = 2
    n_pairs = cfg.batch // 2
    nrow = cfg.nt // n2
    (w1, b1, fq1, w2, b2, fq2, w3) = fp
    nh = w1.shape[1]
    w1p = jnp.pad(w1, ((0, 128 - w1.shape[0]), (0, 0)))
    bands = (w1.shape[0] - 1) // 2
    deltas = jnp.linspace(math.log(1e-2) / 1.5, math.log(1e-2) / 0.3, d, dtype=F32)
    deltas4 = jnp.tile(deltas, 2 * n_orders)[None, :]
    mlp = (w1p, b1.reshape(1, nh), fq1.reshape(1, nh), w2, b2.reshape(1, nh), fq2.reshape(1, nh), w3, deltas4)

    fwd, inv, real_only, bd_fwd, bd_inv = _dft_tables(n1, n2, group)

    taps, sums = filter_taps(cfg.seq, n2, _filter_features(cfg.seq, bands), *mlp)
    norm = sums[:, :n_orders * d] + sums[:, n_orders * d:]
    a_f = fft_stage1(taps, real_only, 1, n2, 1, 0, 2 * n_orders * d)
    h_lat = fft_filter_spectrum(a_f, bd_fwd, norm, n_groups, n2, d, n_orders)

    cf_fwd, cf_inv, cf_real = _ctx_dft_tables(cfg.ctx)
    taps_c, sums_c = filter_taps(cfg.ctx, 1, _filter_features(cfg.ctx, bands), *mlp)
    norm_c = sums_c[:, :n_orders * d] + sums_c[:, n_orders * d:]
    h_ctx = ctx_filter_spectrum(taps_c, cf_real, norm_c, d, n_orders)

    u_view = u
    z_view, z_stride, z_off = u_view, 3, 0
    z = None
    for order in range(n_orders):
        a = fft_stage1(z_view, fwd, n_pairs, n2, z_stride, z_off, d)
        bmid = fft_mid(a, h_lat, order, bd_fwd, bd_inv, n_pairs, n_groups, n2, d)
        z = fft_inverse_gate(bmid, inv, z_view, z_stride, z_off, u_view, 3, 1 + order,
                             long_bias[order:order + 1], n_pairs, n2, d, nrow)
        z_view, z_stride, z_off = z, 1, 0
    return ctx_long_conv(cfg, u, h_ctx, cf_fwd, cf_inv, long_bias, z)


META_E, META_W, META_LP = 0, 2, 4
CHUNK = 8
TAB_N, TAB_SRC, TAB_DST, TAB_TOTAL = 0, 32, 64, 96


def _local_rows(cfg):
    ne = cfg.n_groups * cfg.epg
    return -(-(2 * cfg.tb + (CHUNK - 1) * ne) // 128) * 128


def _route_kernel(x_ref, g_ref, sh_ref, sc_ref, wr_ref, br_ref, h_ref, meta_ref, tab_ref, cnt_ref, run_ref, *,
                  cfg):
    ne = cfg.n_groups * cfg.epg
    step = pl.program_id(0)
    h = _modnorm(x_ref[...], g_ref[...], sh_ref[0], sc_ref[0])
    h_ref[...] = h.astype(h_ref.dtype)
    logits = _dot3(h, wr_ref[...]) + br_ref[...]
    lane = lax.broadcasted_iota(jnp.int32, logits.shape, 1).astype(F32)
    neg = -jnp.inf
    big = 1e9

    def first_argmax(mask):
        v = jnp.where(mask, logits, neg)
        mx = jnp.max(v, axis=-1, keepdims=True)
        idx = jnp.min(jnp.where(mask & (logits == mx), lane, big), axis=-1, keepdims=True)
        return mx, idx

    gmask = (lane >= ne) & (lane < ne + cfg.n_groups)
    gmax, gidx = first_argmax(gmask)
    g_p = 1.0 / jnp.sum(jnp.where(gmask, jnp.exp(logits - gmax), 0.0), axis=-1, keepdims=True)
    e0 = (gidx - ne) * cfg.epg
    emask = (lane >= e0) & (lane < e0 + cfg.epg)
    m1, i1 = first_argmax(emask)
    m2, i2 = first_argmax(emask & (lane != i1))
    r = jnp.exp(m2 - m1)
    w1 = g_p / (1.0 + r)
    w2 = g_p * r / (1.0 + r)

    @pl.when(step == 0)
    def _():
        run_ref[...] = jnp.zeros_like(run_ref)

    hit1 = lane == i1
    hit2 = lane == i2
    onehot = jnp.where(hit1 | hit2, 1.0, 0.0)
    tb = onehot.shape[0]
    row = lax.broadcasted_iota(jnp.int32, (tb, tb), 0)
    col = lax.broadcasted_iota(jnp.int32, (tb, tb), 1)
    earlier = jnp.where(col < row, 1.0, 0.0).astype(BF16)
    before = _dot(earlier, onehot.astype(BF16))
    chunks = jnp.floor((jnp.sum(onehot, axis=0, keepdims=True) + (CHUNK - 1)) * (1.0 / CHUNK))
    ea = lax.broadcasted_iota(jnp.int32, (128, 128), 0)
    eb = lax.broadcasted_iota(jnp.int32, (128, 128), 1)
    lower_experts = jnp.where(ea < eb, 1.0, 0.0).astype(BF16)
    lstart = CHUNK * _dot(jnp.broadcast_to(chunks, (8, 128)).astype(BF16), lower_experts)[0:1]
    local = before + lstart
    lp1 = jnp.sum(jnp.where(hit1, local, 0.0), axis=-1, keepdims=True)
    lp2 = jnp.sum(jnp.where(hit2, local, 0.0), axis=-1, keepdims=True)

    tab = jnp.zeros((8, 128), F32)
    rows8 = lax.broadcasted_iota(jnp.int32, (8, 128), 0)
    for k, v in enumerate((chunks, lstart, run_ref[...])):
        tab = jnp.where(rows8 == k, v, tab)
    tab_ref[0] = tab
    run_ref[...] = run_ref[...] + chunks
    cnt_ref[...] = run_ref[...]

    meta = jnp.zeros_like(logits)
    for k, v in enumerate((i1, i2, w1, w2, lp1, lp2)):
        meta = jnp.where(lane == float(k), v, meta)
    meta_ref[...] = meta


def route(cfg, x, norm_g, mod, w_route, b_route, n_blocks):
    return pl.pallas_call(
        functools.partial(_route_kernel, cfg=cfg),
        grid=(n_blocks,),
        in_specs=[_row_spec(cfg, cfg.d), _full_spec((1, cfg.d)), _mod_spec(cfg, 3), _mod_spec(cfg, 4),
                  _full_spec(w_route.shape), _full_spec((1, 128))],
        out_specs=[_row_spec(cfg, cfg.d), _row_spec(cfg, 128), pl.BlockSpec((1, 8, 128), lambda j: (j, 0, 0)),
                   _full_spec((1, 128))],
        out_shape=[jax.ShapeDtypeStruct((n_blocks * cfg.tb, cfg.d), F32),
                   jax.ShapeDtypeStruct((n_blocks * cfg.tb, 128), F32),
                   jax.ShapeDtypeStruct((n_blocks, 8, 128), F32),
                   jax.ShapeDtypeStruct((1, 128), F32)],
        scratch_shapes=[pltpu.VMEM((1, 128), F32)],
        compiler_params=_cparams(("arbitrary",)),
        name="route",
    )(x, norm_g, mod, mod, w_route, b_route)


def _chunk_copies(hbm, tab_ref, vmem, sem, n_experts, to_hbm):
    def copy(e, k):
        src = pl.multiple_of(tab_ref[0, 0, TAB_SRC + e] + k * CHUNK, CHUNK)
        dst = pl.multiple_of(tab_ref[0, 0, TAB_DST + e] + k * CHUNK, CHUNK)
        v = vmem.at[pl.ds(src, CHUNK), :]
        h = hbm.at[pl.ds(dst, CHUNK), :]
        return pltpu.make_async_copy(v, h, sem) if to_hbm else pltpu.make_async_copy(h, v, sem)

    def each(fn):
        for e in range(n_experts):
            def body(k, carry, e=e):
                fn(copy(e, k))
                return carry
            lax.fori_loop(0, tab_ref[0, 0, TAB_N + e], body, 0)

    return (lambda: each(lambda c: c.start())), (lambda: each(lambda c: c.wait()))


def _dispatch_kernel(tab_ref, meta_ref, h_ref, hs_hbm, sorted_ref, sem, *, cfg):
    ne = cfg.n_groups * cfg.epg
    lr = sorted_ref.shape[0]
    mt = meta_ref[...].T
    lp1 = mt[META_LP:META_LP + 1, :]
    lp2 = mt[META_LP + 1:META_LP + 2, :]
    row = lax.broadcasted_iota(jnp.int32, (lr, cfg.tb), 0).astype(F32)
    perm = jnp.where((row == lp1) | (row == lp2), 1.0, 0.0).astype(BF16)
    sorted_ref[...] = _dot(perm, h_ref[...].astype(BF16))
    start, wait = _chunk_copies(hs_hbm, tab_ref, sorted_ref, sem.at[0], ne, to_hbm=True)
    start()
    wait()


def dispatch_rows(cfg, h, meta, tab, n_blocks, nblk):
    tb = cfg.tb
    return pl.pallas_call(
        functools.partial(_dispatch_kernel, cfg=cfg),
        grid=(n_blocks,),
        in_specs=[pl.BlockSpec((1, 1, 128), lambda j: (j, 0, 0), memory_space=pltpu.SMEM),
                  _row_spec(cfg, 128), _row_spec(cfg, cfg.d)],
        out_specs=pl.BlockSpec(memory_space=pl.ANY),
        out_shape=jax.ShapeDtypeStruct((nblk * tb, cfg.d), F32),
        scratch_shapes=[pltpu.VMEM((_local_rows(cfg), cfg.d), F32), pltpu.SemaphoreType.DMA((1,))],
        compiler_params=_cparams(("arbitrary",)),
        name="dispatch_rows",
    )(tab, meta, h)


def _expert_kernel(be_ref, valid_ref, nu_ref, h_ref, wg_ref, wu_ref, wd_ref, y_ref):
    del be_ref, nu_ref
    valid = valid_ref[pl.program_id(0)]

    @pl.when(valid > 0)
    def _():
        row = lax.broadcasted_iota(jnp.int32, h_ref.shape, 0)
        hb = jnp.where(row < valid, h_ref[...], 0.0).astype(BF16)
        a = _dot(hb, wg_ref[0].astype(BF16))
        b = _dot(hb, wu_ref[0].astype(BF16))
        hid = (_silu(a) * b).astype(BF16)
        y_ref[...] = _dot(hid, wd_ref[0].astype(BF16))


def expert_mlp(cfg, h_sorted, block_expert, block_valid, n_used, w_gate, w_up, w_down, nblk):
    tb = cfg.tb
    de = cfg.d_exp

    def rows(i, be, bv, nu):
        return (jnp.minimum(i, nu[0] - 1), 0)

    grid_spec = pltpu.PrefetchScalarGridSpec(
        num_scalar_prefetch=3,
        grid=(nblk,),
        in_specs=[pl.BlockSpec((tb, cfg.d), rows),
                  pl.BlockSpec((1, cfg.d, de), lambda i, be, bv, nu: (be[i], 0, 0)),
                  pl.BlockSpec((1, cfg.d, de), lambda i, be, bv, nu: (be[i], 0, 0)),
                  pl.BlockSpec((1, de, cfg.d), lambda i, be, bv, nu: (be[i], 0, 0))],
        out_specs=pl.BlockSpec((tb, cfg.d), rows),
    )
    return pl.pallas_call(
        _expert_kernel,
        grid_spec=grid_spec,
        out_shape=jax.ShapeDtypeStruct((nblk * tb, cfg.d), F32),
        compiler_params=_cparams(("arbitrary",)),
        name="expert_mlp",
    )(block_expert, block_valid, n_used, h_sorted, w_gate, w_up, w_down)


def _combine_kernel(tab_ref, tab_next_ref, x_ref, meta_ref, gate_ref, y_hbm, o_ref, ybuf, sem, *, cfg, nblk):
    ne = cfg.n_groups * cfg.epg
    i = pl.program_id(0)
    slot = i % 2
    start_cur, wait_cur = _chunk_copies(y_hbm, tab_ref, ybuf.at[slot], sem.at[slot], ne, to_hbm=False)
    start_next, _ = _chunk_copies(y_hbm, tab_next_ref, ybuf.at[1 - slot], sem.at[1 - slot], ne, to_hbm=False)

    @pl.when(i == 0)
    def _():
        start_cur()

    @pl.when(i + 1 < nblk)
    def _():
        start_next()

    wait_cur()
    lr = ybuf.shape[1]
    row = lax.broadcasted_iota(jnp.int32, (lr, 1), 0)
    y = jnp.where(row < tab_ref[0, 0, TAB_TOTAL], ybuf[slot], 0.0).astype(BF16)
    meta = meta_ref[...]
    lane = lax.broadcasted_iota(jnp.int32, (cfg.tb, lr), 1).astype(F32)
    pick1 = jnp.where(lane == meta[:, META_LP:META_LP + 1], 1.0, 0.0).astype(BF16)
    pick2 = jnp.where(lane == meta[:, META_LP + 1:META_LP + 2], 1.0, 0.0).astype(BF16)
    w1 = meta[:, META_W:META_W + 1]
    w2 = meta[:, META_W + 1:META_W + 2]
    o_ref[...] = x_ref[...] + gate_ref[0] * (w1 * _dot(pick1, y) + w2 * _dot(pick2, y))


def combine_residual(cfg, x, y_sorted, tab, meta, mod, gate_chunk, n_blocks):
    tb = cfg.tb
    return pl.pallas_call(
        functools.partial(_combine_kernel, cfg=cfg, nblk=n_blocks),
        grid=(n_blocks,),
        in_specs=[pl.BlockSpec((1, 1, 128), lambda j: (j, 0, 0), memory_space=pltpu.SMEM),
                  pl.BlockSpec((1, 1, 128), lambda j: (jnp.minimum(j + 1, n_blocks - 1), 0, 0),
                               memory_space=pltpu.SMEM),
                  _row_spec(cfg, cfg.d), _row_spec(cfg, 128), _mod_spec(cfg, gate_chunk),
                  pl.BlockSpec(memory_space=pl.ANY)],
        out_specs=_row_spec(cfg, cfg.d),
        out_shape=jax.ShapeDtypeStruct((n_blocks * tb, cfg.d), F32),
        scratch_shapes=[pltpu.VMEM((2, _local_rows(cfg), cfg.d), F32), pltpu.SemaphoreType.DMA((2,))],
        compiler_params=_cparams(("arbitrary",)),
        name="combine_residual",
    )(tab, tab, x, meta, mod, y_sorted)


def _final_norm_kernel(x_ref, g_ref, o_ref):
    x = x_ref[...]
    o_ref[...] = x * lax.rsqrt(jnp.mean(x * x, axis=-1, keepdims=True) + EPS) * g_ref[...]


def final_norm(cfg, x, g):
    return pl.pallas_call(
        _final_norm_kernel,
        grid=(cfg.nbl,),
        in_specs=[_row_spec(cfg, cfg.d), _full_spec((1, cfg.d))],
        out_specs=_row_spec(cfg, cfg.d),
        out_shape=jax.ShapeDtypeStruct((cfg.nlat, cfg.d), F32),
        compiler_params=_cparams(("arbitrary",)),
        name="final_norm",
    )(x, g)


def _rope_tables(cfg):
    hd = cfg.head_dim
    pairs = hd // 4
    rows = cfg.seq // cfg.grid_w
    row = jnp.repeat(jnp.arange(rows), cfg.grid_w).astype(F32)
    col = jnp.tile(jnp.arange(cfg.grid_w), rows).astype(F32)
    inv_freq = 10000.0 ** (-jnp.arange(pairs, dtype=F32) / pairs)
    ar, ac = row[:, None] * inv_freq, col[:, None] * inv_freq
    cos = jnp.concatenate([jnp.cos(ar), jnp.cos(ar), jnp.cos(ac), jnp.cos(ac)], axis=-1)
    sin = jnp.concatenate([-jnp.sin(ar), jnp.sin(ar), -jnp.sin(ac), jnp.sin(ac)], axis=-1)
    cos = jnp.concatenate([cos, jnp.ones((cfg.tb, hd), F32)], axis=0)
    sin = jnp.concatenate([sin, jnp.zeros((cfg.tb, hd), F32)], axis=0)
    return cos, sin


def _moe_layer(cfg, x, norm_g, mod, w_group, b_group, w_router, b_router, w_gate, w_up, w_down, layer, n_blocks):
    ne = cfg.n_groups * cfg.epg
    w_route = jnp.pad(jnp.concatenate([w_router, w_group], axis=1), ((0, 0), (0, 128 - ne - cfg.n_groups)))
    b_route = jnp.pad(jnp.concatenate([b_router, b_group]), (0, 128 - ne - cfg.n_groups)).reshape(1, 128)
    h, meta, tabs, totals = route(cfg, x, norm_g, mod, w_route, b_route, n_blocks)

    tb = cfg.tb
    nblk = -(-(2 * n_blocks * tb + (CHUNK - 1) * ne * n_blocks) // tb) + ne
    used = CHUNK * totals[0, :ne].astype(jnp.int32)
    padded = ((used + tb - 1) // tb) * tb
    seg_end = jnp.cumsum(padded)
    seg_start = seg_end - padded
    t = tabs.astype(jnp.int32)
    n_chunks, src0, before = t[:, 0, :ne], t[:, 1, :ne], t[:, 2, :ne]
    dst0 = seg_start[None, :] + CHUNK * before
    local_total = CHUNK * jnp.sum(n_chunks, axis=1, keepdims=True)
    tab = jnp.concatenate([n_chunks, src0, dst0, local_total,
                           jnp.zeros((n_blocks, 128 - 3 * ne - 1), jnp.int32)], axis=1).reshape(n_blocks, 1, 128)
    blk_row = jnp.arange(nblk, dtype=jnp.int32) * tb
    block_expert = jnp.minimum(jnp.sum(seg_end[None, :] <= blk_row[:, None], axis=-1), ne - 1).astype(jnp.int32)
    block_valid = jnp.clip(seg_start[block_expert] + used[block_expert] - blk_row, 0, tb).astype(jnp.int32)
    n_used = (seg_end[ne - 1:ne] // tb).astype(jnp.int32)

    h_sorted = dispatch_rows(cfg, h, meta, tab, n_blocks, nblk)
    y = expert_mlp(cfg, h_sorted, block_expert + layer * ne, block_valid, n_used,
                   w_gate.reshape(-1, cfg.d, cfg.d_exp), w_up.reshape(-1, cfg.d, cfg.d_exp),
                   w_down.reshape(-1, cfg.d_exp, cfg.d), nblk)
    return combine_residual(cfg, x, y, tab, meta, mod, 5, n_blocks)


def _forward(cfg, x, c, ctx, c_ctx, w_mod, b_mod, norm_mix_g, norm_ffn_g,
             attn_w_q, attn_w_kv, attn_q_gain, attn_k_gain, attn_w_o,
             conv_w_pw1, conv_b_pw1, conv_w_dw, conv_b_dw, conv_ln_g, conv_ln_b, conv_w_pw2, conv_b_pw2,
             hy_w_in, hy_b_in, hy_w_short, hy_b_short, hy_f_w1, hy_f_b1, hy_f_freq1, hy_f_w2, hy_f_b2,
             hy_f_freq2, hy_f_w3, hy_long_bias, hy_w_out, hy_b_out,
             moe_w_group, moe_b_group, moe_w_router, moe_b_router, moe_w_gate, moe_w_up, moe_w_down,
             final_norm_g):
    d = cfg.d
    depth = w_mod.shape[0]
    xs = jnp.concatenate([x.reshape(-1, d), ctx.reshape(-1, d)], axis=0)
    c_all = jnp.concatenate([c, c_ctx[None, :], jnp.zeros((MOD_ROWS - cfg.batch - 1, d), F32)], axis=0)
    mods = modulation(c_all, w_mod, b_mod)
    rope_cos, rope_sin = _rope_tables(cfg)
    zero_bias = jnp.zeros((1, d), F32)

    for i in range(depth):
        kind, slot = i % 3, i // 3
        last = i == depth - 1
        n_blocks = cfg.nbl if last else cfg.nb
        mod = mods[i].reshape(MOD_ROWS * 6, 1, d)
        g_mix = norm_mix_g[i].reshape(1, d)
        if kind == 0:
            w_qkv = jnp.concatenate([attn_w_q[slot], attn_w_kv[slot]], axis=1).astype(BF16)
            qkv = qkv_project(cfg, xs, g_mix, mod, w_qkv, rope_cos, rope_sin,
                              attn_q_gain[slot].reshape(1, -1), attn_k_gain[slot].reshape(1, -1))
            o = attention(cfg, qkv)
            xs = project_residual(cfg, o, attn_w_o[slot].astype(BF16), zero_bias, xs, mod, 2, n_blocks)
        elif kind == 1:
            u = glu_project(cfg, xs, g_mix, mod, conv_w_pw1[slot].astype(BF16), conv_b_pw1[slot].reshape(1, -1))
            xs = conv_module(cfg, u, conv_w_dw[slot], conv_b_dw[slot].reshape(1, d), conv_ln_g[slot].reshape(1, d),
                             conv_ln_b[slot].reshape(1, d), conv_w_pw2[slot].astype(BF16),
                             conv_b_pw2[slot].reshape(1, d), xs, mod)
        else:
            p = in_project(cfg, xs, g_mix, mod, hy_w_in[slot].astype(BF16), hy_b_in[slot].reshape(1, -1))
            u = short_conv(cfg, p, hy_w_short[slot], hy_b_short[slot].reshape(1, -1))
            fp = (hy_f_w1[slot], hy_f_b1[slot], hy_f_freq1[slot], hy_f_w2[slot], hy_f_b2[slot],
                  hy_f_freq2[slot], hy_f_w3[slot])
            z2 = hyena_long_convs(cfg, u, fp, hy_long_bias[slot])
            xs = project_residual(cfg, z2, hy_w_out[slot].astype(BF16), hy_b_out[slot].reshape(1, d), xs, mod, 2,
                                  n_blocks, fold=cfg.fft_n2)
        xs = _moe_layer(cfg, xs, norm_ffn_g[i].reshape(1, d), mod, moe_w_group[i], moe_b_group[i],
                        moe_w_router[i], moe_b_router[i], moe_w_gate, moe_w_up, moe_w_down, i, n_blocks)

    out = final_norm(cfg, xs, final_norm_g.reshape(1, d))
    return out.reshape(cfg.batch, cfg.seq, d)


def kernel(x, c, ctx, c_ctx, w_mod, b_mod, norm_mix_g, norm_ffn_g, attn_w_q, attn_w_kv, attn_q_gain, attn_k_gain, attn_w_o, conv_w_pw1, conv_b_pw1, conv_w_dw, conv_b_dw, conv_ln_g, conv_ln_b, conv_w_pw2, conv_b_pw2, hy_w_in, hy_b_in, hy_w_short, hy_b_short, hy_f_w1, hy_f_b1, hy_f_freq1, hy_f_w2, hy_f_b2, hy_f_freq2, hy_f_w3, hy_long_bias, hy_w_out, hy_b_out, moe_w_group, moe_b_group, moe_w_router, moe_b_router, moe_w_gate, moe_w_up, moe_w_down, final_norm_g):
    return _forward(CFG, x, c, ctx, c_ctx, w_mod, b_mod, norm_mix_g, norm_ffn_g, attn_w_q, attn_w_kv, attn_q_gain, attn_k_gain, attn_w_o, conv_w_pw1, conv_b_pw1, conv_w_dw, conv_b_dw, conv_ln_g, conv_ln_b, conv_w_pw2, conv_b_pw2, hy_w_in, hy_b_in, hy_w_short, hy_b_short, hy_f_w1, hy_f_b1, hy_f_freq1, hy_f_w2, hy_f_b2, hy_f_freq2, hy_f_w3, hy_long_bias, hy_w_out, hy_b_out, moe_w_group, moe_b_group, moe_w_router, moe_b_router, moe_w_gate, moe_w_up, moe_w_down, final_norm_g)
```

```python
import functools
import math
from typing import NamedTuple

import jax
import jax.numpy as jnp
from jax import lax
from jax.experimental import pallas as pl
from jax.experimental.pallas import tpu as pltpu

F32 = jnp.float32
BF16 = jnp.bfloat16
EPS = 1e-6
V7X_VMEM_LIMIT_BYTES = 56 * 1024 * 1024


class Cfg(NamedTuple):
    batch: int
    seq: int
    ctx: int
    d: int
    grid_w: int
    n_heads: int
    n_kv: int
    head_dim: int
    conv_w: int
    n_groups: int
    epg: int
    d_exp: int
    tb: int
    fft_n1: int
    fft_n2: int
    kv_chunk: int

    @property
    def nlat(self):
        return self.batch * self.seq

    @property
    def nt(self):
        return self.batch * (self.seq + self.ctx)

    @property
    def nbl(self):
        return self.nlat // self.tb

    @property
    def nbc(self):
        return self.batch * self.ctx // self.tb

    @property
    def nb(self):
        return self.nbl + self.nbc

    @property
    def spb(self):
        return self.seq // self.tb

    @property
    def cpb(self):
        return self.ctx // self.tb


CFG = Cfg(batch=8, seq=4096, ctx=256, d=1024, grid_w=64, n_heads=8, n_kv=2, head_dim=128, conv_w=31,
          n_groups=4, epg=8, d_exp=256, tb=256, fft_n1=256, fft_n2=32, kv_chunk=1024)

MOD_ROWS = 16


def _cparams(sem):
    return pltpu.CompilerParams(dimension_semantics=sem, vmem_limit_bytes=V7X_VMEM_LIMIT_BYTES)


def _split_bf16(a):
    hi = a.astype(BF16)
    lo = (a - hi.astype(F32)).astype(BF16)
    return hi, lo


def _dot(a, b):
    return jnp.dot(a, b, preferred_element_type=F32)


def _dot3(a, b):
    ah, al = _split_bf16(a)
    bh, bl = _split_bf16(b)
    return _dot(ah, bh) + (_dot(ah, bl) + _dot(al, bh))


def _modnorm(x, g, sh, sc):
    ms = jnp.mean(x * x, axis=-1, keepdims=True)
    y = x * lax.rsqrt(ms + EPS) * g
    return y * (1.0 + sc) + sh


def _silu(x):
    return x * jax.nn.sigmoid(x)


def _mod_row(cfg, j):
    return jnp.where(j < cfg.nbl, j // cfg.spb, cfg.batch)


def _mod_spec(cfg, chunk):
    return pl.BlockSpec((1, 1, cfg.d), lambda j: (_mod_row(cfg, j) * 6 + chunk, 0, 0))


def _row_spec(cfg, width):
    return pl.BlockSpec((cfg.tb, width), lambda j: (j, 0))


def _full_spec(shape):
    n = len(shape)
    return pl.BlockSpec(shape, lambda *_: (0,) * n)


def _mod_kernel(c_ref, w_ref, b_ref, o_ref):
    o_ref[0] = _dot3(_silu(c_ref[...]), w_ref[0]) + b_ref[0]


def modulation(c_all, w_mod, b_mod):
    depth, d, n6 = w_mod.shape
    bn = n6 // 4
    return pl.pallas_call(
        _mod_kernel,
        grid=(depth, n6 // bn),
        in_specs=[_full_spec((MOD_ROWS, d)),
                  pl.BlockSpec((1, d, bn), lambda i, n: (i, 0, n)),
                  pl.BlockSpec((1, 1, bn), lambda i, n: (i, 0, n))],
        out_specs=pl.BlockSpec((1, MOD_ROWS, bn), lambda i, n: (i, 0, n)),
        out_shape=jax.ShapeDtypeStruct((depth, MOD_ROWS, n6), F32),
        compiler_params=_cparams(("arbitrary", "arbitrary")),
        name="modulation",
    )(c_all, w_mod, b_mod.reshape(depth, 1, n6))


def _qkv_kernel(x_ref, g_ref, sh_ref, sc_ref, w_ref, cs_ref, sn_ref, qg_ref, kg_ref, o_ref, *, cfg):
    hd = cfg.head_dim
    h = _modnorm(x_ref[...], g_ref[...], sh_ref[0], sc_ref[0]).astype(BF16)
    y = _dot(h, w_ref[...])
    cs = cs_ref[...]
    sn = sn_ref[...]
    lane = lax.broadcasted_iota(jnp.int32, cs.shape, 1)
    first_half = (lane % (hd // 2)) < (hd // 4)

    def norm_rope(v, gain, scale):
        ms = jnp.mean(v * v, axis=-1, keepdims=True)
        v = v * lax.rsqrt(ms + EPS) * gain
        partner = jnp.where(first_half, pltpu.roll(v, hd - hd // 4, 1), pltpu.roll(v, hd // 4, 1))
        return (v * cs + partner * sn) * scale

    nq = cfg.n_heads
    for hh in range(nq):
        sl = slice(hh * hd, (hh + 1) * hd)
        o_ref[:, sl] = norm_rope(y[:, sl], qg_ref[...], hd ** -0.5 * math.log2(math.e)).astype(o_ref.dtype)
    for hh in range(cfg.n_kv):
        sl = slice((nq + hh) * hd, (nq + hh + 1) * hd)
        o_ref[:, sl] = norm_rope(y[:, sl], kg_ref[...], 1.0).astype(o_ref.dtype)
    v0 = (nq + cfg.n_kv) * hd
    o_ref[:, v0:] = y[:, v0:].astype(o_ref.dtype)


def qkv_project(cfg, x, norm_g, mod, w_qkv, rope_cos, rope_sin, q_gain, k_gain):
    nqkv = w_qkv.shape[1]
    hd = cfg.head_dim

    def rope_idx(j):
        return (jnp.where(j < cfg.nbl, j % cfg.spb, cfg.spb), 0)

    return pl.pallas_call(
        functools.partial(_qkv_kernel, cfg=cfg),
        grid=(cfg.nb,),
        in_specs=[_row_spec(cfg, cfg.d), _full_spec((1, cfg.d)), _mod_spec(cfg, 0), _mod_spec(cfg, 1),
                  _full_spec((cfg.d, nqkv)),
                  pl.BlockSpec((cfg.tb, hd), rope_idx), pl.BlockSpec((cfg.tb, hd), rope_idx),
                  _full_spec((1, hd)), _full_spec((1, hd))],
        out_specs=_row_spec(cfg, nqkv),
        out_shape=jax.ShapeDtypeStruct((cfg.nt, nqkv), BF16),
        compiler_params=_cparams(("arbitrary",)),
        name="qkv_project",
    )(x, norm_g, mod, mod, w_qkv, rope_cos, rope_sin, q_gain, k_gain)


def _transpose_bf16(x):
    return x.astype(F32).T.astype(BF16)


def _attn_kernel(q_ref, kc_ref, vc_ref, kl_ref, vl_ref, o_ref, vtc_ref, vtl_ref, *, cfg, n_lat_chunks):
    hd = cfg.head_dim
    group = cfg.n_heads // cfg.n_kv
    tq = cfg.tb
    ch = cfg.kv_chunk

    @pl.when(pl.program_id(2) == 0)
    def _():
        vtc_ref[...] = _transpose_bf16(vc_ref[...])
        for c in range(n_lat_chunks):
            vtl_ref[c] = _transpose_bf16(vl_ref[c * ch:(c + 1) * ch, :])

    qt = jnp.concatenate([_transpose_bf16(q_ref[:, h * hd:(h + 1) * hd]) for h in range(group)], axis=1)

    s = _dot(kc_ref[...], qt)
    m = jnp.max(s, axis=0, keepdims=True)
    p = jnp.exp2(s - m)
    l = jnp.sum(p, axis=0, keepdims=True)
    acc = _dot(vtc_ref[...], p.astype(BF16))

    def body(c, carry):
        m, l, acc = carry
        start = pl.multiple_of(c * ch, ch)
        s = _dot(kl_ref[pl.ds(start, ch), :], qt)
        m_new = jnp.maximum(m, jnp.max(s, axis=0, keepdims=True))
        alpha = jnp.exp2(m - m_new)
        p = jnp.exp2(s - m_new)
        l = alpha * l + jnp.sum(p, axis=0, keepdims=True)
        acc = alpha * acc + _dot(vtl_ref[c], p.astype(BF16))
        return m_new, l, acc

    n = jnp.where(pl.program_id(2) < cfg.spb, n_lat_chunks, 0)
    m, l, acc = lax.fori_loop(0, n, body, (m, l, acc))
    o = acc / l
    for h in range(group):
        o_ref[:, h * hd:(h + 1) * hd] = o[:, h * tq:(h + 1) * tq].T.astype(o_ref.dtype)


def attention(cfg, qkv):
    hd = cfg.head_dim
    group = cfg.n_heads // cfg.n_kv
    gw = group * hd
    kcol = cfg.n_heads
    vcol = cfg.n_heads + cfg.n_kv
    assert cfg.cpb == 1

    def qrow(b, qb):
        return jnp.where(qb < cfg.spb, b * cfg.spb + qb, cfg.nbl + b)

    return pl.pallas_call(
        functools.partial(_attn_kernel, cfg=cfg, n_lat_chunks=cfg.seq // cfg.kv_chunk),
        grid=(cfg.batch, cfg.n_kv, cfg.spb + 1),
        in_specs=[pl.BlockSpec((cfg.tb, gw), lambda b, k, qb: (qrow(b, qb), k)),
                  pl.BlockSpec((cfg.ctx, hd), lambda b, k, qb: (cfg.nlat // cfg.ctx + b, kcol + k)),
                  pl.BlockSpec((cfg.ctx, hd), lambda b, k, qb: (cfg.nlat // cfg.ctx + b, vcol + k)),
                  pl.BlockSpec((cfg.seq, hd), lambda b, k, qb: (b, kcol + k)),
                  pl.BlockSpec((cfg.seq, hd), lambda b, k, qb: (b, vcol + k))],
        out_specs=pl.BlockSpec((cfg.tb, gw), lambda b, k, qb: (qrow(b, qb), k)),
        out_shape=jax.ShapeDtypeStruct((cfg.nt, cfg.n_heads * hd), BF16),
        scratch_shapes=[pltpu.VMEM((hd, cfg.ctx), BF16),
                        pltpu.VMEM((cfg.seq // cfg.kv_chunk, hd, cfg.kv_chunk), BF16)],
        compiler_params=_cparams(("arbitrary", "arbitrary", "arbitrary")),
        name="attention",
    )(qkv, qkv, qkv, qkv, qkv)


def _proj_res_kernel(a_ref, w_ref, b_ref, x_ref, gate_ref, o_ref):
    y = _dot(a_ref[...].astype(BF16), w_ref[...]) + b_ref[...]
    o_ref[...] = x_ref[...] + gate_ref[0] * y


def _proj_res_folded_kernel(a_ref, w_ref, b_ref, x_ref, gate_ref, o_ref, nat_ref, *, fold):
    _unfold_to_rows(a_ref, nat_ref, fold)
    y = _dot(_load_tiles(nat_ref).astype(BF16), w_ref[...]) + b_ref[...]
    o_ref[...] = x_ref[...] + gate_ref[0] * y


def project_residual(cfg, a, w, bias, x, mod, gate_chunk, n_blocks, fold=1):
    if fold == 1:
        body, a_spec, scratch = _proj_res_kernel, _row_spec(cfg, a.shape[1]), []
    else:
        body = functools.partial(_proj_res_folded_kernel, fold=fold)
        a_spec = pl.BlockSpec((cfg.tb // fold, a.shape[1]), lambda j: (j, 0))
        scratch = [_lane_tiles(cfg.tb, a.shape[1] // fold)]
    return pl.pallas_call(
        body,
        grid=(n_blocks,),
        in_specs=[a_spec, _full_spec(w.shape), _full_spec((1, cfg.d)),
                  _row_spec(cfg, cfg.d), _mod_spec(cfg, gate_chunk)],
        out_specs=_row_spec(cfg, cfg.d),
        out_shape=jax.ShapeDtypeStruct((n_blocks * cfg.tb, cfg.d), F32),
        scratch_shapes=scratch,
        compiler_params=_cparams(("arbitrary",)),
        name="project_residual",
    )(a, w, bias, x, mod)


def _glu_kernel(x_ref, g_ref, sh_ref, sc_ref, w_ref, b_ref, o_ref, *, d):
    h = _modnorm(x_ref[...], g_ref[...], sh_ref[0], sc_ref[0]).astype(BF16)
    y = _dot(h, w_ref[...]) + b_ref[...]
    o_ref[...] = (y[:, :d] * jax.nn.sigmoid(y[:, d:])).astype(o_ref.dtype)


def glu_project(cfg, x, norm_g, mod, w, bias):
    return pl.pallas_call(
        functools.partial(_glu_kernel, d=cfg.d),
        grid=(cfg.nb,),
        in_specs=[_row_spec(cfg, cfg.d), _full_spec((1, cfg.d)), _mod_spec(cfg, 0), _mod_spec(cfg, 1),
                  _full_spec(w.shape), _full_spec((1, w.shape[1]))],
        out_specs=_row_spec(cfg, cfg.d),
        out_shape=jax.ShapeDtypeStruct((cfg.nt, cfg.d), BF16),
        compiler_params=_cparams(("arbitrary",)),
        name="glu_project",
    )(x, norm_g, mod, mod, w, bias)


def _seq_edges(cfg, j):
    lat = j < cfg.nbl
    first = jnp.where(lat, j % cfg.spb == 0, (j - cfg.nbl) % cfg.cpb == 0)
    last = jnp.where(lat, j % cfg.spb == cfg.spb - 1, (j - cfg.nbl) % cfg.cpb == cfg.cpb - 1)
    return first, last


def _halo_specs(cfg, width, halo):
    per = cfg.tb // halo
    last_blk = cfg.nt // halo - 1
    prev = pl.BlockSpec((halo, width), lambda j: (jnp.maximum(j * per - 1, 0), 0))
    nxt = pl.BlockSpec((halo, width), lambda j: (jnp.minimum((j + 1) * per, last_blk), 0))
    return prev, nxt


def _fill_ext(cfg, ext_ref, prev_ref, cur_ref, next_ref, halo):
    first, last = _seq_edges(cfg, pl.program_id(0))
    tb = cfg.tb
    ext_ref[0:halo, :] = jnp.where(first, 0.0, prev_ref[...].astype(F32))
    ext_ref[halo:halo + tb, :] = cur_ref[...].astype(F32)
    ext_ref[halo + tb:2 * halo + tb, :] = jnp.where(last, 0.0, next_ref[...].astype(F32))


def _conv_kernel(up_ref, uc_ref, un_ref, wdw_ref, bdw_ref, lng_ref, lnb_ref, w2_ref, b2_ref, x_ref, gate_ref,
                 o_ref, ext_ref, *, cfg, halo):
    tb = cfg.tb
    _fill_ext(cfg, ext_ref, up_ref, uc_ref, un_ref, halo)
    half = (cfg.conv_w - 1) // 2
    sub = 8
    acc = jnp.zeros((tb, cfg.d), F32)
    for r in range(sub):
        part = None
        for k in range(cfg.conv_w):
            off = halo - half + k
            if off % sub != r:
                continue
            base = off - r
            term = wdw_ref[k:k + 1, :] * ext_ref[base:base + tb + sub, :]
            part = term if part is None else part + term
        if part is not None:
            acc = acc + part[r:r + tb]
    u = acc + bdw_ref[...]
    mu = jnp.mean(u, axis=-1, keepdims=True)
    uc = u - mu
    u = uc * lax.rsqrt(jnp.mean(uc * uc, axis=-1, keepdims=True) + EPS) * lng_ref[...] + lnb_ref[...]
    u = _silu(u).astype(BF16)
    y = _dot(u, w2_ref[...]) + b2_ref[...]
    o_ref[...] = x_ref[...] + gate_ref[0] * y


def conv_module(cfg, u, w_dw, b_dw, ln_g, ln_b, w2, b2, x, mod):
    halo = 16
    prev, nxt = _halo_specs(cfg, cfg.d, halo)
    return pl.pallas_call(
        functools.partial(_conv_kernel, cfg=cfg, halo=halo),
        grid=(cfg.nb,),
        in_specs=[prev, _row_spec(cfg, cfg.d), nxt, _full_spec(w_dw.shape), _full_spec((1, cfg.d)),
                  _full_spec((1, cfg.d)), _full_spec((1, cfg.d)), _full_spec(w2.shape), _full_spec((1, cfg.d)),
                  _row_spec(cfg, cfg.d), _mod_spec(cfg, 2)],
        out_specs=_row_spec(cfg, cfg.d),
        out_shape=jax.ShapeDtypeStruct((cfg.nt, cfg.d), F32),
        scratch_shapes=[pltpu.VMEM((cfg.tb + 2 * halo, cfg.d), F32)],
        compiler_params=_cparams(("arbitrary",)),
        name="conv_module",
    )(u, u, u, w_dw, b_dw, ln_g, ln_b, w2, b2, x, mod)


def _inproj_kernel(x_ref, g_ref, sh_ref, sc_ref, w_ref, b_ref, o_ref):
    h = _modnorm(x_ref[...], g_ref[...], sh_ref[0], sc_ref[0]).astype(BF16)
    o_ref[...] = (_dot(h, w_ref[...]) + b_ref[...]).astype(o_ref.dtype)


def in_project(cfg, x, norm_g, mod, w, bias):
    n = w.shape[1]
    return pl.pallas_call(
        _inproj_kernel,
        grid=(cfg.nb,),
        in_specs=[_row_spec(cfg, cfg.d), _full_spec((1, cfg.d)), _mod_spec(cfg, 0), _mod_spec(cfg, 1),
                  _full_spec(w.shape), _full_spec((1, n))],
        out_specs=_row_spec(cfg, n),
        out_shape=jax.ShapeDtypeStruct((cfg.nt, n), BF16),
        compiler_params=_cparams(("arbitrary",)),
        name="in_project",
    )(x, norm_g, mod, mod, w, bias)


LANES = 128


def _lane_tiles(rows, width):
    return pltpu.VMEM((width // LANES, rows, LANES), F32)


def _store_tiles(tiles_ref, value):
    for c in range(tiles_ref.shape[0]):
        tiles_ref[c] = value[:, c * LANES:(c + 1) * LANES]


def _load_tiles(tiles_ref, c0=0, c1=None):
    c1 = tiles_ref.shape[0] if c1 is None else c1
    return jnp.concatenate([tiles_ref[c] for c in range(c0, c1)], axis=1)


def _fold_rows(tiles_ref, o_ref, fold):
    nt, rows, _ = tiles_ref.shape
    w = nt * LANES
    for s in range(fold):
        for c in range(nt):
            o_ref[:, s * w + c * LANES:s * w + (c + 1) * LANES] = tiles_ref[c, pl.ds(s, rows // fold, stride=fold), :]


def _unfold_to_rows(f_ref, tiles_ref, fold):
    nt, rows, _ = tiles_ref.shape
    w = nt * LANES
    for s in range(fold):
        for c in range(nt):
            tiles_ref[c, pl.ds(s, rows // fold, stride=fold), :] = f_ref[:, s * w + c * LANES:s * w + (c + 1) * LANES]


def _short_conv_kernel(pp_ref, pc_ref, pn_ref, w_ref, b_ref, o_ref, ext_ref, nat_ref, *, cfg, halo, width):
    tb = cfg.tb
    _fill_ext(cfg, ext_ref, pp_ref, pc_ref, pn_ref, halo)
    half = (width - 1) // 2
    acc = b_ref[...]
    for k in range(width):
        off = halo - half + k
        acc = acc + w_ref[k:k + 1, :] * ext_ref[off:off + tb, :]
    _store_tiles(nat_ref, acc)
    _fold_rows(nat_ref, o_ref, cfg.fft_n2)


def short_conv(cfg, p, w, bias):
    halo = 16
    n = p.shape[1]
    n2 = cfg.fft_n2
    per = cfg.tb // halo
    last_blk = cfg.nt // halo - 1
    return pl.pallas_call(
        functools.partial(_short_conv_kernel, cfg=cfg, halo=halo, width=w.shape[0]),
        grid=(cfg.nb,),
        in_specs=[pl.BlockSpec((halo, n), lambda j: (jnp.maximum(j * per - 1, 0), 0)),
                  pl.BlockSpec((cfg.tb, n), lambda j: (j, 0)),
                  pl.BlockSpec((halo, n), lambda j: (jnp.minimum((j + 1) * per, last_blk), 0)),
                  _full_spec(w.shape), _full_spec((1, n))],
        out_specs=pl.BlockSpec((cfg.tb // n2, n2 * n), lambda j: (j, 0)),
        out_shape=jax.ShapeDtypeStruct((cfg.nt // n2, n2 * n), F32),
        scratch_shapes=[pltpu.VMEM((cfg.tb + 2 * halo, n), F32), _lane_tiles(cfg.tb, n)],
        compiler_params=_cparams(("arbitrary",)),
        name="short_conv",
    )(p, p, p, w, bias)


def _filter_mlp_kernel(feat_ref, w1_ref, b1_ref, f1_ref, w2_ref, b2_ref, f2_ref, w3_ref, delta_ref,
                       h_ref, s_ref, *, length, row_mul, step_mul):
    i = pl.program_id(0)
    z = jnp.sin(f1_ref[...] * (_dot3(feat_ref[0], w1_ref[...]) + b1_ref[...]))
    z = jnp.sin(f2_ref[...] * (_dot3(z, w2_ref[...]) + b2_ref[...]))
    h = _dot3(z, w3_ref[...])
    pos = (i * step_mul + row_mul * lax.broadcasted_iota(jnp.int32, h.shape, 0)).astype(F32)
    t = pos / (length - 1)
    h = h * jnp.exp(-t * jnp.abs(delta_ref[...]))
    col = lax.broadcasted_iota(jnp.int32, h.shape, 1)
    h = jnp.where((pos == 0.0) & (col >= h.shape[1] // 2), 0.0, h)
    h_ref[...] = h
    part = jnp.sum(jnp.abs(h), axis=0, keepdims=True)

    @pl.when(i == 0)
    def _():
        s_ref[...] = part

    @pl.when(i > 0)
    def _():
        s_ref[...] = s_ref[...] + part


def filter_taps(length, fold, feats, w1, b1, f1, w2, b2, f2, w3, deltas4):
    nh = w1.shape[1]
    n = w3.shape[1]
    if fold == 1:
        rows = min(length, 256)
        steps = length // rows
        feats3 = feats.reshape(steps, rows, feats.shape[1])
        row_mul, step_mul = 1, rows
        out_spec = pl.BlockSpec((rows, n), lambda i: (i, 0))
        out_shape = (length, n)
    else:
        rows = length // fold
        steps = fold
        feats3 = feats.reshape(rows, fold, feats.shape[1]).transpose(1, 0, 2)
        row_mul, step_mul = fold, 1
        out_spec = pl.BlockSpec((rows, n), lambda i: (0, i))
        out_shape = (rows, fold * n)
    return pl.pallas_call(
        functools.partial(_filter_mlp_kernel, length=length, row_mul=row_mul, step_mul=step_mul),
        grid=(steps,),
        in_specs=[pl.BlockSpec((1, rows, feats.shape[1]), lambda i: (i, 0, 0)), _full_spec(w1.shape),
                  _full_spec((1, nh)), _full_spec((1, nh)), _full_spec(w2.shape), _full_spec((1, nh)),
                  _full_spec((1, nh)), _full_spec(w3.shape), _full_spec((1, n))],
        out_specs=[out_spec, _full_spec((1, n))],
        out_shape=[jax.ShapeDtypeStruct(out_shape, F32), jax.ShapeDtypeStruct((1, n), F32)],
        compiler_params=_cparams(("arbitrary",)),
        name="filter_taps",
    )(feats3, w1, b1, f1, w2, b2, f2, w3, deltas4)


def _fft_stage1_kernel(z_ref, m_ref, a_ref):
    a_ref[...] = _dot(m_ref[0], z_ref[...].astype(BF16)).astype(a_ref.dtype)


def fft_stage1(z_view, mats, n_pairs, n2, lane_stride, lane_off, width, out_dtype=BF16):
    rows_out, rows_in = mats.shape[1:]
    return pl.pallas_call(
        _fft_stage1_kernel,
        grid=(n_pairs, n2),
        in_specs=[pl.BlockSpec((rows_in, width), lambda p, s: (p, s * lane_stride + lane_off)),
                  pl.BlockSpec((1, rows_out, rows_in), lambda p, s: (s, 0, 0))],
        out_specs=pl.BlockSpec((rows_out, width), lambda p, s: (p, s)),
        out_shape=jax.ShapeDtypeStruct((n_pairs * rows_out, n2 * width), out_dtype),
        compiler_params=_cparams(("arbitrary", "arbitrary")),
        name="fft_stage1",
    )(z_view, mats)


def _unfold_rows(a_ref, n2, width, lane_stride, lane_off):
    return jnp.concatenate(
        [a_ref[:, (s * lane_stride + lane_off) * width:(s * lane_stride + lane_off + 1) * width] for s in range(n2)],
        axis=0)


def _fft_mid_kernel(a_ref, h_ref, bf_ref, bi_ref, o_ref, *, n2, width):
    g = a_ref.shape[0]
    x = _dot(bf_ref[...], _unfold_rows(a_ref, n2, width, 1, 0))
    half = x.shape[0] // 2
    xr, xi = x[:half], x[half:]
    hr, hi = h_ref[:half, :], h_ref[half:, :]
    y = jnp.concatenate([xr * hr - xi * hi, xr * hi + xi * hr], axis=0).astype(BF16)
    b = _dot(bi_ref[...], y).astype(o_ref.dtype)
    for s in range(n2):
        o_ref[:, s * width:(s + 1) * width] = b[s * g:(s + 1) * g]


def fft_mid(a, h, h_col, bd_fwd, bd_inv, n_pairs, n_groups, n2, width):
    rows = bd_fwd.shape[0]
    g = rows // n2
    return pl.pallas_call(
        functools.partial(_fft_mid_kernel, n2=n2, width=width),
        grid=(n_groups, n_pairs),
        in_specs=[pl.BlockSpec((g, n2 * width), lambda q, p: (p * n_groups + q, 0)),
                  pl.BlockSpec((rows, width), lambda q, p: (q, h_col)),
                  _full_spec(bd_fwd.shape), _full_spec(bd_inv.shape)],
        out_specs=pl.BlockSpec((g, n2 * width), lambda q, p: (p * n_groups + q, 0)),
        out_shape=jax.ShapeDtypeStruct(a.shape, BF16),
        compiler_params=_cparams(("arbitrary", "arbitrary")),
        name="fft_mid",
    )(a, h, bd_fwd, bd_inv)


def _fft_spec_kernel(a_ref, bf_ref, nrm_ref, h_ref, *, n2, width, n_orders):
    o = pl.program_id(1)
    for order in range(n_orders):
        @pl.when(o == order)
        def _():
            xf = _dot(bf_ref[...], _unfold_rows(a_ref, n2, width, 2 * n_orders, order))
            xb = _dot(bf_ref[...], _unfold_rows(a_ref, n2, width, 2 * n_orders, n_orders + order))
            half = xf.shape[0] // 2
            inv = 1.0 / nrm_ref[...]
            h_ref[:half, :] = (xf[:half] + xb[:half]) * inv
            h_ref[half:, :] = (xf[half:] - xb[half:]) * inv


def fft_filter_spectrum(a, bd_fwd, norm, n_groups, n2, width, n_orders):
    rows = bd_fwd.shape[0]
    g = rows // n2
    return pl.pallas_call(
        functools.partial(_fft_spec_kernel, n2=n2, width=width, n_orders=n_orders),
        grid=(n_groups, n_orders),
        in_specs=[pl.BlockSpec((g, n2 * 2 * n_orders * width), lambda q, o: (q, 0)),
                  _full_spec(bd_fwd.shape),
                  pl.BlockSpec((1, width), lambda q, o: (0, o))],
        out_specs=pl.BlockSpec((rows, width), lambda q, o: (q, o)),
        out_shape=jax.ShapeDtypeStruct((n_groups * rows, n_orders * width), F32),
        compiler_params=_cparams(("arbitrary", "arbitrary")),
        name="fft_filter_spectrum",
    )(a, bd_fwd, norm)


def _fft_inv_kernel(b_ref, m_ref, z_ref, gate_ref, bias_ref, o_ref):
    y = _dot(m_ref[0], b_ref[...])
    o_ref[...] = gate_ref[...] * (y + z_ref[...] * bias_ref[...])


def fft_inverse_gate(b, mats, z_view, z_stride, z_off, gate_view, g_stride, g_off, bias, n_pairs, n2, width,
                     out_rows):
    rows_out, rows_in = mats.shape[1:]
    return pl.pallas_call(
        _fft_inv_kernel,
        grid=(n_pairs, n2),
        in_specs=[pl.BlockSpec((rows_in, width), lambda p, s: (p, s)),
                  pl.BlockSpec((1, rows_out, rows_in), lambda p, s: (s, 0, 0)),
                  pl.BlockSpec((rows_out, width), lambda p, s: (p, s * z_stride + z_off)),
                  pl.BlockSpec((rows_out, width), lambda p, s: (p, s * g_stride + g_off)),
                  _full_spec((1, width))],
        out_specs=pl.BlockSpec((rows_out, width), lambda p, s: (p, s)),
        out_shape=jax.ShapeDtypeStruct((out_rows, n2 * width), F32),
        compiler_params=_cparams(("arbitrary", "arbitrary")),
        name="fft_inverse_gate",
    )(b, mats, z_view, gate_view, bias)


def _ctx_spec_kernel(hf_ref, hb_ref, f_ref, nrm_ref, h_ref):
    half = f_ref.shape[0] // 2
    xf = _dot3(f_ref[...], hf_ref[...])
    xb = _dot3(f_ref[...], hb_ref[...])
    inv = 1.0 / nrm_ref[...]
    h_ref[:half, :] = (xf[:half] + xb[:half]) * inv
    h_ref[half:, :] = (xf[half:] - xb[half:]) * inv


def ctx_filter_spectrum(taps, f_real, norm, width, n_orders):
    length = taps.shape[0]
    rows = f_real.shape[0]
    return pl.pallas_call(
        _ctx_spec_kernel,
        grid=(n_orders,),
        in_specs=[pl.BlockSpec((length, width), lambda o: (0, o)),
                  pl.BlockSpec((length, width), lambda o: (0, n_orders + o)),
                  _full_spec(f_real.shape),
                  pl.BlockSpec((1, width), lambda o: (0, o))],
        out_specs=pl.BlockSpec((rows, width), lambda o: (0, o)),
        out_shape=jax.ShapeDtypeStruct((rows, n_orders * width), F32),
        compiler_params=_cparams(("arbitrary",)),
        name="ctx_filter_spectrum",
    )(taps, taps, f_real, norm)


def _ctx_conv_kernel(u_ref, h_ref, ff_ref, fi_ref, bias_ref, zin_ref, o_ref, nat_ref, z_ref, *, width, fold):
    del zin_ref
    half = ff_ref.shape[0] // 2
    _unfold_to_rows(u_ref, nat_ref, fold)
    wt = width // LANES
    z = _load_tiles(nat_ref, 0, wt)
    for n in range(2):
        x = _dot(ff_ref[...], z.astype(BF16))
        xr, xi = x[:half], x[half:]
        hr = h_ref[:half, n * width:(n + 1) * width]
        hi = h_ref[half:, n * width:(n + 1) * width]
        y = jnp.concatenate([xr * hr - xi * hi, xr * hi + xi * hr], axis=0).astype(BF16)
        y = _dot(fi_ref[...], y)
        z = _load_tiles(nat_ref, (n + 1) * wt, (n + 2) * wt) * (y + z * bias_ref[n:n + 1, :])
    _store_tiles(z_ref, z)
    _fold_rows(z_ref, o_ref, fold)


def ctx_long_conv(cfg, u_f, h_ctx, f_fwd, f_inv, bias, z_out):
    d = cfg.d
    n2 = cfg.fft_n2
    rows = 2 * cfg.ctx
    g = rows // n2
    base = cfg.nlat // rows
    return pl.pallas_call(
        functools.partial(_ctx_conv_kernel, width=d, fold=n2),
        grid=(cfg.batch // 2,),
        in_specs=[pl.BlockSpec((g, n2 * 3 * d), lambda p: (base + p, 0)),
                  _full_spec(h_ctx.shape), _full_spec(f_fwd.shape), _full_spec(f_inv.shape),
                  _full_spec(bias.shape), pl.BlockSpec(memory_space=pl.ANY)],
        out_specs=pl.BlockSpec((g, n2 * d), lambda p: (base + p, 0)),
        out_shape=jax.ShapeDtypeStruct(z_out.shape, F32),
        scratch_shapes=[_lane_tiles(rows, 3 * d), _lane_tiles(rows, d)],
        input_output_aliases={5: 0},
        compiler_params=_cparams(("arbitrary",)),
        name="ctx_long_conv",
    )(u_f, h_ctx, f_fwd, f_inv, bias, z_out)


def _dft_tables(n1, n2, group):
    n = n1 * n2
    h1 = n1 // 2
    f1 = jnp.arange(n1, dtype=jnp.int32)[None, :, None]
    s1 = jnp.arange(h1, dtype=jnp.int32)[None, None, :]
    s2 = jnp.arange(n2, dtype=jnp.int32)[:, None, None]
    k = (f1 * (n2 * s1 + s2)) % n
    ang = (2.0 * math.pi / n) * k.astype(F32)
    er, ei = jnp.cos(ang), -jnp.sin(ang)
    fwd = jnp.concatenate([jnp.concatenate([er, -ei], axis=2),
                           jnp.concatenate([ei, er], axis=2)], axis=1)
    cr, ci = jnp.swapaxes(er, 1, 2) / n, -jnp.swapaxes(ei, 1, 2) / n
    inv = jnp.concatenate([jnp.concatenate([cr, -ci], axis=2),
                           jnp.concatenate([ci, cr], axis=2)], axis=1)
    q = jnp.arange(n1 // group)[:, None, None]
    ri = jnp.arange(2)[None, :, None]
    j = jnp.arange(group)[None, None, :]
    perm = (ri * n1 + q * group + j).reshape(-1)
    fwd = fwd[:, perm, :]
    inv = inv[:, :, perm]
    real_only = fwd[:, :, :h1]
    a = jnp.arange(n2, dtype=jnp.int32)
    ang2 = (2.0 * math.pi / n2) * ((a[:, None] * a[None, :]) % n2).astype(F32)
    f2r, f2i = jnp.cos(ang2), -jnp.sin(ang2)
    eye = jnp.eye(group, dtype=F32)
    t_fwd = jnp.stack([jnp.stack([f2r, -f2i]), jnp.stack([f2i, f2r])])
    t_inv = jnp.stack([jnp.stack([f2r, f2i]), jnp.stack([-f2i, f2r])])
    size = 2 * group * n2
    bd_fwd = jnp.einsum("abfs,jk->afjsbk", t_fwd, eye).reshape(size, size)
    bd_inv = jnp.einsum("absf,jk->sajbfk", t_inv, eye).reshape(size, size)
    return (fwd.astype(BF16), inv.astype(BF16), real_only.astype(BF16), bd_fwd.astype(BF16),
            bd_inv.astype(BF16))


def _ctx_dft_tables(length):
    n = 2 * length
    f = jnp.arange(n, dtype=jnp.int32)[:, None]
    s = jnp.arange(length, dtype=jnp.int32)[None, :]
    ang = (2.0 * math.pi / n) * ((f * s) % n).astype(F32)
    fr, fi = jnp.cos(ang), -jnp.sin(ang)
    fwd = jnp.concatenate([jnp.concatenate([fr, -fi], axis=1), jnp.concatenate([fi, fr], axis=1)], axis=0)
    frt, fit = fr.T / n, fi.T / n
    inv = jnp.concatenate([jnp.concatenate([frt, fit], axis=1), jnp.concatenate([-fit, frt], axis=1)], axis=0)
    real_only = jnp.concatenate([fr, fi], axis=0)
    return fwd.astype(BF16), inv.astype(BF16), real_only


def _filter_features(length, bands):
    pos = jnp.arange(length, dtype=F32)[:, None]
    t = pos / (length - 1)
    w = 2.0 * math.pi * pos / length
    bnd = jnp.linspace(1e-4, bands - 1, bands, dtype=F32)
    feats = jnp.concatenate([t, jnp.cos(bnd * w), -jnp.sin(bnd * w)], axis=-1)
    return jnp.pad(feats, ((0, 0), (0, 128 - feats.shape[1])))


def hyena_long_convs(cfg, u, fp, long_bias):
    d = cfg.d
    n1, n2 = cfg.fft_n1, cfg.fft_n2
    group = 8
    n_groups = n1 // group
    n_orders = 2
    n_pairs = cfg.batch // 2
    nrow = cfg.nt // n2
    (w1, b1, fq1, w2, b2, fq2, w3) = fp
    nh = w1.shape[1]
    w1p = jnp.pad(w1, ((0, 128 - w1.shape[0]), (0, 0)))
    bands = (w1.shape[0] - 1) // 2
    deltas = jnp.linspace(math.log(1e-2) / 1.5, math.log(1e-2) / 0.3, d, dtype=F32)
    deltas4 = jnp.tile(deltas, 2 * n_orders)[None, :]
    mlp = (w1p, b1.reshape(1, nh), fq1.reshape(1, nh), w2, b2.reshape(1, nh), fq2.reshape(1, nh), w3, deltas4)

    fwd, inv, real_only, bd_fwd, bd_inv = _dft_tables(n1, n2, group)

    taps, sums = filter_taps(cfg.seq, n2, _filter_features(cfg.seq, bands), *mlp)
    norm = sums[:, :n_orders * d] + sums[:, n_orders * d:]
    a_f = fft_stage1(taps, real_only, 1, n2, 1, 0, 2 * n_orders * d)
    h_lat = fft_filter_spectrum(a_f, bd_fwd, norm, n_groups, n2, d, n_orders)

    cf_fwd, cf_inv, cf_real = _ctx_dft_tables(cfg.ctx)
    taps_c, sums_c = filter_taps(cfg.ctx, 1, _filter_features(cfg.ctx, bands), *mlp)
    norm_c = sums_c[:, :n_orders * d] + sums_c[:, n_orders * d:]
    h_ctx = ctx_filter_spectrum(taps_c, cf_real, norm_c, d, n_orders)

    u_view = u
    z_view, z_stride, z_off = u_view, 3, 0
    z = None
    for order in range(n_orders):
        a = fft_stage1(z_view, fwd, n_pairs, n2, z_stride, z_off, d)
        bmid = fft_mid(a, h_lat, order, bd_fwd, bd_inv, n_pairs, n_groups, n2, d)
        z = fft_inverse_gate(bmid, inv, z_view, z_stride, z_off, u_view, 3, 1 + order,
                             long_bias[order:order + 1], n_pairs, n2, d, nrow)
        z_view, z_stride, z_off = z, 1, 0
    return ctx_long_conv(cfg, u, h_ctx, cf_fwd, cf_inv, long_bias, z)


META_E, META_W, META_LP = 0, 2, 4
CHUNK = 8
TAB_N, TAB_SRC, TAB_DST, TAB_TOTAL = 0, 32, 64, 96


def _local_rows(cfg):
    ne = cfg.n_groups * cfg.epg
    return -(-(2 * cfg.tb + (CHUNK - 1) * ne) // 128) * 128


def _route_kernel(x_ref, g_ref, sh_ref, sc_ref, wr_ref, br_ref, h_ref, meta_ref, tab_ref, cnt_ref, run_ref, *,
                  cfg):
    ne = cfg.n_groups * cfg.epg
    step = pl.program_id(0)
    h = _modnorm(x_ref[...], g_ref[...], sh_ref[0], sc_ref[0])
    h_ref[...] = h.astype(h_ref.dtype)
    logits = _dot3(h, wr_ref[...]) + br_ref[...]
    lane = lax.broadcasted_iota(jnp.int32, logits.shape, 1).astype(F32)
    neg = -jnp.inf
    big = 1e9

    def first_argmax(mask):
        v = jnp.where(mask, logits, neg)
        mx = jnp.max(v, axis=-1, keepdims=True)
        idx = jnp.min(jnp.where(mask & (logits == mx), lane, big), axis=-1, keepdims=True)
        return mx, idx

    gmask = (lane >= ne) & (lane < ne + cfg.n_groups)
    gmax, gidx = first_argmax(gmask)
    g_p = 1.0 / jnp.sum(jnp.where(gmask, jnp.exp(logits - gmax), 0.0), axis=-1, keepdims=True)
    e0 = (gidx - ne) * cfg.epg
    emask = (lane >= e0) & (lane < e0 + cfg.epg)
    m1, i1 = first_argmax(emask)
    m2, i2 = first_argmax(emask & (lane != i1))
    r = jnp.exp(m2 - m1)
    w1 = g_p / (1.0 + r)
    w2 = g_p * r / (1.0 + r)

    @pl.when(step == 0)
    def _():
        run_ref[...] = jnp.zeros_like(run_ref)

    hit1 = lane == i1
    hit2 = lane == i2
    onehot = jnp.where(hit1 | hit2, 1.0, 0.0)
    tb = onehot.shape[0]
    row = lax.broadcasted_iota(jnp.int32, (tb, tb), 0)
    col = lax.broadcasted_iota(jnp.int32, (tb, tb), 1)
    earlier = jnp.where(col < row, 1.0, 0.0).astype(BF16)
    before = _dot(earlier, onehot.astype(BF16))
    chunks = jnp.floor((jnp.sum(onehot, axis=0, keepdims=True) + (CHUNK - 1)) * (1.0 / CHUNK))
    ea = lax.broadcasted_iota(jnp.int32, (128, 128), 0)
    eb = lax.broadcasted_iota(jnp.int32, (128, 128), 1)
    lower_experts = jnp.where(ea < eb, 1.0, 0.0).astype(BF16)
    lstart = CHUNK * _dot(jnp.broadcast_to(chunks, (8, 128)).astype(BF16), lower_experts)[0:1]
    local = before + lstart
    lp1 = jnp.sum(jnp.where(hit1, local, 0.0), axis=-1, keepdims=True)
    lp2 = jnp.sum(jnp.where(hit2, local, 0.0), axis=-1, keepdims=True)

    tab = jnp.zeros((8, 128), F32)
    rows8 = lax.broadcasted_iota(jnp.int32, (8, 128), 0)
    for k, v in enumerate((chunks, lstart, run_ref[...])):
        tab = jnp.where(rows8 == k, v, tab)
    tab_ref[0] = tab
    run_ref[...] = run_ref[...] + chunks
    cnt_ref[...] = run_ref[...]

    meta = jnp.zeros_like(logits)
    for k, v in enumerate((i1, i2, w1, w2, lp1, lp2)):
        meta = jnp.where(lane == float(k), v, meta)
    meta_ref[...] = meta


def route(cfg, x, norm_g, mod, w_route, b_route, n_blocks):
    return pl.pallas_call(
        functools.partial(_route_kernel, cfg=cfg),
        grid=(n_blocks,),
        in_specs=[_row_spec(cfg, cfg.d), _full_spec((1, cfg.d)), _mod_spec(cfg, 3), _mod_spec(cfg, 4),
                  _full_spec(w_route.shape), _full_spec((1, 128))],
        out_specs=[_row_spec(cfg, cfg.d), _row_spec(cfg, 128), pl.BlockSpec((1, 8, 128), lambda j: (j, 0, 0)),
                   _full_spec((1, 128))],
        out_shape=[jax.ShapeDtypeStruct((n_blocks * cfg.tb, cfg.d), BF16),
                   jax.ShapeDtypeStruct((n_blocks * cfg.tb, 128), F32),
                   jax.ShapeDtypeStruct((n_blocks, 8, 128), F32),
                   jax.ShapeDtypeStruct((1, 128), F32)],
        scratch_shapes=[pltpu.VMEM((1, 128), F32)],
        compiler_params=_cparams(("arbitrary",)),
        name="route",
    )(x, norm_g, mod, mod, w_route, b_route)


def _chunk_copies(hbm, tab_ref, vmem, sem, n_experts, to_hbm):
    def copy(e, k):
        src = pl.multiple_of(tab_ref[0, 0, TAB_SRC + e] + k * CHUNK, CHUNK)
        dst = pl.multiple_of(tab_ref[0, 0, TAB_DST + e] + k * CHUNK, CHUNK)
        v = vmem.at[pl.ds(src, CHUNK), :]
        h = hbm.at[pl.ds(dst, CHUNK), :]
        return pltpu.make_async_copy(v, h, sem) if to_hbm else pltpu.make_async_copy(h, v, sem)

    def each(fn):
        for e in range(n_experts):
            def body(k, carry, e=e):
                fn(copy(e, k))
                return carry
            lax.fori_loop(0, tab_ref[0, 0, TAB_N + e], body, 0)

    return (lambda: each(lambda c: c.start())), (lambda: each(lambda c: c.wait()))


def _dispatch_kernel(tab_ref, meta_ref, h_ref, hs_hbm, sorted_ref, sem, *, cfg):
    ne = cfg.n_groups * cfg.epg
    lr = sorted_ref.shape[0]
    mt = meta_ref[...].T
    lp1 = mt[META_LP:META_LP + 1, :]
    lp2 = mt[META_LP + 1:META_LP + 2, :]
    row = lax.broadcasted_iota(jnp.int32, (lr, cfg.tb), 0).astype(F32)
    perm = jnp.where((row == lp1) | (row == lp2), 1.0, 0.0).astype(BF16)
    sorted_ref[...] = _dot(perm, h_ref[...].astype(BF16))
    start, wait = _chunk_copies(hs_hbm, tab_ref, sorted_ref, sem.at[0], ne, to_hbm=True)
    start()
    wait()


def dispatch_rows(cfg, h, meta, tab, n_blocks, nblk):
    tb = cfg.tb
    return pl.pallas_call(
        functools.partial(_dispatch_kernel, cfg=cfg),
        grid=(n_blocks,),
        in_specs=[pl.BlockSpec((1, 1, 128), lambda j: (j, 0, 0), memory_space=pltpu.SMEM),
                  _row_spec(cfg, 128), _row_spec(cfg, cfg.d)],
        out_specs=pl.BlockSpec(memory_space=pl.ANY),
        out_shape=jax.ShapeDtypeStruct((nblk * tb, cfg.d), F32),
        scratch_shapes=[pltpu.VMEM((_local_rows(cfg), cfg.d), F32), pltpu.SemaphoreType.DMA((1,))],
        compiler_params=_cparams(("arbitrary",)),
        name="dispatch_rows",
    )(tab, meta, h)


def _expert_kernel(be_ref, valid_ref, nu_ref, h_ref, wg_ref, wu_ref, wd_ref, y_ref):
    del be_ref, nu_ref
    valid = valid_ref[pl.program_id(0)]

    @pl.when(valid > 0)
    def _():
        row = lax.broadcasted_iota(jnp.int32, h_ref.shape, 0)
        hb = jnp.where(row < valid, h_ref[...], 0.0).astype(BF16)
        a = _dot(hb, wg_ref[0].astype(BF16))
        b = _dot(hb, wu_ref[0].astype(BF16))
        hid = (_silu(a) * b).astype(BF16)
        y_ref[...] = _dot(hid, wd_ref[0].astype(BF16))


def expert_mlp(cfg, h_sorted, block_expert, block_valid, n_used, w_gate, w_up, w_down, nblk):
    tb = cfg.tb
    de = cfg.d_exp

    def rows(i, be, bv, nu):
        return (jnp.minimum(i, nu[0] - 1), 0)

    grid_spec = pltpu.PrefetchScalarGridSpec(
        num_scalar_prefetch=3,
        grid=(nblk,),
        in_specs=[pl.BlockSpec((tb, cfg.d), rows),
                  pl.BlockSpec((1, cfg.d, de), lambda i, be, bv, nu: (be[i], 0, 0)),
                  pl.BlockSpec((1, cfg.d, de), lambda i, be, bv, nu: (be[i], 0, 0)),
                  pl.BlockSpec((1, de, cfg.d), lambda i, be, bv, nu: (be[i], 0, 0))],
        out_specs=pl.BlockSpec((tb, cfg.d), rows),
    )
    return pl.pallas_call(
        _expert_kernel,
        grid_spec=grid_spec,
        out_shape=jax.ShapeDtypeStruct((nblk * tb, cfg.d), F32),
        compiler_params=_cparams(("arbitrary",)),
        name="expert_mlp",
    )(block_expert, block_valid, n_used, h_sorted, w_gate, w_up, w_down)


def _combine_kernel(tab_ref, tab_next_ref, x_ref, meta_ref, gate_ref, y_hbm, *rest, cfg, nblk, final):
    if final:
        fg_ref, o_ref, ybuf, sem = rest
    else:
        o_ref, ybuf, sem = rest
    ne = cfg.n_groups * cfg.epg
    i = pl.program_id(0)
    slot = i % 2
    start_cur, wait_cur = _chunk_copies(y_hbm, tab_ref, ybuf.at[slot], sem.at[slot], ne, to_hbm=False)
    start_next, _ = _chunk_copies(y_hbm, tab_next_ref, ybuf.at[1 - slot], sem.at[1 - slot], ne, to_hbm=False)

    @pl.when(i == 0)
    def _():
        start_cur()

    @pl.when(i + 1 < nblk)
    def _():
        start_next()

    wait_cur()
    lr = ybuf.shape[1]
    row = lax.broadcasted_iota(jnp.int32, (lr, 1), 0)
    y = jnp.where(row < tab_ref[0, 0, TAB_TOTAL], ybuf[slot], 0.0).astype(BF16)
    meta = meta_ref[...]
    lane = lax.broadcasted_iota(jnp.int32, (cfg.tb, lr), 1).astype(F32)
    pick1 = jnp.where(lane == meta[:, META_LP:META_LP + 1], 1.0, 0.0).astype(BF16)
    pick2 = jnp.where(lane == meta[:, META_LP + 1:META_LP + 2], 1.0, 0.0).astype(BF16)
    w1 = meta[:, META_W:META_W + 1]
    w2 = meta[:, META_W + 1:META_W + 2]
    out = x_ref[...] + gate_ref[0] * (w1 * _dot(pick1, y) + w2 * _dot(pick2, y))
    if final:
        out = out * lax.rsqrt(jnp.mean(out * out, axis=-1, keepdims=True) + EPS) * fg_ref[...]
    o_ref[...] = out


def combine_residual(cfg, x, y_sorted, tab, meta, mod, gate_chunk, n_blocks, final_g=None):
    tb = cfg.tb
    final = final_g is not None
    extra_specs, extra_args = ([_full_spec((1, cfg.d))], [final_g]) if final else ([], [])
    return pl.pallas_call(
        functools.partial(_combine_kernel, cfg=cfg, nblk=n_blocks, final=final),
        grid=(n_blocks,),
        in_specs=[pl.BlockSpec((1, 1, 128), lambda j: (j, 0, 0), memory_space=pltpu.SMEM),
                  pl.BlockSpec((1, 1, 128), lambda j: (jnp.minimum(j + 1, n_blocks - 1), 0, 0),
                               memory_space=pltpu.SMEM),
                  _row_spec(cfg, cfg.d), _row_spec(cfg, 128), _mod_spec(cfg, gate_chunk),
                  pl.BlockSpec(memory_space=pl.ANY)] + extra_specs,
        out_specs=_row_spec(cfg, cfg.d),
        out_shape=jax.ShapeDtypeStruct((n_blocks * tb, cfg.d), F32),
        scratch_shapes=[pltpu.VMEM((2, _local_rows(cfg), cfg.d), F32), pltpu.SemaphoreType.DMA((2,))],
        compiler_params=_cparams(("arbitrary",)),
        name="combine_residual",
    )(tab, tab, x, meta, mod, y_sorted, *extra_args)


def _rope_tables(cfg):
    hd = cfg.head_dim
    pairs = hd // 4
    rows = cfg.seq // cfg.grid_w
    row = jnp.repeat(jnp.arange(rows), cfg.grid_w).astype(F32)
    col = jnp.tile(jnp.arange(cfg.grid_w), rows).astype(F32)
    inv_freq = 10000.0 ** (-jnp.arange(pairs, dtype=F32) / pairs)
    ar, ac = row[:, None] * inv_freq, col[:, None] * inv_freq
    cos = jnp.concatenate([jnp.cos(ar), jnp.cos(ar), jnp.cos(ac), jnp.cos(ac)], axis=-1)
    sin = jnp.concatenate([-jnp.sin(ar), jnp.sin(ar), -jnp.sin(ac), jnp.sin(ac)], axis=-1)
    cos = jnp.concatenate([cos, jnp.ones((cfg.tb, hd), F32)], axis=0)
    sin = jnp.concatenate([sin, jnp.zeros((cfg.tb, hd), F32)], axis=0)
    return cos, sin


def _moe_layer(cfg, x, norm_g, mod, w_group, b_group, w_router, b_router, w_gate, w_up, w_down, layer, n_blocks,
               final_g=None):
    ne = cfg.n_groups * cfg.epg
    w_route = jnp.pad(jnp.concatenate([w_router, w_group], axis=1), ((0, 0), (0, 128 - ne - cfg.n_groups)))
    b_route = jnp.pad(jnp.concatenate([b_router, b_group]), (0, 128 - ne - cfg.n_groups)).reshape(1, 128)
    h, meta, tabs, totals = route(cfg, x, norm_g, mod, w_route, b_route, n_blocks)

    tb = cfg.tb
    nblk = -(-(2 * n_blocks * tb + (CHUNK - 1) * ne * n_blocks) // tb) + ne
    used = CHUNK * totals[0, :ne].astype(jnp.int32)
    padded = ((used + tb - 1) // tb) * tb
    seg_end = jnp.cumsum(padded)
    seg_start = seg_end - padded
    t = tabs.astype(jnp.int32)
    n_chunks, src0, before = t[:, 0, :ne], t[:, 1, :ne], t[:, 2, :ne]
    dst0 = seg_start[None, :] + CHUNK * before
    local_total = CHUNK * jnp.sum(n_chunks, axis=1, keepdims=True)
    tab = jnp.concatenate([n_chunks, src0, dst0, local_total,
                           jnp.zeros((n_blocks, 128 - 3 * ne - 1), jnp.int32)], axis=1).reshape(n_blocks, 1, 128)
    blk_row = jnp.arange(nblk, dtype=jnp.int32) * tb
    block_expert = jnp.minimum(jnp.sum(seg_end[None, :] <= blk_row[:, None], axis=-1), ne - 1).astype(jnp.int32)
    block_valid = jnp.clip(seg_start[block_expert] + used[block_expert] - blk_row, 0, tb).astype(jnp.int32)
    n_used = (seg_end[ne - 1:ne] // tb).astype(jnp.int32)

    h_sorted = dispatch_rows(cfg, h, meta, tab, n_blocks, nblk)
    y = expert_mlp(cfg, h_sorted, block_expert + layer * ne, block_valid, n_used,
                   w_gate.reshape(-1, cfg.d, cfg.d_exp), w_up.reshape(-1, cfg.d, cfg.d_exp),
                   w_down.reshape(-1, cfg.d_exp, cfg.d), nblk)
    return combine_residual(cfg, x, y, tab, meta, mod, 5, n_blocks, final_g)


def _forward(cfg, x, c, ctx, c_ctx, w_mod, b_mod, norm_mix_g, norm_ffn_g,
             attn_w_q, attn_w_kv, attn_q_gain, attn_k_gain, attn_w_o,
             conv_w_pw1, conv_b_pw1, conv_w_dw, conv_b_dw, conv_ln_g, conv_ln_b, conv_w_pw2, conv_b_pw2,
             hy_w_in, hy_b_in, hy_w_short, hy_b_short, hy_f_w1, hy_f_b1, hy_f_freq1, hy_f_w2, hy_f_b2,
             hy_f_freq2, hy_f_w3, hy_long_bias, hy_w_out, hy_b_out,
             moe_w_group, moe_b_group, moe_w_router, moe_b_router, moe_w_gate, moe_w_up, moe_w_down,
             final_norm_g):
    d = cfg.d
    depth = w_mod.shape[0]
    xs = jnp.concatenate([x.reshape(-1, d), ctx.reshape(-1, d)], axis=0)
    c_all = jnp.concatenate([c, c_ctx[None, :], jnp.zeros((MOD_ROWS - cfg.batch - 1, d), F32)], axis=0)
    mods = modulation(c_all, w_mod, b_mod)
    rope_cos, rope_sin = _rope_tables(cfg)
    zero_bias = jnp.zeros((1, d), F32)

    for i in range(depth):
        kind, slot = i % 3, i // 3
        last = i == depth - 1
        n_blocks = cfg.nbl if last else cfg.nb
        mod = mods[i].reshape(MOD_ROWS * 6, 1, d)
        g_mix = norm_mix_g[i].reshape(1, d)
        if kind == 0:
            w_qkv = jnp.concatenate([attn_w_q[slot], attn_w_kv[slot]], axis=1).astype(BF16)
            qkv = qkv_project(cfg, xs, g_mix, mod, w_qkv, rope_cos, rope_sin,
                              attn_q_gain[slot].reshape(1, -1), attn_k_gain[slot].reshape(1, -1))
            o = attention(cfg, qkv)
            xs = project_residual(cfg, o, attn_w_o[slot].astype(BF16), zero_bias, xs, mod, 2, n_blocks)
        elif kind == 1:
            u = glu_project(cfg, xs, g_mix, mod, conv_w_pw1[slot].astype(BF16), conv_b_pw1[slot].reshape(1, -1))
            xs = conv_module(cfg, u, conv_w_dw[slot], conv_b_dw[slot].reshape(1, d), conv_ln_g[slot].reshape(1, d),
                             conv_ln_b[slot].reshape(1, d), conv_w_pw2[slot].astype(BF16),
                             conv_b_pw2[slot].reshape(1, d), xs, mod)
        else:
            p = in_project(cfg, xs, g_mix, mod, hy_w_in[slot].astype(BF16), hy_b_in[slot].reshape(1, -1))
            u = short_conv(cfg, p, hy_w_short[slot], hy_b_short[slot].reshape(1, -1))
            fp = (hy_f_w1[slot], hy_f_b1[slot], hy_f_freq1[slot], hy_f_w2[slot], hy_f_b2[slot],
                  hy_f_freq2[slot], hy_f_w3[slot])
            z2 = hyena_long_convs(cfg, u, fp, hy_long_bias[slot])
            xs = project_residual(cfg, z2, hy_w_out[slot].astype(BF16), hy_b_out[slot].reshape(1, d), xs, mod, 2,
                                  n_blocks, fold=cfg.fft_n2)
        xs = _moe_layer(cfg, xs, norm_ffn_g[i].reshape(1, d), mod, moe_w_group[i], moe_b_group[i],
                        moe_w_router[i], moe_b_router[i], moe_w_gate, moe_w_up, moe_w_down, i, n_blocks,
                        final_norm_g.reshape(1, d) if last else None)

    return xs.reshape(cfg.batch, cfg.seq, d)


def kernel(x, c, ctx, c_ctx, w_mod, b_mod, norm_mix_g, norm_ffn_g, attn_w_q, attn_w_kv, attn_q_gain, attn_k_gain, attn_w_o, conv_w_pw1, conv_b_pw1, conv_w_dw, conv_b_dw, conv_ln_g, conv_ln_b, conv_w_pw2, conv_b_pw2, hy_w_in, hy_b_in, hy_w_short, hy_b_short, hy_f_w1, hy_f_b1, hy_f_freq1, hy_f_w2, hy_f_b2, hy_f_freq2, hy_f_w3, hy_long_bias, hy_w_out, hy_b_out, moe_w_group, moe_b_group, moe_w_router, moe_b_router, moe_w_gate, moe_w_up, moe_w_down, final_norm_g):
    return _forward(CFG, x, c, ctx, c_ctx, w_mod, b_mod, norm_mix_g, norm_ffn_g, attn_w_q, attn_w_kv, attn_q_gain, attn_k_gain, attn_w_o, conv_w_pw1, conv_b_pw1, conv_w_dw, conv_b_dw, conv_ln_g, conv_ln_b, conv_w_pw2, conv_b_pw2, hy_w_in, hy_b_in, hy_w_short, hy_b_short, hy_f_w1, hy_f_b1, hy_f_freq1, hy_f_w2, hy_f_b2, hy_f_freq2, hy_f_w3, hy_long_bias, hy_w_out, hy_b_out, moe_w_group, moe_b_group, moe_w_router, moe_b_router, moe_w_gate, moe_w_up, moe_w_down, final_norm_g)
```

```python
import functools
import math
from typing import NamedTuple

import jax
import jax.numpy as jnp
from jax import lax
from jax.experimental import pallas as pl
from jax.experimental.pallas import tpu as pltpu

F32 = jnp.float32
BF16 = jnp.bfloat16
EPS = 1e-6
V7X_VMEM_LIMIT_BYTES = 56 * 1024 * 1024


class Cfg(NamedTuple):
    batch: int
    seq: int
    ctx: int
    d: int
    grid_w: int
    n_heads: int
    n_kv: int
    head_dim: int
    conv_w: int
    n_groups: int
    epg: int
    d_exp: int
    tb: int
    fft_n1: int
    fft_n2: int
    kv_chunk: int

    @property
    def nlat(self):
        return self.batch * self.seq

    @property
    def nt(self):
        return self.batch * (self.seq + self.ctx)

    @property
    def nbl(self):
        return self.nlat // self.tb

    @property
    def nbc(self):
        return self.batch * self.ctx // self.tb

    @property
    def nb(self):
        return self.nbl + self.nbc

    @property
    def spb(self):
        return self.seq // self.tb

    @property
    def cpb(self):
        return self.ctx // self.tb


CFG = Cfg(batch=8, seq=4096, ctx=256, d=1024, grid_w=64, n_heads=8, n_kv=2, head_dim=128, conv_w=31,
          n_groups=4, epg=8, d_exp=256, tb=256, fft_n1=256, fft_n2=32, kv_chunk=2048)

MOD_ROWS = 16


def _cparams(sem):
    return pltpu.CompilerParams(dimension_semantics=sem, vmem_limit_bytes=V7X_VMEM_LIMIT_BYTES)


def _split_bf16(a):
    hi = a.astype(BF16)
    lo = (a - hi.astype(F32)).astype(BF16)
    return hi, lo


def _dot(a, b):
    return jnp.dot(a, b, preferred_element_type=F32)


def _dot3(a, b):
    ah, al = _split_bf16(a)
    bh, bl = _split_bf16(b)
    return _dot(ah, bh) + (_dot(ah, bl) + _dot(al, bh))


def _modnorm(x, g, sh, sc):
    ms = jnp.mean(x * x, axis=-1, keepdims=True)
    y = x * lax.rsqrt(ms + EPS) * g
    return y * (1.0 + sc) + sh


def _silu(x):
    return x * jax.nn.sigmoid(x)


def _mod_row(cfg, j):
    return jnp.where(j < cfg.nbl, j // cfg.spb, cfg.batch)


def _mod_spec(cfg, chunk):
    return pl.BlockSpec((1, 1, cfg.d), lambda j: (_mod_row(cfg, j) * 6 + chunk, 0, 0))


def _row_spec(cfg, width):
    return pl.BlockSpec((cfg.tb, width), lambda j: (j, 0))


def _full_spec(shape):
    n = len(shape)
    return pl.BlockSpec(shape, lambda *_: (0,) * n)


def _mod_kernel(c_ref, w_ref, b_ref, o_ref):
    o_ref[0] = _dot3(_silu(c_ref[...]), w_ref[0]) + b_ref[0]


def modulation(c_all, w_mod, b_mod):
    depth, d, n6 = w_mod.shape
    bn = n6 // 4
    return pl.pallas_call(
        _mod_kernel,
        grid=(depth, n6 // bn),
        in_specs=[_full_spec((MOD_ROWS, d)),
                  pl.BlockSpec((1, d, bn), lambda i, n: (i, 0, n)),
                  pl.BlockSpec((1, 1, bn), lambda i, n: (i, 0, n))],
        out_specs=pl.BlockSpec((1, MOD_ROWS, bn), lambda i, n: (i, 0, n)),
        out_shape=jax.ShapeDtypeStruct((depth, MOD_ROWS, n6), F32),
        compiler_params=_cparams(("arbitrary", "arbitrary")),
        name="modulation",
    )(c_all, w_mod, b_mod.reshape(depth, 1, n6))


def _qkv_kernel(x_ref, g_ref, sh_ref, sc_ref, w_ref, cs_ref, sn_ref, qg_ref, kg_ref, o_ref, *, cfg):
    hd = cfg.head_dim
    h = _modnorm(x_ref[...], g_ref[...], sh_ref[0], sc_ref[0]).astype(BF16)
    y = _dot(h, w_ref[...])
    cs = cs_ref[...]
    sn = sn_ref[...]
    lane = lax.broadcasted_iota(jnp.int32, cs.shape, 1)
    first_half = (lane % (hd // 2)) < (hd // 4)

    def norm_rope(v, gain, scale):
        ms = jnp.mean(v * v, axis=-1, keepdims=True)
        v = v * lax.rsqrt(ms + EPS) * gain
        partner = jnp.where(first_half, pltpu.roll(v, hd - hd // 4, 1), pltpu.roll(v, hd // 4, 1))
        return (v * cs + partner * sn) * scale

    nq = cfg.n_heads
    for hh in range(nq):
        sl = slice(hh * hd, (hh + 1) * hd)
        o_ref[:, sl] = norm_rope(y[:, sl], qg_ref[...], hd ** -0.5 * math.log2(math.e)).astype(o_ref.dtype)
    for hh in range(cfg.n_kv):
        sl = slice((nq + hh) * hd, (nq + hh + 1) * hd)
        o_ref[:, sl] = norm_rope(y[:, sl], kg_ref[...], 1.0).astype(o_ref.dtype)
    v0 = (nq + cfg.n_kv) * hd
    o_ref[:, v0:] = y[:, v0:].astype(o_ref.dtype)


def qkv_project(cfg, x, norm_g, mod, w_qkv, rope_cos, rope_sin, q_gain, k_gain):
    nqkv = w_qkv.shape[1]
    hd = cfg.head_dim

    def rope_idx(j):
        return (jnp.where(j < cfg.nbl, j % cfg.spb, cfg.spb), 0)

    return pl.pallas_call(
        functools.partial(_qkv_kernel, cfg=cfg),
        grid=(cfg.nb,),
        in_specs=[_row_spec(cfg, cfg.d), _full_spec((1, cfg.d)), _mod_spec(cfg, 0), _mod_spec(cfg, 1),
                  _full_spec((cfg.d, nqkv)),
                  pl.BlockSpec((cfg.tb, hd), rope_idx), pl.BlockSpec((cfg.tb, hd), rope_idx),
                  _full_spec((1, hd)), _full_spec((1, hd))],
        out_specs=_row_spec(cfg, nqkv),
        out_shape=jax.ShapeDtypeStruct((cfg.nt, nqkv), BF16),
        compiler_params=_cparams(("arbitrary",)),
        name="qkv_project",
    )(x, norm_g, mod, mod, w_qkv, rope_cos, rope_sin, q_gain, k_gain)


def _transpose_bf16(x):
    return x.astype(F32).T.astype(BF16)


def _attn_kernel(q_ref, kc_ref, vc_ref, kl_ref, vl_ref, o_ref, vtc_ref, vtl_ref, *, cfg, n_lat_chunks):
    hd = cfg.head_dim
    group = cfg.n_heads // cfg.n_kv
    tq = cfg.tb
    ch = cfg.kv_chunk

    @pl.when(pl.program_id(2) == 0)
    def _():
        vtc_ref[...] = _transpose_bf16(vc_ref[...])
        for c in range(n_lat_chunks):
            vtl_ref[c] = _transpose_bf16(vl_ref[c * ch:(c + 1) * ch, :])

    qt = jnp.concatenate([_transpose_bf16(q_ref[:, h * hd:(h + 1) * hd]) for h in range(group)], axis=1)

    s = _dot(kc_ref[...], qt)
    m = jnp.max(s, axis=0, keepdims=True)
    p = jnp.exp2(s - m)
    l = jnp.sum(p, axis=0, keepdims=True)
    acc = _dot(vtc_ref[...], p.astype(BF16))

    def body(c, carry):
        m, l, acc = carry
        start = pl.multiple_of(c * ch, ch)
        s = _dot(kl_ref[pl.ds(start, ch), :], qt)
        m_new = jnp.maximum(m, jnp.max(s, axis=0, keepdims=True))
        alpha = jnp.exp2(m - m_new)
        p = jnp.exp2(s - m_new)
        l = alpha * l + jnp.sum(p, axis=0, keepdims=True)
        acc = alpha * acc + _dot(vtl_ref[c], p.astype(BF16))
        return m_new, l, acc

    n = jnp.where(pl.program_id(2) < cfg.spb, n_lat_chunks, 0)
    m, l, acc = lax.fori_loop(0, n, body, (m, l, acc))
    o = acc / l
    for h in range(group):
        o_ref[:, h * hd:(h + 1) * hd] = o[:, h * tq:(h + 1) * tq].T.astype(o_ref.dtype)


def attention(cfg, qkv):
    hd = cfg.head_dim
    group = cfg.n_heads // cfg.n_kv
    gw = group * hd
    kcol = cfg.n_heads
    vcol = cfg.n_heads + cfg.n_kv
    assert cfg.cpb == 1

    def qrow(b, qb):
        return jnp.where(qb < cfg.spb, b * cfg.spb + qb, cfg.nbl + b)

    return pl.pallas_call(
        functools.partial(_attn_kernel, cfg=cfg, n_lat_chunks=cfg.seq // cfg.kv_chunk),
        grid=(cfg.batch, cfg.n_kv, cfg.spb + 1),
        in_specs=[pl.BlockSpec((cfg.tb, gw), lambda b, k, qb: (qrow(b, qb), k)),
                  pl.BlockSpec((cfg.ctx, hd), lambda b, k, qb: (cfg.nlat // cfg.ctx + b, kcol + k)),
                  pl.BlockSpec((cfg.ctx, hd), lambda b, k, qb: (cfg.nlat // cfg.ctx + b, vcol + k)),
                  pl.BlockSpec((cfg.seq, hd), lambda b, k, qb: (b, kcol + k)),
                  pl.BlockSpec((cfg.seq, hd), lambda b, k, qb: (b, vcol + k))],
        out_specs=pl.BlockSpec((cfg.tb, gw), lambda b, k, qb: (qrow(b, qb), k)),
        out_shape=jax.ShapeDtypeStruct((cfg.nt, cfg.n_heads * hd), BF16),
        scratch_shapes=[pltpu.VMEM((hd, cfg.ctx), BF16),
                        pltpu.VMEM((cfg.seq // cfg.kv_chunk, hd, cfg.kv_chunk), BF16)],
        compiler_params=_cparams(("arbitrary", "arbitrary", "arbitrary")),
        name="attention",
    )(qkv, qkv, qkv, qkv, qkv)


def _proj_res_kernel(a_ref, w_ref, b_ref, x_ref, gate_ref, o_ref):
    y = _dot(a_ref[...].astype(BF16), w_ref[...]) + b_ref[...]
    o_ref[...] = x_ref[...] + gate_ref[0] * y


def _proj_res_folded_kernel(a_ref, w_ref, b_ref, x_ref, gate_ref, o_ref, nat_ref, *, fold):
    _unfold_to_rows(a_ref, nat_ref, fold)
    y = _dot(_load_tiles(nat_ref).astype(BF16), w_ref[...]) + b_ref[...]
    o_ref[...] = x_ref[...] + gate_ref[0] * y


def project_residual(cfg, a, w, bias, x, mod, gate_chunk, n_blocks, fold=1):
    if fold == 1:
        body, a_spec, scratch = _proj_res_kernel, _row_spec(cfg, a.shape[1]), []
    else:
        body = functools.partial(_proj_res_folded_kernel, fold=fold)
        a_spec = pl.BlockSpec((cfg.tb // fold, a.shape[1]), lambda j: (j, 0))
        scratch = [_lane_tiles(cfg.tb, a.shape[1] // fold)]
    return pl.pallas_call(
        body,
        grid=(n_blocks,),
        in_specs=[a_spec, _full_spec(w.shape), _full_spec((1, cfg.d)),
                  _row_spec(cfg, cfg.d), _mod_spec(cfg, gate_chunk)],
        out_specs=_row_spec(cfg, cfg.d),
        out_shape=jax.ShapeDtypeStruct((n_blocks * cfg.tb, cfg.d), F32),
        scratch_shapes=scratch,
        compiler_params=_cparams(("arbitrary",)),
        name="project_residual",
    )(a, w, bias, x, mod)


def _glu_kernel(x_ref, g_ref, sh_ref, sc_ref, w_ref, b_ref, o_ref, *, d):
    h = _modnorm(x_ref[...], g_ref[...], sh_ref[0], sc_ref[0]).astype(BF16)
    y = _dot(h, w_ref[...]) + b_ref[...]
    o_ref[...] = (y[:, :d] * jax.nn.sigmoid(y[:, d:])).astype(o_ref.dtype)


def glu_project(cfg, x, norm_g, mod, w, bias):
    return pl.pallas_call(
        functools.partial(_glu_kernel, d=cfg.d),
        grid=(cfg.nb,),
        in_specs=[_row_spec(cfg, cfg.d), _full_spec((1, cfg.d)), _mod_spec(cfg, 0), _mod_spec(cfg, 1),
                  _full_spec(w.shape), _full_spec((1, w.shape[1]))],
        out_specs=_row_spec(cfg, cfg.d),
        out_shape=jax.ShapeDtypeStruct((cfg.nt, cfg.d), BF16),
        compiler_params=_cparams(("arbitrary",)),
        name="glu_project",
    )(x, norm_g, mod, mod, w, bias)


def _seq_edges(cfg, j):
    lat = j < cfg.nbl
    first = jnp.where(lat, j % cfg.spb == 0, (j - cfg.nbl) % cfg.cpb == 0)
    last = jnp.where(lat, j % cfg.spb == cfg.spb - 1, (j - cfg.nbl) % cfg.cpb == cfg.cpb - 1)
    return first, last


def _halo_specs(cfg, width, halo):
    per = cfg.tb // halo
    last_blk = cfg.nt // halo - 1
    prev = pl.BlockSpec((halo, width), lambda j: (jnp.maximum(j * per - 1, 0), 0))
    nxt = pl.BlockSpec((halo, width), lambda j: (jnp.minimum((j + 1) * per, last_blk), 0))
    return prev, nxt


def _fill_ext(cfg, ext_ref, prev_ref, cur_ref, next_ref, halo):
    first, last = _seq_edges(cfg, pl.program_id(0))
    tb = cfg.tb
    ext_ref[0:halo, :] = jnp.where(first, 0.0, prev_ref[...].astype(F32))
    ext_ref[halo:halo + tb, :] = cur_ref[...].astype(F32)
    ext_ref[halo + tb:2 * halo + tb, :] = jnp.where(last, 0.0, next_ref[...].astype(F32))


def _conv_kernel(up_ref, uc_ref, un_ref, wdw_ref, bdw_ref, lng_ref, lnb_ref, w2_ref, b2_ref, x_ref, gate_ref,
                 o_ref, ext_ref, *, cfg, halo):
    tb = cfg.tb
    _fill_ext(cfg, ext_ref, up_ref, uc_ref, un_ref, halo)
    half = (cfg.conv_w - 1) // 2
    sub = 8
    acc = jnp.zeros((tb, cfg.d), F32)
    for r in range(sub):
        part = None
        for k in range(cfg.conv_w):
            off = halo - half + k
            if off % sub != r:
                continue
            base = off - r
            term = wdw_ref[k:k + 1, :] * ext_ref[base:base + tb + sub, :]
            part = term if part is None else part + term
        if part is not None:
            acc = acc + part[r:r + tb]
    u = acc + bdw_ref[...]
    mu = jnp.mean(u, axis=-1, keepdims=True)
    uc = u - mu
    u = uc * lax.rsqrt(jnp.mean(uc * uc, axis=-1, keepdims=True) + EPS) * lng_ref[...] + lnb_ref[...]
    u = _silu(u).astype(BF16)
    y = _dot(u, w2_ref[...]) + b2_ref[...]
    o_ref[...] = x_ref[...] + gate_ref[0] * y


def conv_module(cfg, u, w_dw, b_dw, ln_g, ln_b, w2, b2, x, mod):
    halo = 16
    prev, nxt = _halo_specs(cfg, cfg.d, halo)
    return pl.pallas_call(
        functools.partial(_conv_kernel, cfg=cfg, halo=halo),
        grid=(cfg.nb,),
        in_specs=[prev, _row_spec(cfg, cfg.d), nxt, _full_spec(w_dw.shape), _full_spec((1, cfg.d)),
                  _full_spec((1, cfg.d)), _full_spec((1, cfg.d)), _full_spec(w2.shape), _full_spec((1, cfg.d)),
                  _row_spec(cfg, cfg.d), _mod_spec(cfg, 2)],
        out_specs=_row_spec(cfg, cfg.d),
        out_shape=jax.ShapeDtypeStruct((cfg.nt, cfg.d), F32),
        scratch_shapes=[pltpu.VMEM((cfg.tb + 2 * halo, cfg.d), F32)],
        compiler_params=_cparams(("arbitrary",)),
        name="conv_module",
    )(u, u, u, w_dw, b_dw, ln_g, ln_b, w2, b2, x, mod)


def _inproj_kernel(x_ref, g_ref, sh_ref, sc_ref, w_ref, b_ref, o_ref):
    h = _modnorm(x_ref[...], g_ref[...], sh_ref[0], sc_ref[0]).astype(BF16)
    o_ref[...] = (_dot(h, w_ref[...]) + b_ref[...]).astype(o_ref.dtype)


def in_project(cfg, x, norm_g, mod, w, bias):
    n = w.shape[1]
    return pl.pallas_call(
        _inproj_kernel,
        grid=(cfg.nb,),
        in_specs=[_row_spec(cfg, cfg.d), _full_spec((1, cfg.d)), _mod_spec(cfg, 0), _mod_spec(cfg, 1),
                  _full_spec(w.shape), _full_spec((1, n))],
        out_specs=_row_spec(cfg, n),
        out_shape=jax.ShapeDtypeStruct((cfg.nt, n), BF16),
        compiler_params=_cparams(("arbitrary",)),
        name="in_project",
    )(x, norm_g, mod, mod, w, bias)


LANES = 128


def _lane_tiles(rows, width):
    return pltpu.VMEM((width // LANES, rows, LANES), F32)


def _store_tiles(tiles_ref, value):
    for c in range(tiles_ref.shape[0]):
        tiles_ref[c] = value[:, c * LANES:(c + 1) * LANES]


def _load_tiles(tiles_ref, c0=0, c1=None):
    c1 = tiles_ref.shape[0] if c1 is None else c1
    return jnp.concatenate([tiles_ref[c] for c in range(c0, c1)], axis=1)


def _fold_rows(tiles_ref, o_ref, fold):
    nt, rows, _ = tiles_ref.shape
    w = nt * LANES
    for s in range(fold):
        for c in range(nt):
            o_ref[:, s * w + c * LANES:s * w + (c + 1) * LANES] = tiles_ref[c, pl.ds(s, rows // fold, stride=fold), :]


def _unfold_to_rows(f_ref, tiles_ref, fold):
    nt, rows, _ = tiles_ref.shape
    w = nt * LANES
    for s in range(fold):
        for c in range(nt):
            tiles_ref[c, pl.ds(s, rows // fold, stride=fold), :] = f_ref[:, s * w + c * LANES:s * w + (c + 1) * LANES]


def _short_conv_kernel(pp_ref, pc_ref, pn_ref, w_ref, b_ref, o_ref, ext_ref, nat_ref, *, cfg, halo, width):
    tb = cfg.tb
    _fill_ext(cfg, ext_ref, pp_ref, pc_ref, pn_ref, halo)
    half = (width - 1) // 2
    acc = b_ref[...]
    for k in range(width):
        off = halo - half + k
        acc = acc + w_ref[k:k + 1, :] * ext_ref[off:off + tb, :]
    _store_tiles(nat_ref, acc)
    _fold_rows(nat_ref, o_ref, cfg.fft_n2)


def short_conv(cfg, p, w, bias):
    halo = 16
    n = p.shape[1]
    n2 = cfg.fft_n2
    per = cfg.tb // halo
    last_blk = cfg.nt // halo - 1
    return pl.pallas_call(
        functools.partial(_short_conv_kernel, cfg=cfg, halo=halo, width=w.shape[0]),
        grid=(cfg.nb,),
        in_specs=[pl.BlockSpec((halo, n), lambda j: (jnp.maximum(j * per - 1, 0), 0)),
                  pl.BlockSpec((cfg.tb, n), lambda j: (j, 0)),
                  pl.BlockSpec((halo, n), lambda j: (jnp.minimum((j + 1) * per, last_blk), 0)),
                  _full_spec(w.shape), _full_spec((1, n))],
        out_specs=pl.BlockSpec((cfg.tb // n2, n2 * n), lambda j: (j, 0)),
        out_shape=jax.ShapeDtypeStruct((cfg.nt // n2, n2 * n), F32),
        scratch_shapes=[pltpu.VMEM((cfg.tb + 2 * halo, n), F32), _lane_tiles(cfg.tb, n)],
        compiler_params=_cparams(("arbitrary",)),
        name="short_conv",
    )(p, p, p, w, bias)


def _filter_mlp_kernel(feat_ref, w1_ref, b1_ref, f1_ref, w2_ref, b2_ref, f2_ref, w3_ref, delta_ref,
                       h_ref, s_ref, *, length, row_mul, step_mul):
    i = pl.program_id(0)
    z = jnp.sin(f1_ref[...] * (_dot3(feat_ref[0], w1_ref[...]) + b1_ref[...]))
    z = jnp.sin(f2_ref[...] * (_dot3(z, w2_ref[...]) + b2_ref[...]))
    h = _dot3(z, w3_ref[...])
    pos = (i * step_mul + row_mul * lax.broadcasted_iota(jnp.int32, h.shape, 0)).astype(F32)
    t = pos / (length - 1)
    h = h * jnp.exp(-t * jnp.abs(delta_ref[...]))
    col = lax.broadcasted_iota(jnp.int32, h.shape, 1)
    h = jnp.where((pos == 0.0) & (col >= h.shape[1] // 2), 0.0, h)
    h_ref[...] = h
    part = jnp.sum(jnp.abs(h), axis=0, keepdims=True)

    @pl.when(i == 0)
    def _():
        s_ref[...] = part

    @pl.when(i > 0)
    def _():
        s_ref[...] = s_ref[...] + part


def filter_taps(length, fold, feats, w1, b1, f1, w2, b2, f2, w3, deltas4):
    nh = w1.shape[1]
    n = w3.shape[1]
    if fold == 1:
        rows = min(length, 256)
        steps = length // rows
        feats3 = feats.reshape(steps, rows, feats.shape[1])
        row_mul, step_mul = 1, rows
        out_spec = pl.BlockSpec((rows, n), lambda i: (i, 0))
        out_shape = (length, n)
    else:
        rows = length // fold
        steps = fold
        feats3 = feats.reshape(rows, fold, feats.shape[1]).transpose(1, 0, 2)
        row_mul, step_mul = fold, 1
        out_spec = pl.BlockSpec((rows, n), lambda i: (0, i))
        out_shape = (rows, fold * n)
    return pl.pallas_call(
        functools.partial(_filter_mlp_kernel, length=length, row_mul=row_mul, step_mul=step_mul),
        grid=(steps,),
        in_specs=[pl.BlockSpec((1, rows, feats.shape[1]), lambda i: (i, 0, 0)), _full_spec(w1.shape),
                  _full_spec((1, nh)), _full_spec((1, nh)), _full_spec(w2.shape), _full_spec((1, nh)),
                  _full_spec((1, nh)), _full_spec(w3.shape), _full_spec((1, n))],
        out_specs=[out_spec, _full_spec((1, n))],
        out_shape=[jax.ShapeDtypeStruct(out_shape, F32), jax.ShapeDtypeStruct((1, n), F32)],
        compiler_params=_cparams(("arbitrary",)),
        name="filter_taps",
    )(feats3, w1, b1, f1, w2, b2, f2, w3, deltas4)


def _fft_stage1_kernel(z_ref, m_ref, a_ref):
    a_ref[...] = _dot(m_ref[0], z_ref[...].astype(BF16)).astype(a_ref.dtype)


def fft_stage1(z_view, mats, n_pairs, n2, lane_stride, lane_off, width, out_dtype=BF16):
    rows_out, rows_in = mats.shape[1:]
    return pl.pallas_call(
        _fft_stage1_kernel,
        grid=(n_pairs, n2),
        in_specs=[pl.BlockSpec((rows_in, width), lambda p, s: (p, s * lane_stride + lane_off)),
                  pl.BlockSpec((1, rows_out, rows_in), lambda p, s: (s, 0, 0))],
        out_specs=pl.BlockSpec((rows_out, width), lambda p, s: (p, s)),
        out_shape=jax.ShapeDtypeStruct((n_pairs * rows_out, n2 * width), out_dtype),
        compiler_params=_cparams(("arbitrary", "arbitrary")),
        name="fft_stage1",
    )(z_view, mats)


def _unfold_rows(a_ref, n2, width, lane_stride, lane_off):
    return jnp.concatenate(
        [a_ref[:, (s * lane_stride + lane_off) * width:(s * lane_stride + lane_off + 1) * width] for s in range(n2)],
        axis=0)


def _fft_mid_kernel(a_ref, h_ref, bf_ref, bi_ref, o_ref, *, n2, width):
    g = a_ref.shape[0]
    x = _dot(bf_ref[...], _unfold_rows(a_ref, n2, width, 1, 0))
    half = x.shape[0] // 2
    xr, xi = x[:half], x[half:]
    hr, hi = h_ref[:half, :], h_ref[half:, :]
    y = jnp.concatenate([xr * hr - xi * hi, xr * hi + xi * hr], axis=0).astype(BF16)
    b = _dot(bi_ref[...], y).astype(o_ref.dtype)
    for s in range(n2):
        o_ref[:, s * width:(s + 1) * width] = b[s * g:(s + 1) * g]


def fft_mid(a, h, h_col, bd_fwd, bd_inv, n_pairs, n_groups, n2, width):
    rows = bd_fwd.shape[0]
    g = rows // n2
    return pl.pallas_call(
        functools.partial(_fft_mid_kernel, n2=n2, width=width),
        grid=(n_groups, n_pairs),
        in_specs=[pl.BlockSpec((g, n2 * width), lambda q, p: (p * n_groups + q, 0)),
                  pl.BlockSpec((rows, width), lambda q, p: (q, h_col)),
                  _full_spec(bd_fwd.shape), _full_spec(bd_inv.shape)],
        out_specs=pl.BlockSpec((g, n2 * width), lambda q, p: (p * n_groups + q, 0)),
        out_shape=jax.ShapeDtypeStruct(a.shape, BF16),
        compiler_params=_cparams(("arbitrary", "arbitrary")),
        name="fft_mid",
    )(a, h, bd_fwd, bd_inv)


def _fft_spec_kernel(a_ref, bf_ref, nrm_ref, h_ref, *, n2, width, n_orders):
    o = pl.program_id(1)
    for order in range(n_orders):
        @pl.when(o == order)
        def _():
            xf = _dot(bf_ref[...], _unfold_rows(a_ref, n2, width, 2 * n_orders, order))
            xb = _dot(bf_ref[...], _unfold_rows(a_ref, n2, width, 2 * n_orders, n_orders + order))
            half = xf.shape[0] // 2
            inv = 1.0 / nrm_ref[...]
            h_ref[:half, :] = (xf[:half] + xb[:half]) * inv
            h_ref[half:, :] = (xf[half:] - xb[half:]) * inv


def fft_filter_spectrum(a, bd_fwd, norm, n_groups, n2, width, n_orders):
    rows = bd_fwd.shape[0]
    g = rows // n2
    return pl.pallas_call(
        functools.partial(_fft_spec_kernel, n2=n2, width=width, n_orders=n_orders),
        grid=(n_groups, n_orders),
        in_specs=[pl.BlockSpec((g, n2 * 2 * n_orders * width), lambda q, o: (q, 0)),
                  _full_spec(bd_fwd.shape),
                  pl.BlockSpec((1, width), lambda q, o: (0, o))],
        out_specs=pl.BlockSpec((rows, width), lambda q, o: (q, o)),
        out_shape=jax.ShapeDtypeStruct((n_groups * rows, n_orders * width), F32),
        compiler_params=_cparams(("arbitrary", "arbitrary")),
        name="fft_filter_spectrum",
    )(a, bd_fwd, norm)


def _fft_inv_kernel(b_ref, m_ref, z_ref, gate_ref, bias_ref, o_ref):
    y = _dot(m_ref[0], b_ref[...])
    o_ref[...] = gate_ref[...] * (y + z_ref[...] * bias_ref[...])


def fft_inverse_gate(b, mats, z_view, z_stride, z_off, gate_view, g_stride, g_off, bias, n_pairs, n2, width,
                     out_rows):
    rows_out, rows_in = mats.shape[1:]
    return pl.pallas_call(
        _fft_inv_kernel,
        grid=(n_pairs, n2),
        in_specs=[pl.BlockSpec((rows_in, width), lambda p, s: (p, s)),
                  pl.BlockSpec((1, rows_out, rows_in), lambda p, s: (s, 0, 0)),
                  pl.BlockSpec((rows_out, width), lambda p, s: (p, s * z_stride + z_off)),
                  pl.BlockSpec((rows_out, width), lambda p, s: (p, s * g_stride + g_off)),
                  _full_spec((1, width))],
        out_specs=pl.BlockSpec((rows_out, width), lambda p, s: (p, s)),
        out_shape=jax.ShapeDtypeStruct((out_rows, n2 * width), F32),
        compiler_params=_cparams(("arbitrary", "arbitrary")),
        name="fft_inverse_gate",
    )(b, mats, z_view, gate_view, bias)


def _ctx_spec_kernel(hf_ref, hb_ref, f_ref, nrm_ref, h_ref):
    half = f_ref.shape[0] // 2
    xf = _dot3(f_ref[...], hf_ref[...])
    xb = _dot3(f_ref[...], hb_ref[...])
    inv = 1.0 / nrm_ref[...]
    h_ref[:half, :] = (xf[:half] + xb[:half]) * inv
    h_ref[half:, :] = (xf[half:] - xb[half:]) * inv


def ctx_filter_spectrum(taps, f_real, norm, width, n_orders):
    length = taps.shape[0]
    rows = f_real.shape[0]
    return pl.pallas_call(
        _ctx_spec_kernel,
        grid=(n_orders,),
        in_specs=[pl.BlockSpec((length, width), lambda o: (0, o)),
                  pl.BlockSpec((length, width), lambda o: (0, n_orders + o)),
                  _full_spec(f_real.shape),
                  pl.BlockSpec((1, width), lambda o: (0, o))],
        out_specs=pl.BlockSpec((rows, width), lambda o: (0, o)),
        out_shape=jax.ShapeDtypeStruct((rows, n_orders * width), F32),
        compiler_params=_cparams(("arbitrary",)),
        name="ctx_filter_spectrum",
    )(taps, taps, f_real, norm)


def _ctx_conv_kernel(u_ref, h_ref, ff_ref, fi_ref, bias_ref, zin_ref, o_ref, nat_ref, z_ref, *, width, fold):
    del zin_ref
    half = ff_ref.shape[0] // 2
    _unfold_to_rows(u_ref, nat_ref, fold)
    wt = width // LANES
    z = _load_tiles(nat_ref, 0, wt)
    for n in range(2):
        x = _dot(ff_ref[...], z.astype(BF16))
        xr, xi = x[:half], x[half:]
        hr = h_ref[:half, n * width:(n + 1) * width]
        hi = h_ref[half:, n * width:(n + 1) * width]
        y = jnp.concatenate([xr * hr - xi * hi, xr * hi + xi * hr], axis=0).astype(BF16)
        y = _dot(fi_ref[...], y)
        z = _load_tiles(nat_ref, (n + 1) * wt, (n + 2) * wt) * (y + z * bias_ref[n:n + 1, :])
    _store_tiles(z_ref, z)
    _fold_rows(z_ref, o_ref, fold)


def ctx_long_conv(cfg, u_f, h_ctx, f_fwd, f_inv, bias, z_out):
    d = cfg.d
    n2 = cfg.fft_n2
    rows = 2 * cfg.ctx
    g = rows // n2
    base = cfg.nlat // rows
    return pl.pallas_call(
        functools.partial(_ctx_conv_kernel, width=d, fold=n2),
        grid=(cfg.batch // 2,),
        in_specs=[pl.BlockSpec((g, n2 * 3 * d), lambda p: (base + p, 0)),
                  _full_spec(h_ctx.shape), _full_spec(f_fwd.shape), _full_spec(f_inv.shape),
                  _full_spec(bias.shape), pl.BlockSpec(memory_space=pl.ANY)],
        out_specs=pl.BlockSpec((g, n2 * d), lambda p: (base + p, 0)),
        out_shape=jax.ShapeDtypeStruct(z_out.shape, F32),
        scratch_shapes=[_lane_tiles(rows, 3 * d), _lane_tiles(rows, d)],
        input_output_aliases={5: 0},
        compiler_params=_cparams(("arbitrary",)),
        name="ctx_long_conv",
    )(u_f, h_ctx, f_fwd, f_inv, bias, z_out)


def _dft_tables(n1, n2, group):
    n = n1 * n2
    h1 = n1 // 2
    f1 = jnp.arange(n1, dtype=jnp.int32)[None, :, None]
    s1 = jnp.arange(h1, dtype=jnp.int32)[None, None, :]
    s2 = jnp.arange(n2, dtype=jnp.int32)[:, None, None]
    k = (f1 * (n2 * s1 + s2)) % n
    ang = (2.0 * math.pi / n) * k.astype(F32)
    er, ei = jnp.cos(ang), -jnp.sin(ang)
    fwd = jnp.concatenate([jnp.concatenate([er, -ei], axis=2),
                           jnp.concatenate([ei, er], axis=2)], axis=1)
    cr, ci = jnp.swapaxes(er, 1, 2) / n, -jnp.swapaxes(ei, 1, 2) / n
    inv = jnp.concatenate([jnp.concatenate([cr, -ci], axis=2),
                           jnp.concatenate([ci, cr], axis=2)], axis=1)
    q = jnp.arange(n1 // group)[:, None, None]
    ri = jnp.arange(2)[None, :, None]
    j = jnp.arange(group)[None, None, :]
    perm = (ri * n1 + q * group + j).reshape(-1)
    fwd = fwd[:, perm, :]
    inv = inv[:, :, perm]
    real_only = fwd[:, :, :h1]
    a = jnp.arange(n2, dtype=jnp.int32)
    ang2 = (2.0 * math.pi / n2) * ((a[:, None] * a[None, :]) % n2).astype(F32)
    f2r, f2i = jnp.cos(ang2), -jnp.sin(ang2)
    eye = jnp.eye(group, dtype=F32)
    t_fwd = jnp.stack([jnp.stack([f2r, -f2i]), jnp.stack([f2i, f2r])])
    t_inv = jnp.stack([jnp.stack([f2r, f2i]), jnp.stack([-f2i, f2r])])
    size = 2 * group * n2
    bd_fwd = jnp.einsum("abfs,jk->afjsbk", t_fwd, eye).reshape(size, size)
    bd_inv = jnp.einsum("absf,jk->sajbfk", t_inv, eye).reshape(size, size)
    return (fwd.astype(BF16), inv.astype(BF16), real_only.astype(BF16), bd_fwd.astype(BF16),
            bd_inv.astype(BF16))


def _ctx_dft_tables(length):
    n = 2 * length
    f = jnp.arange(n, dtype=jnp.int32)[:, None]
    s = jnp.arange(length, dtype=jnp.int32)[None, :]
    ang = (2.0 * math.pi / n) * ((f * s) % n).astype(F32)
    fr, fi = jnp.cos(ang), -jnp.sin(ang)
    fwd = jnp.concatenate([jnp.concatenate([fr, -fi], axis=1), jnp.concatenate([fi, fr], axis=1)], axis=0)
    frt, fit = fr.T / n, fi.T / n
    inv = jnp.concatenate([jnp.concatenate([frt, fit], axis=1), jnp.concatenate([-fit, frt], axis=1)], axis=0)
    real_only = jnp.concatenate([fr, fi], axis=0)
    return fwd.astype(BF16), inv.astype(BF16), real_only


def _filter_features(length, bands):
    pos = jnp.arange(length, dtype=F32)[:, None]
    t = pos / (length - 1)
    w = 2.0 * math.pi * pos / length
    bnd = jnp.linspace(1e-4, bands - 1, bands, dtype=F32)
    feats = jnp.concatenate([t, jnp.cos(bnd * w), -jnp.sin(bnd * w)], axis=-1)
    return jnp.pad(feats, ((0, 0), (0, 128 - feats.shape[1])))


def hyena_long_convs(cfg, u, fp, long_bias):
    d = cfg.d
    n1, n2 = cfg.fft_n1, cfg.fft_n2
    group = 8
    n_groups = n1 // group
    n_orders = 2
    n_pairs = cfg.batch // 2
    nrow = cfg.nt // n2
    (w1, b1, fq1, w2, b2, fq2, w3) = fp
    nh = w1.shape[1]
    w1p = jnp.pad(w1, ((0, 128 - w1.shape[0]), (0, 0)))
    bands = (w1.shape[0] - 1) // 2
    deltas = jnp.linspace(math.log(1e-2) / 1.5, math.log(1e-2) / 0.3, d, dtype=F32)
    deltas4 = jnp.tile(deltas, 2 * n_orders)[None, :]
    mlp = (w1p, b1.reshape(1, nh), fq1.reshape(1, nh), w2, b2.reshape(1, nh), fq2.reshape(1, nh), w3, deltas4)

    fwd, inv, real_only, bd_fwd, bd_inv = _dft_tables(n1, n2, group)

    taps, sums = filter_taps(cfg.seq, n2, _filter_features(cfg.seq, bands), *mlp)
    norm = sums[:, :n_orders * d] + sums[:, n_orders * d:]
    a_f = fft_stage1(taps, real_only, 1, n2, 1, 0, 2 * n_orders * d)
    h_lat = fft_filter_spectrum(a_f, bd_fwd, norm, n_groups, n2, d, n_orders)

    cf_fwd, cf_inv, cf_real = _ctx_dft_tables(cfg.ctx)
    taps_c, sums_c = filter_taps(cfg.ctx, 1, _filter_features(cfg.ctx, bands), *mlp)
    norm_c = sums_c[:, :n_orders * d] + sums_c[:, n_orders * d:]
    h_ctx = ctx_filter_spectrum(taps_c, cf_real, norm_c, d, n_orders)

    u_view = u
    z_view, z_stride, z_off = u_view, 3, 0
    z = None
    for order in range(n_orders):
        a = fft_stage1(z_view, fwd, n_pairs, n2, z_stride, z_off, d)
        bmid = fft_mid(a, h_lat, order, bd_fwd, bd_inv, n_pairs, n_groups, n2, d)
        z = fft_inverse_gate(bmid, inv, z_view, z_stride, z_off, u_view, 3, 1 + order,
                             long_bias[order:order + 1], n_pairs, n2, d, nrow)
        z_view, z_stride, z_off = z, 1, 0
    return ctx_long_conv(cfg, u, h_ctx, cf_fwd, cf_inv, long_bias, z)


META_E, META_W, META_LP = 0, 2, 4
CHUNK = 8
TAB_N, TAB_SRC, TAB_DST, TAB_TOTAL = 0, 32, 64, 96


def _local_rows(cfg):
    ne = cfg.n_groups * cfg.epg
    return -(-(2 * cfg.tb + (CHUNK - 1) * ne) // 128) * 128


def _route_kernel(x_ref, g_ref, sh_ref, sc_ref, wr_ref, br_ref, h_ref, meta_ref, tab_ref, cnt_ref, run_ref, *,
                  cfg):
    ne = cfg.n_groups * cfg.epg
    step = pl.program_id(0)
    h = _modnorm(x_ref[...], g_ref[...], sh_ref[0], sc_ref[0])
    h_ref[...] = h.astype(h_ref.dtype)
    logits = _dot3(h, wr_ref[...]) + br_ref[...]
    lane = lax.broadcasted_iota(jnp.int32, logits.shape, 1).astype(F32)
    neg = -jnp.inf
    big = 1e9

    def first_argmax(mask):
        v = jnp.where(mask, logits, neg)
        mx = jnp.max(v, axis=-1, keepdims=True)
        idx = jnp.min(jnp.where(mask & (logits == mx), lane, big), axis=-1, keepdims=True)
        return mx, idx

    gmask = (lane >= ne) & (lane < ne + cfg.n_groups)
    gmax, gidx = first_argmax(gmask)
    g_p = 1.0 / jnp.sum(jnp.where(gmask, jnp.exp(logits - gmax), 0.0), axis=-1, keepdims=True)
    e0 = (gidx - ne) * cfg.epg
    emask = (lane >= e0) & (lane < e0 + cfg.epg)
    m1, i1 = first_argmax(emask)
    m2, i2 = first_argmax(emask & (lane != i1))
    r = jnp.exp(m2 - m1)
    w1 = g_p / (1.0 + r)
    w2 = g_p * r / (1.0 + r)

    @pl.when(step == 0)
    def _():
        run_ref[...] = jnp.zeros_like(run_ref)

    hit1 = lane == i1
    hit2 = lane == i2
    onehot = jnp.where(hit1 | hit2, 1.0, 0.0)
    tb = onehot.shape[0]
    row = lax.broadcasted_iota(jnp.int32, (tb, tb), 0)
    col = lax.broadcasted_iota(jnp.int32, (tb, tb), 1)
    earlier = jnp.where(col < row, 1.0, 0.0).astype(BF16)
    before = _dot(earlier, onehot.astype(BF16))
    chunks = jnp.floor((jnp.sum(onehot, axis=0, keepdims=True) + (CHUNK - 1)) * (1.0 / CHUNK))
    ea = lax.broadcasted_iota(jnp.int32, (128, 128), 0)
    eb = lax.broadcasted_iota(jnp.int32, (128, 128), 1)
    lower_experts = jnp.where(ea < eb, 1.0, 0.0).astype(BF16)
    lstart = CHUNK * _dot(jnp.broadcast_to(chunks, (8, 128)).astype(BF16), lower_experts)[0:1]
    local = before + lstart
    lp1 = jnp.sum(jnp.where(hit1, local, 0.0), axis=-1, keepdims=True)
    lp2 = jnp.sum(jnp.where(hit2, local, 0.0), axis=-1, keepdims=True)

    tab = jnp.zeros((8, 128), F32)
    rows8 = lax.broadcasted_iota(jnp.int32, (8, 128), 0)
    for k, v in enumerate((chunks, lstart, run_ref[...])):
        tab = jnp.where(rows8 == k, v, tab)
    tab_ref[0] = tab
    run_ref[...] = run_ref[...] + chunks
    cnt_ref[...] = run_ref[...]

    meta = jnp.zeros_like(logits)
    for k, v in enumerate((i1, i2, w1, w2, lp1, lp2)):
        meta = jnp.where(lane == float(k), v, meta)
    meta_ref[...] = meta


def route(cfg, x, norm_g, mod, w_route, b_route, n_blocks):
    return pl.pallas_call(
        functools.partial(_route_kernel, cfg=cfg),
        grid=(n_blocks,),
        in_specs=[_row_spec(cfg, cfg.d), _full_spec((1, cfg.d)), _mod_spec(cfg, 3), _mod_spec(cfg, 4),
                  _full_spec(w_route.shape), _full_spec((1, 128))],
        out_specs=[_row_spec(cfg, cfg.d), _row_spec(cfg, 128), pl.BlockSpec((1, 8, 128), lambda j: (j, 0, 0)),
                   _full_spec((1, 128))],
        out_shape=[jax.ShapeDtypeStruct((n_blocks * cfg.tb, cfg.d), BF16),
                   jax.ShapeDtypeStruct((n_blocks * cfg.tb, 128), F32),
                   jax.ShapeDtypeStruct((n_blocks, 8, 128), F32),
                   jax.ShapeDtypeStruct((1, 128), F32)],
        scratch_shapes=[pltpu.VMEM((1, 128), F32)],
        compiler_params=_cparams(("arbitrary",)),
        name="route",
    )(x, norm_g, mod, mod, w_route, b_route)


def _chunk_copies(hbm, tab_ref, vmem, sem, n_experts, to_hbm):
    def copy(e, k):
        src = pl.multiple_of(tab_ref[0, 0, TAB_SRC + e] + k * CHUNK, CHUNK)
        dst = pl.multiple_of(tab_ref[0, 0, TAB_DST + e] + k * CHUNK, CHUNK)
        v = vmem.at[pl.ds(src, CHUNK), :]
        h = hbm.at[pl.ds(dst, CHUNK), :]
        return pltpu.make_async_copy(v, h, sem) if to_hbm else pltpu.make_async_copy(h, v, sem)

    def each(fn):
        for e in range(n_experts):
            def body(k, carry, e=e):
                fn(copy(e, k))
                return carry
            lax.fori_loop(0, tab_ref[0, 0, TAB_N + e], body, 0)

    return (lambda: each(lambda c: c.start())), (lambda: each(lambda c: c.wait()))


def _dispatch_kernel(tab_ref, meta_ref, h_ref, hs_hbm, sorted_ref, sem, *, cfg):
    ne = cfg.n_groups * cfg.epg
    lr = sorted_ref.shape[0]
    mt = meta_ref[...].T
    lp1 = mt[META_LP:META_LP + 1, :]
    lp2 = mt[META_LP + 1:META_LP + 2, :]
    row = lax.broadcasted_iota(jnp.int32, (lr, cfg.tb), 0).astype(F32)
    perm = jnp.where((row == lp1) | (row == lp2), 1.0, 0.0).astype(BF16)
    sorted_ref[...] = _dot(perm, h_ref[...].astype(BF16))
    start, wait = _chunk_copies(hs_hbm, tab_ref, sorted_ref, sem.at[0], ne, to_hbm=True)
    start()
    wait()


def dispatch_rows(cfg, h, meta, tab, n_blocks, nblk):
    tb = cfg.tb
    return pl.pallas_call(
        functools.partial(_dispatch_kernel, cfg=cfg),
        grid=(n_blocks,),
        in_specs=[pl.BlockSpec((1, 1, 128), lambda j: (j, 0, 0), memory_space=pltpu.SMEM),
                  _row_spec(cfg, 128), _row_spec(cfg, cfg.d)],
        out_specs=pl.BlockSpec(memory_space=pl.ANY),
        out_shape=jax.ShapeDtypeStruct((nblk * tb, cfg.d), F32),
        scratch_shapes=[pltpu.VMEM((_local_rows(cfg), cfg.d), F32), pltpu.SemaphoreType.DMA((1,))],
        compiler_params=_cparams(("arbitrary",)),
        name="dispatch_rows",
    )(tab, meta, h)


def _expert_kernel(be_ref, valid_ref, nu_ref, h_ref, wg_ref, wu_ref, wd_ref, y_ref):
    del be_ref, nu_ref
    valid = valid_ref[pl.program_id(0)]

    @pl.when(valid > 0)
    def _():
        row = lax.broadcasted_iota(jnp.int32, h_ref.shape, 0)
        hb = jnp.where(row < valid, h_ref[...], 0.0).astype(BF16)
        a = _dot(hb, wg_ref[0].astype(BF16))
        b = _dot(hb, wu_ref[0].astype(BF16))
        hid = (_silu(a) * b).astype(BF16)
        y_ref[...] = _dot(hid, wd_ref[0].astype(BF16))


def expert_mlp(cfg, h_sorted, block_expert, block_valid, n_used, w_gate, w_up, w_down, nblk):
    tb = cfg.tb
    de = cfg.d_exp

    def rows(i, be, bv, nu):
        return (jnp.minimum(i, nu[0] - 1), 0)

    grid_spec = pltpu.PrefetchScalarGridSpec(
        num_scalar_prefetch=3,
        grid=(nblk,),
        in_specs=[pl.BlockSpec((tb, cfg.d), rows),
                  pl.BlockSpec((1, cfg.d, de), lambda i, be, bv, nu: (be[i], 0, 0)),
                  pl.BlockSpec((1, cfg.d, de), lambda i, be, bv, nu: (be[i], 0, 0)),
                  pl.BlockSpec((1, de, cfg.d), lambda i, be, bv, nu: (be[i], 0, 0))],
        out_specs=pl.BlockSpec((tb, cfg.d), rows),
    )
    return pl.pallas_call(
        _expert_kernel,
        grid_spec=grid_spec,
        out_shape=jax.ShapeDtypeStruct((nblk * tb, cfg.d), F32),
        compiler_params=_cparams(("arbitrary",)),
        name="expert_mlp",
    )(block_expert, block_valid, n_used, h_sorted, w_gate, w_up, w_down)


def _combine_kernel(tab_ref, tab_next_ref, x_ref, meta_ref, gate_ref, y_hbm, *rest, cfg, nblk, final):
    if final:
        fg_ref, o_ref, ybuf, sem = rest
    else:
        o_ref, ybuf, sem = rest
    ne = cfg.n_groups * cfg.epg
    i = pl.program_id(0)
    slot = i % 2
    start_cur, wait_cur = _chunk_copies(y_hbm, tab_ref, ybuf.at[slot], sem.at[slot], ne, to_hbm=False)
    start_next, _ = _chunk_copies(y_hbm, tab_next_ref, ybuf.at[1 - slot], sem.at[1 - slot], ne, to_hbm=False)

    @pl.when(i == 0)
    def _():
        start_cur()

    @pl.when(i + 1 < nblk)
    def _():
        start_next()

    wait_cur()
    lr = ybuf.shape[1]
    row = lax.broadcasted_iota(jnp.int32, (lr, 1), 0)
    y = jnp.where(row < tab_ref[0, 0, TAB_TOTAL], ybuf[slot], 0.0).astype(BF16)
    meta = meta_ref[...]
    lane = lax.broadcasted_iota(jnp.int32, (cfg.tb, lr), 1).astype(F32)
    pick1 = jnp.where(lane == meta[:, META_LP:META_LP + 1], 1.0, 0.0).astype(BF16)
    pick2 = jnp.where(lane == meta[:, META_LP + 1:META_LP + 2], 1.0, 0.0).astype(BF16)
    w1 = meta[:, META_W:META_W + 1]
    w2 = meta[:, META_W + 1:META_W + 2]
    out = x_ref[...] + gate_ref[0] * (w1 * _dot(pick1, y) + w2 * _dot(pick2, y))
    if final:
        out = out * lax.rsqrt(jnp.mean(out * out, axis=-1, keepdims=True) + EPS) * fg_ref[...]
    o_ref[...] = out


def combine_residual(cfg, x, y_sorted, tab, meta, mod, gate_chunk, n_blocks, final_g=None):
    tb = cfg.tb
    final = final_g is not None
    extra_specs, extra_args = ([_full_spec((1, cfg.d))], [final_g]) if final else ([], [])
    return pl.pallas_call(
        functools.partial(_combine_kernel, cfg=cfg, nblk=n_blocks, final=final),
        grid=(n_blocks,),
        in_specs=[pl.BlockSpec((1, 1, 128), lambda j: (j, 0, 0), memory_space=pltpu.SMEM),
                  pl.BlockSpec((1, 1, 128), lambda j: (jnp.minimum(j + 1, n_blocks - 1), 0, 0),
                               memory_space=pltpu.SMEM),
                  _row_spec(cfg, cfg.d), _row_spec(cfg, 128), _mod_spec(cfg, gate_chunk),
                  pl.BlockSpec(memory_space=pl.ANY)] + extra_specs,
        out_specs=_row_spec(cfg, cfg.d),
        out_shape=jax.ShapeDtypeStruct((n_blocks * tb, cfg.d), F32),
        scratch_shapes=[pltpu.VMEM((2, _local_rows(cfg), cfg.d), F32), pltpu.SemaphoreType.DMA((2,))],
        compiler_params=_cparams(("arbitrary",)),
        name="combine_residual",
    )(tab, tab, x, meta, mod, y_sorted, *extra_args)


def _rope_tables(cfg):
    hd = cfg.head_dim
    pairs = hd // 4
    rows = cfg.seq // cfg.grid_w
    row = jnp.repeat(jnp.arange(rows), cfg.grid_w).astype(F32)
    col = jnp.tile(jnp.arange(cfg.grid_w), rows).astype(F32)
    inv_freq = 10000.0 ** (-jnp.arange(pairs, dtype=F32) / pairs)
    ar, ac = row[:, None] * inv_freq, col[:, None] * inv_freq
    cos = jnp.concatenate([jnp.cos(ar), jnp.cos(ar), jnp.cos(ac), jnp.cos(ac)], axis=-1)
    sin = jnp.concatenate([-jnp.sin(ar), jnp.sin(ar), -jnp.sin(ac), jnp.sin(ac)], axis=-1)
    cos = jnp.concatenate([cos, jnp.ones((cfg.tb, hd), F32)], axis=0)
    sin = jnp.concatenate([sin, jnp.zeros((cfg.tb, hd), F32)], axis=0)
    return cos, sin


def _moe_layer(cfg, x, norm_g, mod, w_group, b_group, w_router, b_router, w_gate, w_up, w_down, layer, n_blocks,
               final_g=None):
    ne = cfg.n_groups * cfg.epg
    w_route = jnp.pad(jnp.concatenate([w_router, w_group], axis=1), ((0, 0), (0, 128 - ne - cfg.n_groups)))
    b_route = jnp.pad(jnp.concatenate([b_router, b_group]), (0, 128 - ne - cfg.n_groups)).reshape(1, 128)
    h, meta, tabs, totals = route(cfg, x, norm_g, mod, w_route, b_route, n_blocks)

    tb = cfg.tb
    nblk = -(-(2 * n_blocks * tb + (CHUNK - 1) * ne * n_blocks) // tb) + ne
    used = CHUNK * totals[0, :ne].astype(jnp.int32)
    padded = ((used + tb - 1) // tb) * tb
    seg_end = jnp.cumsum(padded)
    seg_start = seg_end - padded
    t = tabs.astype(jnp.int32)
    n_chunks, src0, before = t[:, 0, :ne], t[:, 1, :ne], t[:, 2, :ne]
    dst0 = seg_start[None, :] + CHUNK * before
    local_total = CHUNK * jnp.sum(n_chunks, axis=1, keepdims=True)
    tab = jnp.concatenate([n_chunks, src0, dst0, local_total,
                           jnp.zeros((n_blocks, 128 - 3 * ne - 1), jnp.int32)], axis=1).reshape(n_blocks, 1, 128)
    blk_row = jnp.arange(nblk, dtype=jnp.int32) * tb
    block_expert = jnp.minimum(jnp.sum(seg_end[None, :] <= blk_row[:, None], axis=-1), ne - 1).astype(jnp.int32)
    block_valid = jnp.clip(seg_start[block_expert] + used[block_expert] - blk_row, 0, tb).astype(jnp.int32)
    n_used = (seg_end[ne - 1:ne] // tb).astype(jnp.int32)

    h_sorted = dispatch_rows(cfg, h, meta, tab, n_blocks, nblk)
    y = expert_mlp(cfg, h_sorted, block_expert + layer * ne, block_valid, n_used,
                   w_gate.reshape(-1, cfg.d, cfg.d_exp), w_up.reshape(-1, cfg.d, cfg.d_exp),
                   w_down.reshape(-1, cfg.d_exp, cfg.d), nblk)
    return combine_residual(cfg, x, y, tab, meta, mod, 5, n_blocks, final_g)


def _forward(cfg, x, c, ctx, c_ctx, w_mod, b_mod, norm_mix_g, norm_ffn_g,
             attn_w_q, attn_w_kv, attn_q_gain, attn_k_gain, attn_w_o,
             conv_w_pw1, conv_b_pw1, conv_w_dw, conv_b_dw, conv_ln_g, conv_ln_b, conv_w_pw2, conv_b_pw2,
             hy_w_in, hy_b_in, hy_w_short, hy_b_short, hy_f_w1, hy_f_b1, hy_f_freq1, hy_f_w2, hy_f_b2,
             hy_f_freq2, hy_f_w3, hy_long_bias, hy_w_out, hy_b_out,
             moe_w_group, moe_b_group, moe_w_router, moe_b_router, moe_w_gate, moe_w_up, moe_w_down,
             final_norm_g):
    d = cfg.d
    depth = w_mod.shape[0]
    xs = jnp.concatenate([x.reshape(-1, d), ctx.reshape(-1, d)], axis=0)
    c_all = jnp.concatenate([c, c_ctx[None, :], jnp.zeros((MOD_ROWS - cfg.batch - 1, d), F32)], axis=0)
    mods = modulation(c_all, w_mod, b_mod)
    rope_cos, rope_sin = _rope_tables(cfg)
    zero_bias = jnp.zeros((1, d), F32)

    for i in range(depth):
        kind, slot = i % 3, i // 3
        last = i == depth - 1
        n_blocks = cfg.nbl if last else cfg.nb
        mod = mods[i].reshape(MOD_ROWS * 6, 1, d)
        g_mix = norm_mix_g[i].reshape(1, d)
        if kind == 0:
            w_qkv = jnp.concatenate([attn_w_q[slot], attn_w_kv[slot]], axis=1).astype(BF16)
            qkv = qkv_project(cfg, xs, g_mix, mod, w_qkv, rope_cos, rope_sin,
                              attn_q_gain[slot].reshape(1, -1), attn_k_gain[slot].reshape(1, -1))
            o = attention(cfg, qkv)
            xs = project_residual(cfg, o, attn_w_o[slot].astype(BF16), zero_bias, xs, mod, 2, n_blocks)
        elif kind == 1:
            u = glu_project(cfg, xs, g_mix, mod, conv_w_pw1[slot].astype(BF16), conv_b_pw1[slot].reshape(1, -1))
            xs = conv_module(cfg, u, conv_w_dw[slot], conv_b_dw[slot].reshape(1, d), conv_ln_g[slot].reshape(1, d),
                             conv_ln_b[slot].reshape(1, d), conv_w_pw2[slot].astype(BF16),
                             conv_b_pw2[slot].reshape(1, d), xs, mod)
        else:
            p = in_project(cfg, xs, g_mix, mod, hy_w_in[slot].astype(BF16), hy_b_in[slot].reshape(1, -1))
            u = short_conv(cfg, p, hy_w_short[slot], hy_b_short[slot].reshape(1, -1))
            fp = (hy_f_w1[slot], hy_f_b1[slot], hy_f_freq1[slot], hy_f_w2[slot], hy_f_b2[slot],
                  hy_f_freq2[slot], hy_f_w3[slot])
            z2 = hyena_long_convs(cfg, u, fp, hy_long_bias[slot])
            xs = project_residual(cfg, z2, hy_w_out[slot].astype(BF16), hy_b_out[slot].reshape(1, d), xs, mod, 2,
                                  n_blocks, fold=cfg.fft_n2)
        xs = _moe_layer(cfg, xs, norm_ffn_g[i].reshape(1, d), mod, moe_w_group[i], moe_b_group[i],
                        moe_w_router[i], moe_b_router[i], moe_w_gate, moe_w_up, moe_w_down, i, n_blocks,
                        final_norm_g.reshape(1, d) if last else None)

    return xs.reshape(cfg.batch, cfg.seq, d)


def kernel(x, c, ctx, c_ctx, w_mod, b_mod, norm_mix_g, norm_ffn_g, attn_w_q, attn_w_kv, attn_q_gain, attn_k_gain, attn_w_o, conv_w_pw1, conv_b_pw1, conv_w_dw, conv_b_dw, conv_ln_g, conv_ln_b, conv_w_pw2, conv_b_pw2, hy_w_in, hy_b_in, hy_w_short, hy_b_short, hy_f_w1, hy_f_b1, hy_f_freq1, hy_f_w2, hy_f_b2, hy_f_freq2, hy_f_w3, hy_long_bias, hy_w_out, hy_b_out, moe_w_group, moe_b_group, moe_w_router, moe_b_router, moe_w_gate, moe_w_up, moe_w_down, final_norm_g):
    return _forward(CFG, x, c, ctx, c_ctx, w_mod, b_mod, norm_mix_g, norm_ffn_g, attn_w_q, attn_w_kv, attn_q_gain, attn_k_gain, attn_w_o, conv_w_pw1, conv_b_pw1, conv_w_dw, conv_b_dw, conv_ln_g, conv_ln_b, conv_w_pw2, conv_b_pw2, hy_w_in, hy_b_in, hy_w_short, hy_b_short, hy_f_w1, hy_f_b1, hy_f_freq1, hy_f_w2, hy_f_b2, hy_f_freq2, hy_f_w3, hy_long_bias, hy_w_out, hy_b_out, moe_w_group, moe_b_group, moe_w_router, moe_b_router, moe_w_gate, moe_w_up, moe_w_down, final_norm_g)
```
